```python
import jax, jax.numpy as jnp
from jax import lax
import numpy as np

D_MODEL = 1024
BATCH = 4
SEQ = 4096
DEPTH = 2

GRID_W = 64
CTX_LEN = 256
D_MIX = D_MODEL
RMS_EPS = 1e-6
A_WIDTH = D_MIX // 4
A_HEADS = 4
A_HD = A_WIDTH // A_HEADS
A_CONV = 4
LRU_C = 8.0
A_COLS = 2 * A_WIDTH
B_WIDTH = D_MIX // 4
B_HD = 64
B_HEADS = B_WIDTH // B_HD
DECAY_RANK = 64
ICLR_RANK = 64
GATE_RANK = 64
GN_EPS = 64e-5
B_COLS = 3 * B_WIDTH + DECAY_RANK + ICLR_RANK + GATE_RANK
B_SPLITS = (B_WIDTH, 2 * B_WIDTH, 3 * B_WIDTH, 3 * B_WIDTH + DECAY_RANK,
            3 * B_WIDTH + DECAY_RANK + ICLR_RANK)
C_WIDTH = D_MIX // 2
C_HD = 64
C_HEADS = C_WIDTH // C_HD
WIN_H = 8
WIN_W = 16
C_COLS = 3 * C_WIDTH
IN_COLS = A_COLS + B_COLS + C_COLS
N_EXPERTS = 32
TOP_K = 4
D_FF = D_MODEL
SWIGLU_ALPHA = 1.702
SWIGLU_LIMIT = 7.0
MOE_BLOCK = 128

kernel_name = 'hybrid_lru_rwkv7_natten_moe_dit'


def _rmsnorm(x, g):
    xf = x.astype(jnp.float32)
    y = xf * lax.rsqrt(jnp.mean(xf * xf, axis=-1, keepdims=True) + RMS_EPS)
    return (y * g.astype(jnp.float32)).astype(x.dtype)


def _modulate(h, shift, scale):
    return h * (1 + scale) + shift


def _dir_conv(x, w, b, reverse):
    t = x.shape[1]
    kw = w.shape[0]
    if reverse:
        xp = jnp.pad(x, ((0, 0), (0, kw - 1), (0, 0)))
        taps = [xp[:, j:j + t] for j in range(kw)]
    else:
        xp = jnp.pad(x, ((0, 0), (kw - 1, 0), (0, 0)))
        taps = [xp[:, kw - 1 - j:kw - 1 - j + t] for j in range(kw)]
    return sum(w[j] * taps[j] for j in range(kw)) + b


def _lin_scan(a, b, h0, reverse):
    def combine(e1, e2):
        a1, b1 = e1
        a2, b2 = e2
        return a1 * a2, a2 * b1 + b2
    a_cum, h = lax.associative_scan(combine, (a, b), reverse=reverse, axis=1)
    return h + a_cum * h0[:, None, :]


def _rglru(xc, wr, br, wi, bi, lam, h0, reverse):
    bsz, t, _ = xc.shape
    xf = xc.astype(jnp.float32)
    xh = xf.reshape(bsz, t, A_HEADS, A_HD)
    gate_r = jax.nn.sigmoid(jnp.einsum('bthi,hij->bthj', xh, wr.astype(jnp.float32)).reshape(bsz, t, A_WIDTH) + br)
    gate_i = jax.nn.sigmoid(jnp.einsum('bthi,hij->bthj', xh, wi.astype(jnp.float32)).reshape(bsz, t, A_WIDTH) + bi)
    log_a = -LRU_C * gate_r * jax.nn.softplus(-lam.astype(jnp.float32))
    a = jnp.exp(log_a)
    b = jnp.sqrt(-jnp.expm1(2.0 * log_a)) * (gate_i * xf)
    return _lin_scan(a, b, h0, reverse)


def _lru_mixer(pc, pl, conv_w, conv_b, wr, br, wi, bi, lam, need_ctx_out):
    xc, gc = pc[..., :A_WIDTH], pc[..., A_WIDTH:]
    xl, gl = pl[..., :A_WIDTH], pl[..., A_WIDTH:]
    h0 = jnp.zeros((pc.shape[0], A_WIDTH), jnp.float32)
    hc_dirs, hl_dirs = [], []
    for d, rev in enumerate((False, True)):
        hc = _rglru(_dir_conv(xc, conv_w[d], conv_b[d], rev), wr[d], br[d], wi[d], bi[d], lam[d], h0, rev)
        h_end = hc[:, 0] if rev else hc[:, -1]
        hl = _rglru(_dir_conv(xl, conv_w[d], conv_b[d], rev), wr[d], br[d], wi[d], bi[d], lam[d], h_end, rev)
        hc_dirs.append(hc)
        hl_dirs.append(hl)
    out_l = (jax.nn.gelu(gl.astype(jnp.float32)) * (hl_dirs[0] + hl_dirs[1])).astype(pl.dtype)
    out_c = None
    if need_ctx_out:
        out_c = (jax.nn.gelu(gc.astype(jnp.float32)) * (hc_dirs[0] + hc_dirs[1])).astype(pc.dtype)
    return out_c, out_l


def _token_shift(p, mu):
    prev = jnp.pad(p[:, :-1], ((0, 0), (1, 0), (0, 0)))
    nxt = jnp.pad(p[:, 1:], ((0, 0), (0, 1), (0, 0)))
    return p + mu[0] * (prev - p) + mu[1] * (nxt - p)


def _heads(z):
    return z.reshape(z.shape[0], z.shape[1], B_HEADS, B_HD)


def _wkv7_scan(r, w, k, v, a_vec, b_vec, s0, reverse):
    def step(s, inp):
        r_t, w_t, k_t, v_t, a_t, b_t = inp
        sa = jnp.einsum('bhij,bhj->bhi', s, a_t)
        s = s * w_t[:, :, None, :] + sa[..., None] * b_t[:, :, None, :] + v_t[..., None] * k_t[:, :, None, :]
        return s, jnp.einsum('bhij,bhj->bhi', s, r_t)
    xs = tuple(jnp.moveaxis(z, 1, 0) for z in (r, w, k, v, a_vec, b_vec))
    s_end, ys = lax.scan(step, s0, xs, reverse=reverse)
    return jnp.moveaxis(ys, 0, 1), s_end


def _rwkv_dir(r, k, v, wlo, alo, w0, w2, a0, a2, k_k, k_a, s0, reverse):
    w = -jax.nn.softplus(-(w0 + jnp.tanh(wlo) @ w2)) - 0.5
    decay = jnp.exp(-jnp.exp(w))
    a = jax.nn.sigmoid(a0 + alo @ a2)
    kk = _heads(k * k_k)
    kk = kk / jnp.maximum(jnp.linalg.norm(kk, axis=-1, keepdims=True), 1e-12)
    k_eff = _heads(k * (1 + (a - 1) * k_a))
    y, s_end = _wkv7_scan(_heads(r), _heads(decay), k_eff, _heads(v), -kk, kk * _heads(a), s0, reverse)
    return y, s_end, k_eff


def _rwkv_finish(parts, outs, g2, r_k, lnx_g, lnx_b):
    r, v, glo = _heads(parts[0]), _heads(parts[2]), parts[5]
    y = outs[0][0] + outs[1][0]
    mean = jnp.mean(y, axis=-1, keepdims=True)
    var = jnp.mean(jnp.square(y - mean), axis=-1, keepdims=True)
    yn = (y - mean) * lax.rsqrt(var + GN_EPS)
    bonus = sum(jnp.sum(r * o[1] * r_k, axis=-1, keepdims=True) for o in outs)
    bsz, t = y.shape[:2]
    yn = yn.reshape(bsz, t, B_WIDTH) * lnx_g + lnx_b + (bonus * v).reshape(bsz, t, B_WIDTH)
    g = jax.nn.sigmoid(glo) @ g2
    return yn * g


def _rwkv_mixer(pc, pl, mu, w0, w2, a0, a2, g2, k_k, k_a, r_k, lnx_g, lnx_b, need_ctx_out):
    parts_c = jnp.split(_token_shift(pc, mu).astype(jnp.float32), B_SPLITS, axis=-1)
    parts_l = jnp.split(_token_shift(pl, mu).astype(jnp.float32), B_SPLITS, axis=-1)
    s0 = jnp.zeros((pc.shape[0], B_HEADS, B_HD, B_HD), jnp.float32)
    outs_c, outs_l = [], []
    for d, rev in enumerate((False, True)):
        yc, sc_end, kc_eff = _rwkv_dir(*parts_c[:5], w0[d], w2[d], a0[d], a2[d], k_k, k_a, s0, rev)
        yl, _, kl_eff = _rwkv_dir(*parts_l[:5], w0[d], w2[d], a0[d], a2[d], k_k, k_a, sc_end, rev)
        outs_c.append((yc, kc_eff))
        outs_l.append((yl, kl_eff))
    out_l = _rwkv_finish(parts_l, outs_l, g2, r_k, lnx_g, lnx_b).astype(pl.dtype)
    out_c = None
    if need_ctx_out:
        out_c = _rwkv_finish(parts_c, outs_c, g2, r_k, lnx_g, lnx_b).astype(pc.dtype)
    return out_c, out_l


def _na_mixer(pc, pl, rpb, need_ctx_out):
    bsz, ctx_len, _ = pc.shape
    seq = pl.shape[1]
    scale = C_HD ** -0.5
    qc, kc, vc = (z.reshape(bsz, ctx_len, C_HEADS, C_HD) for z in jnp.split(pc, 3, axis=-1))
    out_c = None
    if need_ctx_out:
        s = jnp.einsum('bqhd,bkhd->bhqk', qc * scale, kc).astype(jnp.float32)
        p = jax.nn.softmax(s, axis=-1).astype(vc.dtype)
        out_c = jnp.einsum('bhqk,bkhd->bqhd', p, vc).reshape(bsz, ctx_len, C_WIDTH)
    rows = seq // GRID_W
    kh = min(WIN_H, rows)
    n_win = kh * WIN_W
    qg, kg, vg = (z.reshape(bsz, rows, GRID_W, C_HEADS, C_HD) for z in jnp.split(pl, 3, axis=-1))
    col = jnp.arange(GRID_W)
    col_start = jnp.clip(col - WIN_W // 2, 0, GRID_W - WIN_W)
    cols = col_start[:, None] + jnp.arange(WIN_W)[None, :]
    col_bidx = cols - col[:, None] + (WIN_W - 1)

    def row_block(args):
        r, q_row = args
        rs = jnp.clip(r - kh // 2, 0, rows - kh)
        k_band = lax.dynamic_slice_in_dim(kg, rs, kh, axis=1)
        v_band = lax.dynamic_slice_in_dim(vg, rs, kh, axis=1)
        k_win = k_band[:, :, cols]
        v_win = v_band[:, :, cols]
        row_bidx = rs + jnp.arange(kh) - r + (WIN_H - 1)
        bias = rpb[:, row_bidx[:, None, None], col_bidx[None, :, :]]
        s_win = jnp.einsum('bqhd,biqjhd->bhqij', q_row * scale, k_win).astype(jnp.float32)
        s_win = s_win + jnp.transpose(bias, (0, 2, 1, 3))[None].astype(jnp.float32)
        s_ctx = jnp.einsum('bqhd,bkhd->bhqk', q_row * scale, kc).astype(jnp.float32)
        s = jnp.concatenate([s_win.reshape(bsz, C_HEADS, GRID_W, n_win), s_ctx], axis=-1)
        p = jax.nn.softmax(s, axis=-1).astype(vc.dtype)
        p_win = p[..., :n_win].reshape(bsz, C_HEADS, GRID_W, kh, WIN_W)
        p_ctx = p[..., n_win:]
        return (jnp.einsum('bhqij,biqjhd->bqhd', p_win, v_win)
                + jnp.einsum('bhqk,bkhd->bqhd', p_ctx, vc))

    out = lax.map(row_block, (jnp.arange(rows), jnp.moveaxis(qg, 1, 0)))
    out_l = jnp.moveaxis(out, 0, 1).reshape(bsz, seq, C_WIDTH)
    return out_c, out_l


def _moe(h, router_w, router_b, w_gu, b_gu, w_dn, b_dn):
    n, d = h.shape
    logits = (h @ router_w + router_b).astype(jnp.float32)
    top_val, top_idx = lax.top_k(logits, TOP_K)
    gate = jax.nn.softmax(top_val, axis=-1)
    n_assign = n * TOP_K
    flat_e = top_idx.reshape(-1).astype(jnp.int32)
    flat_tok = jnp.arange(n_assign, dtype=jnp.int32) // TOP_K
    flat_gate = gate.reshape(-1)
    order = jnp.argsort(flat_e)
    e_sorted = flat_e[order]
    counts = jnp.bincount(flat_e, length=N_EXPERTS)
    padded = (counts + MOE_BLOCK - 1) // MOE_BLOCK * MOE_BLOCK
    pad_end = jnp.cumsum(padded)
    pad_start = pad_end - padded
    start = jnp.cumsum(counts) - counts
    dest = pad_start[e_sorted] + jnp.arange(n_assign, dtype=jnp.int32) - start[e_sorted]
    n_blocks = (n_assign + N_EXPERTS * (MOE_BLOCK - 1) + MOE_BLOCK - 1) // MOE_BLOCK
    cap = n_blocks * MOE_BLOCK
    slot_tok = jnp.full((cap,), n, dtype=jnp.int32).at[dest].set(flat_tok[order])
    slot_gate = jnp.zeros((cap,), jnp.float32).at[dest].set(flat_gate[order])
    block_e = jnp.minimum(jnp.searchsorted(pad_end, jnp.arange(n_blocks, dtype=jnp.int32) * MOE_BLOCK, side='right'),
                          N_EXPERTS - 1)
    h_pad = jnp.concatenate([h, jnp.zeros((1, d), h.dtype)], axis=0)
    xb = h_pad[slot_tok].reshape(n_blocks, MOE_BLOCK, d)

    def expert_block(args):
        xblk, e = args
        gu = xblk @ w_gu[e] + b_gu[e]
        g_, u_ = gu[:, :D_FF], gu[:, D_FF:]
        g_ = jnp.minimum(g_, SWIGLU_LIMIT)
        u_ = jnp.clip(u_, -SWIGLU_LIMIT, SWIGLU_LIMIT)
        act = (u_ + 1) * (g_ * jax.nn.sigmoid(SWIGLU_ALPHA * g_))
        return act @ w_dn[e] + b_dn[e]

    yb = lax.map(expert_block, (xb, block_e)).reshape(cap, d)
    out = jnp.zeros((n + 1, d), yb.dtype).at[slot_tok].add(yb * slot_gate[:, None].astype(yb.dtype))
    return out[:n]


def setup_inputs(seed: int = 0) -> dict:
    key = jax.random.key(seed)
    keys = iter(jax.random.split(key, 40))

    def nrm(shape, scale):
        return jax.random.normal(next(keys), shape, jnp.float32) * scale

    def unif(shape, lo, hi):
        return jax.random.uniform(next(keys), shape, jnp.float32, lo, hi)

    L = DEPTH
    a_init = unif((L, 2, A_WIDTH), 0.9, 0.999)
    return {
        'x': nrm((BATCH, SEQ, D_MODEL), 1.0),
        'c': nrm((BATCH, D_MODEL), 1.0),
        'ctx': nrm((BATCH, CTX_LEN, D_MODEL), 1.0),
        'c_ctx': nrm((D_MODEL,), 1.0),
        'ada_w': nrm((L, D_MODEL, 6 * D_MODEL), 0.5 * D_MODEL ** -0.5),
        'ada_b': nrm((L, 6 * D_MODEL), 0.02),
        'norm_mix_g': 1.0 + nrm((L, D_MODEL), 0.1),
        'norm_ffn_g': 1.0 + nrm((L, D_MODEL), 0.1),
        'w_in': nrm((L, D_MODEL, IN_COLS), D_MODEL ** -0.5),
        'w_out': nrm((L, D_MIX, D_MODEL), D_MIX ** -0.5),
        'lru_conv_w': nrm((L, 2, A_CONV, A_WIDTH), A_CONV ** -0.5),
        'lru_conv_b': nrm((L, 2, A_WIDTH), 0.02),
        'lru_wr': nrm((L, 2, A_HEADS, A_HD, A_HD), A_HD ** -0.5),
        'lru_br': nrm((L, 2, A_WIDTH), 0.1),
        'lru_wi': nrm((L, 2, A_HEADS, A_HD, A_HD), A_HD ** -0.5),
        'lru_bi': nrm((L, 2, A_WIDTH), 0.1),
        'lru_lambda': jnp.log(a_init) - jnp.log1p(-a_init),
        'rwkv_mu': unif((L, 2, B_COLS), 0.0, 0.5),
        'rwkv_w0': unif((L, 2, B_WIDTH), -6.0, -1.0),
        'rwkv_w2': nrm((L, 2, DECAY_RANK, B_WIDTH), 0.5 * DECAY_RANK ** -0.5),
        'rwkv_a0': nrm((L, 2, B_WIDTH), 0.3),
        'rwkv_a2': nrm((L, 2, ICLR_RANK, B_WIDTH), 0.5 * ICLR_RANK ** -0.5),
        'rwkv_g2': nrm((L, GATE_RANK, B_WIDTH), GATE_RANK ** -0.5),
        'rwkv_kk': 0.85 + nrm((L, B_WIDTH), 0.05),
        'rwkv_ka': 1.0 + nrm((L, B_WIDTH), 0.05),
        'rwkv_rk': nrm((L, B_HEADS, B_HD), 0.1),
        'rwkv_lnx_g': 1.0 + nrm((L, B_WIDTH), 0.1),
        'rwkv_lnx_b': nrm((L, B_WIDTH), 0.02),
        'na_rpb': nrm((L, C_HEADS, 2 * WIN_H - 1, 2 * WIN_W - 1), 0.2),
        'router_w': nrm((L, D_MODEL, N_EXPERTS), D_MODEL ** -0.5),
        'router_b': nrm((L, N_EXPERTS), 0.01),
        'moe_w_gu': nrm((L, N_EXPERTS, D_MODEL, 2 * D_FF), D_MODEL ** -0.5),
        'moe_b_gu': nrm((L, N_EXPERTS, 2 * D_FF), 0.02),
        'moe_w_dn': nrm((L, N_EXPERTS, D_FF, D_MODEL), D_FF ** -0.5),
        'moe_b_dn': nrm((L, N_EXPERTS, D_MODEL), 0.02),
        'final_g': 1.0 + nrm((D_MODEL,), 0.1),
    }


def reference(x, c, ctx, c_ctx, ada_w, ada_b, norm_mix_g, norm_ffn_g, w_in, w_out,
              lru_conv_w, lru_conv_b, lru_wr, lru_br, lru_wi, lru_bi, lru_lambda,
              rwkv_mu, rwkv_w0, rwkv_w2, rwkv_a0, rwkv_a2, rwkv_g2, rwkv_kk, rwkv_ka, rwkv_rk,
              rwkv_lnx_g, rwkv_lnx_b, na_rpb, router_w, router_b, moe_w_gu, moe_b_gu,
              moe_w_dn, moe_b_dn, final_g):
    bsz, seq, d = x.shape
    ctx_len = ctx.shape[1]
    cond_lat = jax.nn.silu(c)[:, None, :]
    cond_ctx = jax.nn.silu(c_ctx)[None, None, :]
    xl, xc = x, ctx
    for l in range(DEPTH):
        last = l == DEPTH - 1
        mod_l = jnp.split(cond_lat @ ada_w[l] + ada_b[l], 6, axis=-1)
        mod_c = jnp.split(cond_ctx @ ada_w[l] + ada_b[l], 6, axis=-1)
        hl = _modulate(_rmsnorm(xl, norm_mix_g[l]), mod_l[0], mod_l[1])
        hc = _modulate(_rmsnorm(xc, norm_mix_g[l]), mod_c[0], mod_c[1])
        pl = hl @ w_in[l]
        pc = hc @ w_in[l]
        pl_a, pl_b, pl_c = jnp.split(pl, [A_COLS, A_COLS + B_COLS], axis=-1)
        pc_a, pc_b, pc_c = jnp.split(pc, [A_COLS, A_COLS + B_COLS], axis=-1)
        need_ctx = not last
        ya_c, ya_l = _lru_mixer(pc_a, pl_a, lru_conv_w[l], lru_conv_b[l], lru_wr[l], lru_br[l],
                                lru_wi[l], lru_bi[l], lru_lambda[l], need_ctx)
        yb_c, yb_l = _rwkv_mixer(pc_b, pl_b, rwkv_mu[l], rwkv_w0[l], rwkv_w2[l], rwkv_a0[l], rwkv_a2[l],
                                 rwkv_g2[l], rwkv_kk[l], rwkv_ka[l], rwkv_rk[l], rwkv_lnx_g[l],
                                 rwkv_lnx_b[l], need_ctx)
        yc_c, yc_l = _na_mixer(pc_c, pl_c, na_rpb[l], need_ctx)
        xl = xl + mod_l[2] * (jnp.concatenate([ya_l, yb_l, yc_l], axis=-1) @ w_out[l])
        hl = _modulate(_rmsnorm(xl, norm_ffn_g[l]), mod_l[3], mod_l[4])
        if last:
            yl = _moe(hl.reshape(-1, d), router_w[l], router_b[l], moe_w_gu[l], moe_b_gu[l],
                      moe_w_dn[l], moe_b_dn[l])
        else:
            xc = xc + mod_c[2] * (jnp.concatenate([ya_c, yb_c, yc_c], axis=-1) @ w_out[l])
            hc = _modulate(_rmsnorm(xc, norm_ffn_g[l]), mod_c[3], mod_c[4])
            tok = jnp.concatenate([hc.reshape(-1, d), hl.reshape(-1, d)], axis=0)
            y = _moe(tok, router_w[l], router_b[l], moe_w_gu[l], moe_b_gu[l], moe_w_dn[l], moe_b_dn[l])
            xc = xc + mod_c[5] * y[:bsz * ctx_len].reshape(bsz, ctx_len, d)
            yl = y[bsz * ctx_len:]
        xl = xl + mod_l[5] * yl.reshape(bsz, seq, d)
    return _rmsnorm(xl, final_g)
```

```python
import functools

import numpy as np
import jax
import jax.numpy as jnp
from jax import lax
from jax.experimental import pallas as pl
from jax.experimental.pallas import tpu as pltpu

F32 = jnp.float32
BF16 = jnp.bfloat16
HI = lax.Precision.HIGHEST

RMS_EPS = 1e-6
GN_EPS = 64e-5
LRU_C = 8.0
SWIGLU_ALPHA = 1.702
SWIGLU_LIMIT = 7.0
TOP_K = 4
GRID_W = 64
WIN_H = 8
WIN_W = 16
HEAD_DIM = 64
A_WIDTH = 256
B_WIDTH = 256
C_WIDTH = 512
LORA = 64
HALO = 8
RWKV_CHUNK = 64
LRU_CHUNK = 256
MOE_BM = 256
MASK_VALUE = -1e30
VMEM_LIMIT = 56 * 1024 * 1024


def _dot(a, b, prec=None):
    return jnp.dot(a, b, preferred_element_type=F32, precision=prec)


def _dot_nt(a, b, prec=None):
    return lax.dot_general(a, b, (((1,), (1,)), ((), ())), preferred_element_type=F32, precision=prec)


def _dot_tn(a, b, prec=None):
    return lax.dot_general(a, b, (((0,), (0,)), ((), ())), preferred_element_type=F32, precision=prec)


def _softplus(z):
    return jnp.maximum(z, 0.0) + jnp.log1p(jnp.exp(-jnp.abs(z)))


def _params(*sem):
    return pltpu.CompilerParams(dimension_semantics=sem, vmem_limit_bytes=VMEM_LIMIT)


def _head_ones(width):
    r = lax.broadcasted_iota(jnp.int32, (width, width), 0) // HEAD_DIM
    c = lax.broadcasted_iota(jnp.int32, (width, width), 1) // HEAD_DIM
    return r == c


def _adaln_kernel(cond_ref, w_ref, b_ref, o_ref):
    c = cond_ref[...]
    o_ref[0] = _dot(c * jax.nn.sigmoid(c), w_ref[0], HI) + b_ref[0]


def _adaln(cond, ada_w, ada_b):
    depth, d, n = ada_w.shape
    tn = 1536
    return pl.pallas_call(
        _adaln_kernel,
        grid=(depth, n // tn),
        in_specs=[pl.BlockSpec((8, d), lambda l, j: (0, 0)),
                  pl.BlockSpec((1, d, tn), lambda l, j: (l, 0, j)),
                  pl.BlockSpec((1, 1, tn), lambda l, j: (l, 0, j))],
        out_specs=pl.BlockSpec((1, 8, tn), lambda l, j: (l, 0, j)),
        out_shape=jax.ShapeDtypeStruct((depth, 8, n), F32),
        compiler_params=_params("arbitrary", "arbitrary"),
        name="adaln",
    )(cond, ada_w, ada_b.reshape(depth, 1, n))


N_SHIFT = 1280
N_PROJ = 3072


def _inproj_kernel(x_ref, xp_ref, xn_ref, g_ref, sh_ref, sc_ref, w_ref, cw_ref, cb_ref, mu_ref,
                   xcf_ref, xcb_ref, ga_ref, brkv_ref, blo_ref, q_ref, k_ref, v_ref):
    i = pl.program_id(1)
    n = pl.num_programs(1)
    tm = x_ref.shape[1]
    g = g_ref[...]
    sh = sh_ref[0]
    sc = sc_ref[0]

    def norm_mod(x):
        y = x * lax.rsqrt(jnp.mean(x * x, axis=-1, keepdims=True) + RMS_EPS) * g
        return y * (1.0 + sc) + sh

    p = _dot(norm_mod(x_ref[0]).astype(BF16), w_ref[...])
    halo = jnp.concatenate([xp_ref[0], xn_ref[0]], axis=0)
    ph = _dot(norm_mod(halo).astype(BF16), w_ref[:, :N_SHIFT])
    p_prev = jnp.where(i > 0, ph[:HALO], 0.0)
    p_next = jnp.where(i < n - 1, ph[HALO:], 0.0)
    ext = jnp.concatenate([p_prev, p[:, :N_SHIFT], p_next], axis=0)

    xa = ext[:, :A_WIDTH]
    cw = cw_ref[...]
    cb = cb_ref[...]
    xcf = cb[0:1]
    xcb = cb[1:2]
    for j in range(cw.shape[1]):
        xcf = xcf + cw[0, j:j + 1] * xa[HALO - j:HALO - j + tm]
        xcb = xcb + cw[1, j:j + 1] * xa[HALO + j:HALO + j + tm]
    xcf_ref[0] = xcf
    xcb_ref[0] = xcb

    pb = ext[HALO:HALO + tm, A_WIDTH:]
    prev = ext[HALO - 1:HALO - 1 + tm, A_WIDTH:]
    nxt = ext[HALO + 1:HALO + 1 + tm, A_WIDTH:]
    mu = mu_ref[...]
    sb = pb + mu[0:1] * (prev - pb) + mu[1:2] * (nxt - pb)
    brkv_ref[0] = sb[:, :3 * B_WIDTH]
    blo_ref[0] = sb[:, 3 * B_WIDTH:]

    o = N_SHIFT
    ga_ref[0] = p[:, o:o + A_WIDTH]
    o += A_WIDTH
    q_ref[0] = (p[:, o:o + C_WIDTH] * (HEAD_DIM ** -0.5)).astype(BF16)
    k_ref[0] = p[:, o + C_WIDTH:o + 2 * C_WIDTH].astype(BF16)
    v_ref[0] = p[:, o + 2 * C_WIDTH:o + 3 * C_WIDTH].astype(BF16)


def _inproj(x, g, shift, scale, w_perm, conv_w, conv_b, mu_perm, tm):
    bsz, t, d = x.shape
    nt = t // tm
    hb = tm // HALO
    last = t // HALO - 1
    f = lambda shape, dt=F32: jax.ShapeDtypeStruct(shape, dt)
    blk = lambda w: pl.BlockSpec((1, tm, w), lambda b, i: (b, i, 0))
    full = lambda a: pl.BlockSpec(a.shape, lambda b, i: (0,) * a.ndim)
    return pl.pallas_call(
        _inproj_kernel,
        grid=(bsz, nt),
        in_specs=[blk(d),
                  pl.BlockSpec((1, HALO, d), lambda b, i: (b, jnp.maximum(i * hb - 1, 0), 0)),
                  pl.BlockSpec((1, HALO, d), lambda b, i: (b, jnp.minimum((i + 1) * hb, last), 0)),
                  full(g),
                  pl.BlockSpec((1, 1, d), lambda b, i: (b, 0, 0)),
                  pl.BlockSpec((1, 1, d), lambda b, i: (b, 0, 0)),
                  full(w_perm), full(conv_w), full(conv_b), full(mu_perm)],
        out_specs=[blk(A_WIDTH), blk(A_WIDTH), blk(A_WIDTH), blk(3 * B_WIDTH), blk(256),
                   blk(C_WIDTH), blk(C_WIDTH), blk(C_WIDTH)],
        out_shape=[f((bsz, t, A_WIDTH)), f((bsz, t, A_WIDTH)), f((bsz, t, A_WIDTH)),
                   f((bsz, t, 3 * B_WIDTH)), f((bsz, t, 256)),
                   f((bsz, t, C_WIDTH), BF16), f((bsz, t, C_WIDTH), BF16), f((bsz, t, C_WIDTH), BF16)],
        compiler_params=_params("arbitrary", "arbitrary"),
        name="inproj",
    )(x, x, x, g, shift, scale, w_perm, conv_w, conv_b, mu_perm)


def _chunk_scan(a, b, rev):
    n = a.shape[0]
    row = lax.broadcasted_iota(jnp.int32, a.shape, 0)
    s = 1
    while s < n:
        if rev:
            keep = row < n - s
            a_s = jnp.where(keep, pltpu.roll(a, n - s, 0), 1.0)
            b_s = jnp.where(keep, pltpu.roll(b, n - s, 0), 0.0)
        else:
            keep = row >= s
            a_s = jnp.where(keep, pltpu.roll(a, s, 0), 1.0)
            b_s = jnp.where(keep, pltpu.roll(b, s, 0), 0.0)
        b = a * b_s + b
        a = a * a_s
        s *= 2
    return a, b


def _lru_kernel(xf_ref, xb_ref, h0_ref, wr_ref, br_ref, wi_ref, bi_ref, lam_ref,
                hf_ref, hb_ref, hend_ref, carry):
    i = pl.program_id(1)
    n = pl.num_programs(1)
    ch = xf_ref.shape[1]

    @pl.when(i == 0)
    def _():
        carry[...] = h0_ref[0]

    for d, (x_ref, o_ref) in enumerate(((xf_ref, hf_ref), (xb_ref, hb_ref))):
        x = x_ref[0]
        gate_r = jax.nn.sigmoid(_dot(x, wr_ref[d], HI) + br_ref[d:d + 1])
        gate_i = jax.nn.sigmoid(_dot(x, wi_ref[d], HI) + bi_ref[d:d + 1])
        log_a = -LRU_C * gate_r * _softplus(-lam_ref[d:d + 1])
        a = jnp.exp(log_a)
        b = jnp.sqrt(1.0 - jnp.exp(2.0 * log_a)) * (gate_i * x)
        a_cum, h = _chunk_scan(a, b, rev=bool(d))
        h = h + a_cum * carry[d:d + 1]
        o_ref[0] = h
        carry[d:d + 1] = h[0:1] if d else h[ch - 1:ch]

    @pl.when(i == n - 1)
    def _():
        hend_ref[0] = carry[...]


def _lru_scan(xcf, xcb, h0, wr_bd, br, wi_bd, bi, lam):
    bsz, t, a = xcf.shape
    ch = min(LRU_CHUNK, t)
    nt = t // ch
    full = lambda z: pl.BlockSpec(z.shape, lambda b, i: (0,) * z.ndim)
    fwd = pl.BlockSpec((1, ch, a), lambda b, i: (b, i, 0))
    bwd = pl.BlockSpec((1, ch, a), lambda b, i: (b, nt - 1 - i, 0))
    st = pl.BlockSpec((1, 2, a), lambda b, i: (b, 0, 0))
    return pl.pallas_call(
        _lru_kernel,
        grid=(bsz, nt),
        in_specs=[fwd, bwd, st, full(wr_bd), full(br), full(wi_bd), full(bi), full(lam)],
        out_specs=[fwd, bwd, st],
        out_shape=[jax.ShapeDtypeStruct((bsz, t, a), F32), jax.ShapeDtypeStruct((bsz, t, a), F32),
                   jax.ShapeDtypeStruct((bsz, 2, a), F32)],
        scratch_shapes=[pltpu.VMEM((2, a), F32)],
        compiler_params=_params("arbitrary", "arbitrary"),
        name="lru_scan",
    )(xcf, xcb, h0, wr_bd, br, wi_bd, bi, lam)


def _stack_heads(x):
    nh = x.shape[1] // HEAD_DIM
    return jnp.concatenate([x[:, h * HEAD_DIM:(h + 1) * HEAD_DIM] for h in range(nh)], axis=0)


def _unstack_heads(x, c):
    nh = x.shape[0] // c
    return jnp.concatenate([x[h * c:(h + 1) * c] for h in range(nh)], axis=1)


def _rwkv_chunk(r, k, v, wlo, alo, w0, w2, a0, a2, k_k, k_a, r_k, m_ref, rev):
    c, width = r.shape
    hc = (width // HEAD_DIM) * c
    ones_bd = _head_ones(width).astype(F32)

    w_raw = -_softplus(-(w0 + _dot(jnp.tanh(wlo), w2, HI))) - 0.5
    logw = -jnp.exp(w_raw)
    a_gate = jax.nn.sigmoid(a0 + _dot(alo, a2, HI))
    kk = k * k_k
    kk = kk / jnp.maximum(jnp.sqrt(_dot(kk * kk, ones_bd, HI)), 1e-12)
    k_eff = k * (1.0 + (a_gate - 1.0) * k_a)
    a_vec = -kk
    b_vec = kk * a_gate

    tr = lax.broadcasted_iota(jnp.int32, (c, c), 0)
    tc = lax.broadcasted_iota(jnp.int32, (c, c), 1)
    tri = (tc >= tr) if rev else (tc <= tr)
    cum = _dot(tri.astype(F32), logw, HI)
    total = cum[0:1] if rev else cum[c - 1:c]
    a_t = a_vec * jnp.exp(cum - logw)
    b_t = b_vec * jnp.exp(-cum)
    k_t = k_eff * jnp.exp(-cum)
    r_t = r * jnp.exp(cum)
    b_end = b_vec * jnp.exp(total - cum)
    k_end = k_eff * jnp.exp(total - cum)

    a_s, b_s, k_s, r_s, v_s = (_stack_heads(z) for z in (a_t, b_t, k_t, r_t, v))
    ri = lax.broadcasted_iota(jnp.int32, (hc, hc), 0)
    ci = lax.broadcasted_iota(jnp.int32, (hc, hc), 1)
    same = (ri // c) == (ci // c)
    earlier = (ci > ri) if rev else (ci < ri)
    strict = same & earlier
    incl = same & (earlier | (ri == ci))
    n_ab = jnp.where(strict, _dot_nt(a_s, b_s, HI), 0.0)
    g_ak = jnp.where(strict, _dot_nt(a_s, k_s, HI), 0.0)
    g_rb = jnp.where(incl, _dot_nt(r_s, b_s, HI), 0.0)
    g_rk = jnp.where(incl, _dot_nt(r_s, k_s, HI), 0.0)

    eye = (ri == ci).astype(F32)
    t_inv = eye + n_ab
    pw = n_ab
    s = 2
    while s < c:
        pw = _dot(pw, pw, HI)
        t_inv = t_inv + _dot(t_inv, pw, HI)
        s *= 2

    a_hat = _dot(t_inv, a_s, HI)
    w2_s = _dot(t_inv, _dot(g_ak, v_s, HI), HI)
    r_hat = r_s + _dot(g_rb, a_hat, HI)
    y0 = _dot(g_rb, w2_s, HI) + _dot(g_rk, v_s, HI)
    a_hat, w2_l, r_hat, y0 = (_unstack_heads(z, c) for z in (a_hat, w2_s, r_hat, y0))

    m = m_ref[...]
    y = _dot(r_hat, m, HI) + y0
    wi = lax.broadcasted_iota(jnp.int32, (width, width), 0)
    wj = lax.broadcasted_iota(jnp.int32, (width, width), 1)
    bd = (wi // HEAD_DIM) == (wj // HEAD_DIM)
    p_m = jnp.where(bd, _dot_tn(b_end, a_hat, HI), 0.0) + jnp.where(wi == wj, jnp.exp(total), 0.0)
    q_m = jnp.where(bd, _dot_tn(b_end, w2_l, HI) + _dot_tn(k_end, v, HI), 0.0)
    m_ref[...] = _dot(p_m, m, HI) + q_m
    bonus_v = _dot(r * k_eff * r_k, ones_bd, HI) * v
    return y, bonus_v


def _rwkv_kernel(rkvf_ref, lof_ref, rkvb_ref, lob_ref, m0_ref, w0_ref, w2_ref, a0_ref, a2_ref,
                 kk_ref, ka_ref, rk_ref, yf_ref, yb_ref, bvf_ref, bvb_ref, mend_ref, m_f, m_b):
    i = pl.program_id(1)
    n = pl.num_programs(1)

    @pl.when(i == 0)
    def _():
        m_f[...] = m0_ref[0, 0]
        m_b[...] = m0_ref[0, 1]

    w = B_WIDTH
    for d, (rkv_ref, lo_ref, y_ref, bv_ref, m_ref) in enumerate(
            ((rkvf_ref, lof_ref, yf_ref, bvf_ref, m_f), (rkvb_ref, lob_ref, yb_ref, bvb_ref, m_b))):
        rkv = rkv_ref[0]
        lo = lo_ref[0]
        y, bv = _rwkv_chunk(rkv[:, :w], rkv[:, w:2 * w], rkv[:, 2 * w:], lo[:, :LORA], lo[:, LORA:2 * LORA],
                            w0_ref[d:d + 1], w2_ref[d], a0_ref[d:d + 1], a2_ref[d],
                            kk_ref[...], ka_ref[...], rk_ref[...], m_ref, rev=bool(d))
        y_ref[0] = y
        bv_ref[0] = bv

    @pl.when(i == n - 1)
    def _():
        mend_ref[0, 0] = m_f[...]
        mend_ref[0, 1] = m_b[...]


def _rwkv_scan(brkv, blo, m0, w0, w2, a0, a2, k_k, k_a, r_k):
    bsz, t, _ = brkv.shape
    c = RWKV_CHUNK
    nt = t // c
    w = B_WIDTH
    full = lambda z: pl.BlockSpec(z.shape, lambda b, i: (0,) * z.ndim)
    fwd = lambda width: pl.BlockSpec((1, c, width), lambda b, i: (b, i, 0))
    bwd = lambda width: pl.BlockSpec((1, c, width), lambda b, i: (b, nt - 1 - i, 0))
    st = pl.BlockSpec((1, 2, w, w), lambda b, i: (b, 0, 0, 0))
    o = jax.ShapeDtypeStruct((bsz, t, w), F32)
    return pl.pallas_call(
        _rwkv_kernel,
        grid=(bsz, nt),
        in_specs=[fwd(3 * w), fwd(256), bwd(3 * w), bwd(256), st,
                  full(w0), full(w2), full(a0), full(a2), full(k_k), full(k_a), full(r_k)],
        out_specs=[fwd(w), bwd(w), fwd(w), bwd(w), st],
        out_shape=[o, o, o, o, jax.ShapeDtypeStruct((bsz, 2, w, w), F32)],
        scratch_shapes=[pltpu.VMEM((w, w), F32), pltpu.VMEM((w, w), F32)],
        compiler_params=_params("arbitrary", "arbitrary"),
        name="rwkv_scan",
    )(brkv, blo, brkv, blo, m0, w0, w2, a0, a2, k_k, k_a, r_k)


def _na_kernel(q_ref, k_ref, v_ref, kc_ref, vc_ref, bias_ref, o_ref):
    r = pl.program_id(1)
    rows = pl.num_programs(1)
    kh = bias_ref.shape[3] // GRID_W
    rs = jnp.clip(r - kh // 2, 0, rows - kh)
    start = pl.multiple_of(rs * GRID_W, GRID_W)
    q = q_ref[0]
    kb = k_ref[0, pl.ds(start, kh * GRID_W), :]
    vb = v_ref[0, pl.ds(start, kh * GRID_W), :]
    kc = kc_ref[0]
    vc = vc_ref[0]
    outs = []
    for h in range(q.shape[1] // HEAD_DIM):
        sl = slice(h * HEAD_DIM, (h + 1) * HEAD_DIM)
        s_w = _dot_nt(q[:, sl], kb[:, sl]) + bias_ref[0, h]
        s_c = _dot_nt(q[:, sl], kc[:, sl])
        m = jnp.maximum(jnp.max(s_w, axis=-1, keepdims=True), jnp.max(s_c, axis=-1, keepdims=True))
        p_w = jnp.exp(s_w - m)
        p_c = jnp.exp(s_c - m)
        den = jnp.sum(p_w, axis=-1, keepdims=True) + jnp.sum(p_c, axis=-1, keepdims=True)
        o = _dot(p_w.astype(BF16), vb[:, sl]) + _dot(p_c.astype(BF16), vc[:, sl])
        outs.append(o / den)
    o_ref[0] = jnp.concatenate(outs, axis=1)


def _na_bias_table(rpb, rows):
    kh = min(WIN_H, rows)
    var = np.arange(kh)[:, None, None, None]
    qcol = np.arange(GRID_W)[None, :, None, None]
    krow = np.arange(kh)[None, None, :, None]
    kcol = np.arange(GRID_W)[None, None, None, :]
    cstart = np.clip(qcol - WIN_W // 2, 0, GRID_W - WIN_W)
    valid = (kcol >= cstart) & (kcol < cstart + WIN_W)
    ridx = np.broadcast_to(krow - var + (WIN_H - 1), (kh, GRID_W, kh, GRID_W))
    cidx = np.broadcast_to(np.clip(kcol - qcol + (WIN_W - 1), 0, 2 * WIN_W - 2), (kh, GRID_W, kh, GRID_W))
    valid = np.broadcast_to(valid, (kh, GRID_W, kh, GRID_W))
    tab = rpb[:, ridx, cidx]
    tab = jnp.where(valid[None], tab, MASK_VALUE)
    nh = rpb.shape[0]
    return jnp.transpose(tab, (1, 0, 2, 3, 4)).reshape(kh, nh, GRID_W, kh * GRID_W)


def _na_attention(q, k, v, kc, vc, bias_tab):
    bsz, seq, cw = q.shape
    ctx_len = kc.shape[1]
    rows = seq // GRID_W
    kh = bias_tab.shape[0]
    nh = bias_tab.shape[1]

    def bias_map(b, r):
        return (r - jnp.clip(r - kh // 2, 0, rows - kh), 0, 0, 0)

    return pl.pallas_call(
        _na_kernel,
        grid=(bsz, rows),
        in_specs=[pl.BlockSpec((1, GRID_W, cw), lambda b, r: (b, r, 0)),
                  pl.BlockSpec((1, seq, cw), lambda b, r: (b, 0, 0)),
                  pl.BlockSpec((1, seq, cw), lambda b, r: (b, 0, 0)),
                  pl.BlockSpec((1, ctx_len, cw), lambda b, r: (b, 0, 0)),
                  pl.BlockSpec((1, ctx_len, cw), lambda b, r: (b, 0, 0)),
                  pl.BlockSpec((1, nh, GRID_W, kh * GRID_W), bias_map)],
        out_specs=pl.BlockSpec((1, GRID_W, cw), lambda b, r: (b, r, 0)),
        out_shape=jax.ShapeDtypeStruct((bsz, seq, cw), F32),
        compiler_params=_params("arbitrary", "arbitrary"),
        name="na_attention",
    )(q, k, v, kc, vc, bias_tab)


def _ctx_attn_kernel(q_ref, k_ref, v_ref, o_ref):
    q = q_ref[0]
    k = k_ref[0]
    v = v_ref[0]
    outs = []
    for h in range(q.shape[1] // HEAD_DIM):
        sl = slice(h * HEAD_DIM, (h + 1) * HEAD_DIM)
        s = _dot_nt(q[:, sl], k[:, sl])
        p = jnp.exp(s - jnp.max(s, axis=-1, keepdims=True))
        outs.append(_dot(p.astype(BF16), v[:, sl]) / jnp.sum(p, axis=-1, keepdims=True))
    o_ref[0] = jnp.concatenate(outs, axis=1)


def _ctx_attention(q, k, v):
    bsz, n, cw = q.shape
    spec = pl.BlockSpec((1, n, cw), lambda b: (b, 0, 0))
    return pl.pallas_call(
        _ctx_attn_kernel, grid=(bsz,), in_specs=[spec, spec, spec], out_specs=spec,
        out_shape=jax.ShapeDtypeStruct((bsz, n, cw), F32),
        compiler_params=_params("arbitrary"), name="ctx_attention",
    )(q, k, v)


def _outproj_kernel(x_ref, hf_ref, hb_ref, ga_ref, yf_ref, yb_ref, bvf_ref, bvb_ref, lo_ref, yc_ref,
                    wo_ref, g2_ref, lng_ref, lnb_ref, gate_ref, g_ref, sh_ref, sc_ref, rw_ref, rb_ref,
                    xo_ref, h_ref, lg_ref):
    ya = jax.nn.gelu(ga_ref[0]) * (hf_ref[0] + hb_ref[0])

    y = yf_ref[0] + yb_ref[0]
    avg = _head_ones(B_WIDTH).astype(F32) * (1.0 / HEAD_DIM)
    mean = _dot(y, avg, HI)
    yc_ = y - mean
    var = _dot(yc_ * yc_, avg, HI)
    yn = yc_ * lax.rsqrt(var + GN_EPS) * lng_ref[...] + lnb_ref[...] + bvf_ref[0] + bvb_ref[0]
    gate_b = _dot(jax.nn.sigmoid(lo_ref[0][:, 2 * LORA:3 * LORA]), g2_ref[...], HI)
    yb = yn * gate_b

    mix = (_dot(ya.astype(BF16), wo_ref[:A_WIDTH]) + _dot(yb.astype(BF16), wo_ref[A_WIDTH:A_WIDTH + B_WIDTH])
           + _dot(yc_ref[0].astype(BF16), wo_ref[A_WIDTH + B_WIDTH:]))
    x = x_ref[0] + gate_ref[0] * mix
    xo_ref[0] = x
    hn = x * lax.rsqrt(jnp.mean(x * x, axis=-1, keepdims=True) + RMS_EPS) * g_ref[...]
    hn = hn * (1.0 + sc_ref[0]) + sh_ref[0]
    h_ref[0] = hn
    lg_ref[0] = _dot(hn, rw_ref[...], HI) + rb_ref[...]


def _outproj(x, hf, hb, ga, yf, yb, bvf, bvb, blo, yc, wo, g2, lng, lnb, gate, g, shift, scale, rw, rb, tm):
    bsz, t, d = x.shape
    ne = rw.shape[1]
    blk = lambda w: pl.BlockSpec((1, tm, w), lambda b, i: (b, i, 0))
    full = lambda a: pl.BlockSpec(a.shape, lambda b, i: (0,) * a.ndim)
    vec = pl.BlockSpec((1, 1, d), lambda b, i: (b, 0, 0))
    return pl.pallas_call(
        _outproj_kernel,
        grid=(bsz, t // tm),
        in_specs=[blk(d), blk(A_WIDTH), blk(A_WIDTH), blk(A_WIDTH), blk(B_WIDTH), blk(B_WIDTH),
                  blk(B_WIDTH), blk(B_WIDTH), blk(256), blk(C_WIDTH),
                  full(wo), full(g2), full(lng), full(lnb), vec, full(g), vec, vec, full(rw), full(rb)],
        out_specs=[blk(d), blk(d), blk(ne)],
        out_shape=[jax.ShapeDtypeStruct((bsz, t, d), F32), jax.ShapeDtypeStruct((bsz, t, d), F32),
                   jax.ShapeDtypeStruct((bsz, t, ne), F32)],
        compiler_params=_params("arbitrary", "arbitrary"),
        name="outproj",
    )(x, hf, hb, ga, yf, yb, bvf, bvb, blo, yc, wo, g2, lng, lnb, gate, g, shift, scale, rw, rb)


def _expert_kernel(be_ref, nb_ref, x_ref, gt_ref, wgu_ref, bgu_ref, wdn_ref, bdn_ref, o_ref, wgu_s, wdn_s):
    i = pl.program_id(0)
    e = be_ref[i]
    fresh = jnp.logical_or(i == 0, be_ref[jnp.maximum(i - 1, 0)] != e)

    @pl.when(fresh)
    def _():
        wgu_s[...] = wgu_ref[0].astype(BF16)
        wdn_s[...] = wdn_ref[0].astype(BF16)

    @pl.when(i < nb_ref[0])
    def _():
        dff = wdn_s.shape[0]
        gu = _dot(x_ref[...].astype(BF16), wgu_s[...]) + bgu_ref[0]
        g_ = jnp.minimum(gu[:, :dff], SWIGLU_LIMIT)
        u_ = jnp.clip(gu[:, dff:], -SWIGLU_LIMIT, SWIGLU_LIMIT)
        act = (u_ + 1.0) * (g_ * jax.nn.sigmoid(SWIGLU_ALPHA * g_))
        o_ref[...] = (_dot(act.astype(BF16), wdn_s[...]) + bdn_ref[0]) * gt_ref[...]

    @pl.when(i >= nb_ref[0])
    def _():
        o_ref[...] = jnp.zeros_like(o_ref)


def _experts(block_e, n_used, xb, slot_gate, w_gu, b_gu, w_dn, b_dn):
    cap, d = xb.shape
    ne, _, f2 = w_gu.shape
    dff = w_dn.shape[1]
    nb = cap // MOE_BM
    grid_spec = pltpu.PrefetchScalarGridSpec(
        num_scalar_prefetch=2,
        grid=(nb,),
        in_specs=[pl.BlockSpec((MOE_BM, d), lambda i, be, nu: (i, 0)),
                  pl.BlockSpec((MOE_BM, 1), lambda i, be, nu: (i, 0)),
                  pl.BlockSpec((1, d, f2), lambda i, be, nu: (be[i], 0, 0)),
                  pl.BlockSpec((1, 1, f2), lambda i, be, nu: (be[i], 0, 0)),
                  pl.BlockSpec((1, dff, d), lambda i, be, nu: (be[i], 0, 0)),
                  pl.BlockSpec((1, 1, d), lambda i, be, nu: (be[i], 0, 0))],
        out_specs=pl.BlockSpec((MOE_BM, d), lambda i, be, nu: (i, 0)),
        scratch_shapes=[pltpu.VMEM((d, f2), BF16), pltpu.VMEM((dff, d), BF16)],
    )
    return pl.pallas_call(
        _expert_kernel, grid_spec=grid_spec,
        out_shape=jax.ShapeDtypeStruct((cap, d), F32),
        compiler_params=_params("arbitrary"), name="experts",
    )(block_e, n_used, xb, slot_gate.reshape(cap, 1), w_gu, b_gu.reshape(ne, 1, f2), w_dn, b_dn.reshape(ne, 1, d))


def _moe(h, logits, w_gu, b_gu, w_dn, b_dn):
    n, d = h.shape
    ne = logits.shape[1]
    top_val, top_idx = lax.top_k(logits, TOP_K)
    gate = jax.nn.softmax(top_val, axis=-1)
    n_assign = n * TOP_K
    flat_e = top_idx.reshape(-1).astype(jnp.int32)
    flat_tok = jnp.arange(n_assign, dtype=jnp.int32) // TOP_K
    order = jnp.argsort(flat_e)
    e_sorted = flat_e[order]
    counts = jnp.bincount(flat_e, length=ne)
    padded = (counts + MOE_BM - 1) // MOE_BM * MOE_BM
    pad_end = jnp.cumsum(padded)
    pad_start = pad_end - padded
    start = jnp.cumsum(counts) - counts
    dest = pad_start[e_sorted] + jnp.arange(n_assign, dtype=jnp.int32) - start[e_sorted]
    nb = (n_assign + ne * (MOE_BM - 1) + MOE_BM - 1) // MOE_BM
    cap = nb * MOE_BM
    slot_tok = jnp.full((cap,), n, dtype=jnp.int32).at[dest].set(flat_tok[order])
    slot_gate = jnp.zeros((cap,), F32).at[dest].set(gate.reshape(-1)[order])
    block_e = jnp.minimum(jnp.searchsorted(pad_end, jnp.arange(nb, dtype=jnp.int32) * MOE_BM, side='right'),
                          ne - 1).astype(jnp.int32)
    n_used = (pad_end[-1] // MOE_BM).astype(jnp.int32).reshape(1)
    h_pad = jnp.concatenate([h, jnp.zeros((1, d), h.dtype)], axis=0)
    xb = h_pad[slot_tok]
    yb = _experts(block_e, n_used, xb, slot_gate, w_gu, b_gu, w_dn, b_dn)
    out = jnp.zeros((n + 1, d), F32).at[slot_tok].add(yb)
    return out[:n]


def _residual_kernel(x_ref, y_ref, gate_ref, g_ref, o_ref, *, final):
    x = x_ref[0] + gate_ref[0] * y_ref[0]
    if final:
        x = x * lax.rsqrt(jnp.mean(x * x, axis=-1, keepdims=True) + RMS_EPS) * g_ref[...]
    o_ref[0] = x


def _residual(x, y, gate, g, final, tm):
    bsz, t, d = x.shape
    blk = pl.BlockSpec((1, tm, d), lambda b, i: (b, i, 0))
    return pl.pallas_call(
        functools.partial(_residual_kernel, final=final),
        grid=(bsz, t // tm),
        in_specs=[blk, blk, pl.BlockSpec((1, 1, d), lambda b, i: (b, 0, 0)),
                  pl.BlockSpec((1, d), lambda b, i: (0, 0))],
        out_specs=blk,
        out_shape=jax.ShapeDtypeStruct((bsz, t, d), F32),
        compiler_params=_params("arbitrary", "arbitrary"),
        name="residual",
    )(x, y, gate, g)


def _block_diag(w):
    nh, n, _ = w.shape
    out = jnp.zeros((nh * n, nh * n), w.dtype)
    for h in range(nh):
        out = out.at[h * n:(h + 1) * n, h * n:(h + 1) * n].set(w[h])
    return out


def _permute_in_cols(w_in):
    a0 = 0
    b0 = 2 * A_WIDTH
    c0 = b0 + 3 * B_WIDTH + 3 * LORA
    d = w_in.shape[0]
    pad = jnp.zeros((d, 256 - 3 * LORA), w_in.dtype)
    return jnp.concatenate([w_in[:, a0:a0 + A_WIDTH], w_in[:, b0:b0 + 3 * B_WIDTH],
                            w_in[:, b0 + 3 * B_WIDTH:c0], pad,
                            w_in[:, A_WIDTH:2 * A_WIDTH], w_in[:, c0:]], axis=1)


def _mixers(p_ctx, p_lat, lp, need_ctx):
    (xcf_c, xcb_c, ga_c, brkv_c, blo_c, q_c, k_c, v_c) = p_ctx
    (xcf_l, xcb_l, ga_l, brkv_l, blo_l, q_l, k_l, v_l) = p_lat
    bsz = xcf_c.shape[0]

    lru_w = (lp['wr_bd'], lp['br'], lp['wi_bd'], lp['bi'], lp['lam'])
    hf_c, hb_c, hend = _lru_scan(xcf_c, xcb_c, jnp.zeros((bsz, 2, A_WIDTH), F32), *lru_w)
    hf_l, hb_l, _ = _lru_scan(xcf_l, xcb_l, hend, *lru_w)

    rw_w = (lp['w0'], lp['w2'], lp['a0'], lp['a2'], lp['k_k'], lp['k_a'], lp['r_k'])
    yf_c, yb_c, bvf_c, bvb_c, mend = _rwkv_scan(brkv_c, blo_c, jnp.zeros((bsz, 2, B_WIDTH, B_WIDTH), F32), *rw_w)
    yf_l, yb_l, bvf_l, bvb_l, _ = _rwkv_scan(brkv_l, blo_l, mend, *rw_w)

    yc_l = _na_attention(q_l, k_l, v_l, k_c, v_c, lp['bias_tab'])
    lat = (hf_l, hb_l, ga_l, yf_l, yb_l, bvf_l, bvb_l, blo_l, yc_l)
    ctx = None
    if need_ctx:
        yc_c = _ctx_attention(q_c, k_c, v_c)
        ctx = (hf_c, hb_c, ga_c, yf_c, yb_c, bvf_c, bvb_c, blo_c, yc_c)
    return ctx, lat


def kernel(x, c, ctx, c_ctx, ada_w, ada_b, norm_mix_g, norm_ffn_g, w_in, w_out, lru_conv_w, lru_conv_b, lru_wr, lru_br, lru_wi, lru_bi, lru_lambda, rwkv_mu, rwkv_w0, rwkv_w2, rwkv_a0, rwkv_a2, rwkv_g2, rwkv_kk, rwkv_ka, rwkv_rk, rwkv_lnx_g, rwkv_lnx_b, na_rpb, router_w, router_b, moe_w_gu, moe_b_gu, moe_w_dn, moe_b_dn, final_g):
    bsz, seq, d = x.shape
    ctx_len = ctx.shape[1]
    depth = ada_w.shape[0]
    rows = seq // GRID_W
    tm_l = 256
    tm_c = min(256, ctx_len)

    cond = jnp.concatenate([c, c_ctx[None], jnp.zeros((8 - bsz - 1, d), F32)], axis=0)
    mod = _adaln(cond, ada_w, ada_b)

    xl, xc = x, ctx
    for l in range(depth):
        last = l == depth - 1
        ml = [mod[l, :bsz, j * d:(j + 1) * d][:, None, :] for j in range(6)]
        mc = [jnp.broadcast_to(mod[l, bsz:bsz + 1, j * d:(j + 1) * d][:, None, :], (bsz, 1, d)) for j in range(6)]
        n_b = 3 * B_WIDTH
        mu = rwkv_mu[l]
        lp = dict(
            wr_bd=_block_diag(lru_wr[l, 0])[None], wi_bd=_block_diag(lru_wi[l, 0])[None],
            br=lru_br[l], bi=lru_bi[l], lam=lru_lambda[l],
            w0=rwkv_w0[l], w2=rwkv_w2[l], a0=rwkv_a0[l], a2=rwkv_a2[l],
            k_k=rwkv_kk[l][None], k_a=rwkv_ka[l][None], r_k=rwkv_rk[l].reshape(1, B_WIDTH),
            bias_tab=_na_bias_table(na_rpb[l], rows),
        )
        lp['wr_bd'] = jnp.stack([_block_diag(lru_wr[l, dd]) for dd in range(2)])
        lp['wi_bd'] = jnp.stack([_block_diag(lru_wi[l, dd]) for dd in range(2)])
        w_perm = _permute_in_cols(w_in[l]).astype(BF16)
        mu_perm = jnp.concatenate([mu[:, :n_b], mu[:, n_b:], jnp.zeros((2, 256 - 3 * LORA), F32)], axis=1)
        g_mix = norm_mix_g[l][None]
        g_ffn = norm_ffn_g[l][None]
        wo = w_out[l].astype(BF16)

        p_lat = _inproj(xl, g_mix, ml[0], ml[1], w_perm, lru_conv_w[l], lru_conv_b[l], mu_perm, tm_l)
        p_ctx = _inproj(xc, g_mix, mc[0], mc[1], w_perm, lru_conv_w[l], lru_conv_b[l], mu_perm, tm_c)
        mix_c, mix_l = _mixers(p_ctx, p_lat, lp, not last)

        fin = (wo, rwkv_g2[l], rwkv_lnx_g[l][None], rwkv_lnx_b[l][None])
        rt = (router_w[l], router_b[l][None])
        xl, hl, lg_l = _outproj(xl, *mix_l, *fin, ml[2], g_ffn, ml[3], ml[4], *rt, tm_l)
        ne = router_w.shape[2]
        if last:
            y = _moe(hl.reshape(-1, d), lg_l.reshape(-1, ne), moe_w_gu[l], moe_b_gu[l], moe_w_dn[l], moe_b_dn[l])
            yl = y
        else:
            xc, hc, lg_c = _outproj(xc, *mix_c, *fin, mc[2], g_ffn, mc[3], mc[4], *rt, tm_c)
            tok = jnp.concatenate([hc.reshape(-1, d), hl.reshape(-1, d)], axis=0)
            lg = jnp.concatenate([lg_c.reshape(-1, ne), lg_l.reshape(-1, ne)], axis=0)
            y = _moe(tok, lg, moe_w_gu[l], moe_b_gu[l], moe_w_dn[l], moe_b_dn[l])
            nc = bsz * ctx_len
            xc = _residual(xc, y[:nc].reshape(bsz, ctx_len, d), mc[5], final_g[None], False, tm_c)
            yl = y[nc:]
        xl = _residual(xl, yl.reshape(bsz, seq, d), ml[5], final_g[None], last, tm_l)
    return xl
```

```python
import functools

import numpy as np
import jax
import jax.numpy as jnp
from jax import lax
from jax.experimental import pallas as pl
from jax.experimental.pallas import tpu as pltpu

F32 = jnp.float32
BF16 = jnp.bfloat16
HI = lax.Precision.HIGHEST

RMS_EPS = 1e-6
GN_EPS = 64e-5
LRU_C = 8.0
SWIGLU_ALPHA = 1.702
SWIGLU_LIMIT = 7.0
TOP_K = 4
GRID_W = 64
WIN_H = 8
WIN_W = 16
HEAD_DIM = 64
A_WIDTH = 256
B_WIDTH = 256
C_WIDTH = 512
LORA = 64
HALO = 8
RWKV_CHUNK = 64
LRU_CHUNK = 256
MOE_BM = 256
ROUTE_TM = 512
DISPATCH_TM = 256
COMBINE_TM = 128
MASK_VALUE = -1e30
VMEM_LIMIT = 56 * 1024 * 1024


def _dot(a, b, prec=None):
    return jnp.dot(a, b, preferred_element_type=F32, precision=prec)


def _dot_nt(a, b, prec=None):
    return lax.dot_general(a, b, (((1,), (1,)), ((), ())), preferred_element_type=F32, precision=prec)


def _dot_tn(a, b, prec=None):
    return lax.dot_general(a, b, (((0,), (0,)), ((), ())), preferred_element_type=F32, precision=prec)


def _softplus(z):
    return jnp.maximum(z, 0.0) + jnp.log1p(jnp.exp(-jnp.abs(z)))


def _params(*sem):
    return pltpu.CompilerParams(dimension_semantics=sem, vmem_limit_bytes=VMEM_LIMIT)


def _head_ones(width):
    r = lax.broadcasted_iota(jnp.int32, (width, width), 0) // HEAD_DIM
    c = lax.broadcasted_iota(jnp.int32, (width, width), 1) // HEAD_DIM
    return r == c


def _adaln_kernel(cond_ref, w_ref, b_ref, o_ref):
    c = cond_ref[...]
    o_ref[0] = _dot(c * jax.nn.sigmoid(c), w_ref[0], HI) + b_ref[0]


def _adaln(cond, ada_w, ada_b):
    depth, d, n = ada_w.shape
    tn = 1536
    return pl.pallas_call(
        _adaln_kernel,
        grid=(depth, n // tn),
        in_specs=[pl.BlockSpec((8, d), lambda l, j: (0, 0)),
                  pl.BlockSpec((1, d, tn), lambda l, j: (l, 0, j)),
                  pl.BlockSpec((1, 1, tn), lambda l, j: (l, 0, j))],
        out_specs=pl.BlockSpec((1, 8, tn), lambda l, j: (l, 0, j)),
        out_shape=jax.ShapeDtypeStruct((depth, 8, n), F32),
        compiler_params=_params("arbitrary", "arbitrary"),
        name="adaln",
    )(cond, ada_w, ada_b.reshape(depth, 1, n))


N_SHIFT = 1280
N_PROJ = 3072


def _inproj_kernel(x_ref, xp_ref, xn_ref, g_ref, sh_ref, sc_ref, w_ref, cw_ref, cb_ref, mu_ref,
                   xcf_ref, xcb_ref, ga_ref, brkv_ref, blo_ref, q_ref, k_ref, v_ref):
    i = pl.program_id(1)
    n = pl.num_programs(1)
    tm = x_ref.shape[1]
    g = g_ref[...]
    sh = sh_ref[0]
    sc = sc_ref[0]

    def norm_mod(x):
        y = x * lax.rsqrt(jnp.mean(x * x, axis=-1, keepdims=True) + RMS_EPS) * g
        return y * (1.0 + sc) + sh

    p = _dot(norm_mod(x_ref[0]).astype(BF16), w_ref[...])
    halo = jnp.concatenate([xp_ref[0], xn_ref[0]], axis=0)
    ph = _dot(norm_mod(halo).astype(BF16), w_ref[:, :N_SHIFT])
    p_prev = jnp.where(i > 0, ph[:HALO], 0.0)
    p_next = jnp.where(i < n - 1, ph[HALO:], 0.0)
    ext = jnp.concatenate([p_prev, p[:, :N_SHIFT], p_next], axis=0)

    xa = ext[:, :A_WIDTH]
    cw = cw_ref[...]
    cb = cb_ref[...]
    xcf = cb[0:1]
    xcb = cb[1:2]
    for j in range(cw.shape[1]):
        xcf = xcf + cw[0, j:j + 1] * xa[HALO - j:HALO - j + tm]
        xcb = xcb + cw[1, j:j + 1] * xa[HALO + j:HALO + j + tm]
    xcf_ref[0] = xcf
    xcb_ref[0] = xcb

    pb = ext[HALO:HALO + tm, A_WIDTH:]
    prev = ext[HALO - 1:HALO - 1 + tm, A_WIDTH:]
    nxt = ext[HALO + 1:HALO + 1 + tm, A_WIDTH:]
    mu = mu_ref[...]
    sb = pb + mu[0:1] * (prev - pb) + mu[1:2] * (nxt - pb)
    brkv_ref[0] = sb[:, :3 * B_WIDTH]
    blo_ref[0] = sb[:, 3 * B_WIDTH:]

    o = N_SHIFT
    ga_ref[0] = p[:, o:o + A_WIDTH]
    o += A_WIDTH
    q_ref[0] = (p[:, o:o + C_WIDTH] * (HEAD_DIM ** -0.5)).astype(BF16)
    k_ref[0] = p[:, o + C_WIDTH:o + 2 * C_WIDTH].astype(BF16)
    v_ref[0] = p[:, o + 2 * C_WIDTH:o + 3 * C_WIDTH].astype(BF16)


def _inproj(x, g, shift, scale, w_perm, conv_w, conv_b, mu_perm, tm):
    bsz, t, d = x.shape
    nt = t // tm
    hb = tm // HALO
    last = t // HALO - 1
    f = lambda shape, dt=F32: jax.ShapeDtypeStruct(shape, dt)
    blk = lambda w: pl.BlockSpec((1, tm, w), lambda b, i: (b, i, 0))
    full = lambda a: pl.BlockSpec(a.shape, lambda b, i: (0,) * a.ndim)
    return pl.pallas_call(
        _inproj_kernel,
        grid=(bsz, nt),
        in_specs=[blk(d),
                  pl.BlockSpec((1, HALO, d), lambda b, i: (b, jnp.maximum(i * hb - 1, 0), 0)),
                  pl.BlockSpec((1, HALO, d), lambda b, i: (b, jnp.minimum((i + 1) * hb, last), 0)),
                  full(g),
                  pl.BlockSpec((1, 1, d), lambda b, i: (b, 0, 0)),
                  pl.BlockSpec((1, 1, d), lambda b, i: (b, 0, 0)),
                  full(w_perm), full(conv_w), full(conv_b), full(mu_perm)],
        out_specs=[blk(A_WIDTH), blk(A_WIDTH), blk(A_WIDTH), blk(3 * B_WIDTH), blk(256),
                   blk(C_WIDTH), blk(C_WIDTH), blk(C_WIDTH)],
        out_shape=[f((bsz, t, A_WIDTH)), f((bsz, t, A_WIDTH)), f((bsz, t, A_WIDTH)),
                   f((bsz, t, 3 * B_WIDTH)), f((bsz, t, 256)),
                   f((bsz, t, C_WIDTH), BF16), f((bsz, t, C_WIDTH), BF16), f((bsz, t, C_WIDTH), BF16)],
        compiler_params=_params("arbitrary", "arbitrary"),
        name="inproj",
    )(x, x, x, g, shift, scale, w_perm, conv_w, conv_b, mu_perm)


def _chunk_scan(a, b, rev):
    n = a.shape[0]
    row = lax.broadcasted_iota(jnp.int32, a.shape, 0)
    s = 1
    while s < n:
        if rev:
            keep = row < n - s
            a_s = jnp.where(keep, pltpu.roll(a, n - s, 0), 1.0)
            b_s = jnp.where(keep, pltpu.roll(b, n - s, 0), 0.0)
        else:
            keep = row >= s
            a_s = jnp.where(keep, pltpu.roll(a, s, 0), 1.0)
            b_s = jnp.where(keep, pltpu.roll(b, s, 0), 0.0)
        b = a * b_s + b
        a = a * a_s
        s *= 2
    return a, b


def _lru_kernel(xf_ref, xb_ref, h0_ref, wr_ref, br_ref, wi_ref, bi_ref, lam_ref,
                hf_ref, hb_ref, hend_ref, carry):
    i = pl.program_id(1)
    n = pl.num_programs(1)
    ch = xf_ref.shape[1]

    @pl.when(i == 0)
    def _():
        carry[...] = h0_ref[0]

    for d, (x_ref, o_ref) in enumerate(((xf_ref, hf_ref), (xb_ref, hb_ref))):
        x = x_ref[0]
        gate_r = jax.nn.sigmoid(_dot(x, wr_ref[d], HI) + br_ref[d:d + 1])
        gate_i = jax.nn.sigmoid(_dot(x, wi_ref[d], HI) + bi_ref[d:d + 1])
        log_a = -LRU_C * gate_r * _softplus(-lam_ref[d:d + 1])
        a = jnp.exp(log_a)
        b = jnp.sqrt(1.0 - jnp.exp(2.0 * log_a)) * (gate_i * x)
        a_cum, h = _chunk_scan(a, b, rev=bool(d))
        h = h + a_cum * carry[d:d + 1]
        o_ref[0] = h
        carry[d:d + 1] = h[0:1] if d else h[ch - 1:ch]

    @pl.when(i == n - 1)
    def _():
        hend_ref[0] = carry[...]


def _lru_scan(xcf, xcb, h0, wr_bd, br, wi_bd, bi, lam):
    bsz, t, a = xcf.shape
    ch = min(LRU_CHUNK, t)
    nt = t // ch
    full = lambda z: pl.BlockSpec(z.shape, lambda b, i: (0,) * z.ndim)
    fwd = pl.BlockSpec((1, ch, a), lambda b, i: (b, i, 0))
    bwd = pl.BlockSpec((1, ch, a), lambda b, i: (b, nt - 1 - i, 0))
    st = pl.BlockSpec((1, 2, a), lambda b, i: (b, 0, 0))
    return pl.pallas_call(
        _lru_kernel,
        grid=(bsz, nt),
        in_specs=[fwd, bwd, st, full(wr_bd), full(br), full(wi_bd), full(bi), full(lam)],
        out_specs=[fwd, bwd, st],
        out_shape=[jax.ShapeDtypeStruct((bsz, t, a), F32), jax.ShapeDtypeStruct((bsz, t, a), F32),
                   jax.ShapeDtypeStruct((bsz, 2, a), F32)],
        scratch_shapes=[pltpu.VMEM((2, a), F32)],
        compiler_params=_params("arbitrary", "arbitrary"),
        name="lru_scan",
    )(xcf, xcb, h0, wr_bd, br, wi_bd, bi, lam)


def _stack_heads(x):
    nh = x.shape[1] // HEAD_DIM
    return jnp.concatenate([x[:, h * HEAD_DIM:(h + 1) * HEAD_DIM] for h in range(nh)], axis=0)


def _unstack_heads(x, c):
    nh = x.shape[0] // c
    return jnp.concatenate([x[h * c:(h + 1) * c] for h in range(nh)], axis=1)


def _rwkv_chunk(r, k, v, wlo, alo, w0, w2, a0, a2, k_k, k_a, r_k, m_ref, rev):
    c, width = r.shape
    hc = (width // HEAD_DIM) * c
    ones_bd = _head_ones(width).astype(F32)

    w_raw = -_softplus(-(w0 + _dot(jnp.tanh(wlo), w2, HI))) - 0.5
    logw = -jnp.exp(w_raw)
    a_gate = jax.nn.sigmoid(a0 + _dot(alo, a2, HI))
    kk = k * k_k
    kk = kk / jnp.maximum(jnp.sqrt(_dot(kk * kk, ones_bd, HI)), 1e-12)
    k_eff = k * (1.0 + (a_gate - 1.0) * k_a)
    a_vec = -kk
    b_vec = kk * a_gate

    tr = lax.broadcasted_iota(jnp.int32, (c, c), 0)
    tc = lax.broadcasted_iota(jnp.int32, (c, c), 1)
    tri = (tc >= tr) if rev else (tc <= tr)
    cum = _dot(tri.astype(F32), logw, HI)
    total = cum[0:1] if rev else cum[c - 1:c]
    g_inv = jnp.exp(-cum)
    g_end = jnp.exp(total - cum)
    a_t = a_vec * jnp.exp(cum - logw)
    b_t = b_vec * g_inv
    k_t = k_eff * g_inv
    r_t = r * jnp.exp(cum)
    b_end = (b_vec * g_end).astype(BF16)
    k_end = (k_eff * g_end).astype(BF16)

    a_s, b_s, k_s, r_s, v_s = (_stack_heads(z.astype(BF16)) for z in (a_t, b_t, k_t, r_t, v))
    ri = lax.broadcasted_iota(jnp.int32, (hc, hc), 0)
    ci = lax.broadcasted_iota(jnp.int32, (hc, hc), 1)
    same = (ri // c) == (ci // c)
    earlier = (ci > ri) if rev else (ci < ri)
    strict = same & earlier
    incl = same & (earlier | (ri == ci))
    n_ab = jnp.where(strict, _dot_nt(a_s, b_s), 0.0)
    g_ak = jnp.where(strict, _dot_nt(a_s, k_s), 0.0).astype(BF16)
    g_rb = jnp.where(incl, _dot_nt(r_s, b_s), 0.0).astype(BF16)
    g_rk = jnp.where(incl, _dot_nt(r_s, k_s), 0.0).astype(BF16)

    t_inv = (ri == ci).astype(F32) + n_ab
    pw = n_ab.astype(BF16)
    s = 2
    while s < c:
        pw = _dot(pw, pw).astype(BF16)
        t_inv = t_inv + _dot(t_inv.astype(BF16), pw)
        s *= 2

    rhs = jnp.concatenate([a_s, _dot(g_ak, v_s).astype(BF16)], axis=1)
    aw = _dot(t_inv.astype(BF16), rhs).astype(BF16)
    ry = _dot(g_rb, aw)
    r_hat = r_s.astype(F32) + ry[:, :HEAD_DIM]
    y0 = ry[:, HEAD_DIM:] + _dot(g_rk, v_s)
    a_hat, w2_l, r_hat, y0 = (_unstack_heads(z, c) for z in (aw[:, :HEAD_DIM], aw[:, HEAD_DIM:], r_hat, y0))

    m = m_ref[...].astype(BF16)
    y = _dot(r_hat.astype(BF16), m) + y0
    wi = lax.broadcasted_iota(jnp.int32, (width, width), 0)
    wj = lax.broadcasted_iota(jnp.int32, (width, width), 1)
    bd = (wi // HEAD_DIM) == (wj // HEAD_DIM)
    p_m = jnp.where(bd, _dot_tn(b_end, a_hat), 0.0) + jnp.where(wi == wj, jnp.exp(total), 0.0)
    q_m = jnp.where(bd, _dot_tn(b_end, w2_l) + _dot_tn(k_end, v.astype(BF16)), 0.0)
    m_ref[...] = _dot(p_m.astype(BF16), m) + q_m
    bonus_v = _dot(r * k_eff * r_k, ones_bd, HI) * v
    return y, bonus_v


def _rwkv_kernel(rkvf_ref, lof_ref, rkvb_ref, lob_ref, m0_ref, w0_ref, w2_ref, a0_ref, a2_ref,
                 kk_ref, ka_ref, rk_ref, yf_ref, yb_ref, bvf_ref, bvb_ref, mend_ref, m_f, m_b):
    i = pl.program_id(1)
    n = pl.num_programs(1)

    @pl.when(i == 0)
    def _():
        m_f[...] = m0_ref[0, 0]
        m_b[...] = m0_ref[0, 1]

    w = B_WIDTH
    for d, (rkv_ref, lo_ref, y_ref, bv_ref, m_ref) in enumerate(
            ((rkvf_ref, lof_ref, yf_ref, bvf_ref, m_f), (rkvb_ref, lob_ref, yb_ref, bvb_ref, m_b))):
        rkv = rkv_ref[0]
        lo = lo_ref[0]
        y, bv = _rwkv_chunk(rkv[:, :w], rkv[:, w:2 * w], rkv[:, 2 * w:], lo[:, :LORA], lo[:, LORA:2 * LORA],
                            w0_ref[d:d + 1], w2_ref[d], a0_ref[d:d + 1], a2_ref[d],
                            kk_ref[...], ka_ref[...], rk_ref[...], m_ref, rev=bool(d))
        y_ref[0] = y
        bv_ref[0] = bv

    @pl.when(i == n - 1)
    def _():
        mend_ref[0, 0] = m_f[...]
        mend_ref[0, 1] = m_b[...]


def _rwkv_scan(brkv, blo, m0, w0, w2, a0, a2, k_k, k_a, r_k):
    bsz, t, _ = brkv.shape
    c = RWKV_CHUNK
    nt = t // c
    w = B_WIDTH
    full = lambda z: pl.BlockSpec(z.shape, lambda b, i: (0,) * z.ndim)
    fwd = lambda width: pl.BlockSpec((1, c, width), lambda b, i: (b, i, 0))
    bwd = lambda width: pl.BlockSpec((1, c, width), lambda b, i: (b, nt - 1 - i, 0))
    st = pl.BlockSpec((1, 2, w, w), lambda b, i: (b, 0, 0, 0))
    o = jax.ShapeDtypeStruct((bsz, t, w), F32)
    return pl.pallas_call(
        _rwkv_kernel,
        grid=(bsz, nt),
        in_specs=[fwd(3 * w), fwd(256), bwd(3 * w), bwd(256), st,
                  full(w0), full(w2), full(a0), full(a2), full(k_k), full(k_a), full(r_k)],
        out_specs=[fwd(w), bwd(w), fwd(w), bwd(w), st],
        out_shape=[o, o, o, o, jax.ShapeDtypeStruct((bsz, 2, w, w), F32)],
        scratch_shapes=[pltpu.VMEM((w, w), F32), pltpu.VMEM((w, w), F32)],
        compiler_params=_params("arbitrary", "arbitrary"),
        name="rwkv_scan",
    )(brkv, blo, brkv, blo, m0, w0, w2, a0, a2, k_k, k_a, r_k)


def _na_kernel(q_ref, k_ref, v_ref, kc_ref, vc_ref, bias_ref, o_ref):
    r = pl.program_id(1)
    rows = pl.num_programs(1)
    kh = bias_ref.shape[3] // GRID_W
    rs = jnp.clip(r - kh // 2, 0, rows - kh)
    start = pl.multiple_of(rs * GRID_W, GRID_W)
    q = q_ref[0]
    kb = k_ref[0, pl.ds(start, kh * GRID_W), :]
    vb = v_ref[0, pl.ds(start, kh * GRID_W), :]
    kc = kc_ref[0]
    vc = vc_ref[0]
    outs = []
    for h in range(q.shape[1] // HEAD_DIM):
        sl = slice(h * HEAD_DIM, (h + 1) * HEAD_DIM)
        s_w = _dot_nt(q[:, sl], kb[:, sl]) + bias_ref[0, h]
        s_c = _dot_nt(q[:, sl], kc[:, sl])
        m = jnp.maximum(jnp.max(s_w, axis=-1, keepdims=True), jnp.max(s_c, axis=-1, keepdims=True))
        p_w = jnp.exp(s_w - m)
        p_c = jnp.exp(s_c - m)
        den = jnp.sum(p_w, axis=-1, keepdims=True) + jnp.sum(p_c, axis=-1, keepdims=True)
        o = _dot(p_w.astype(BF16), vb[:, sl]) + _dot(p_c.astype(BF16), vc[:, sl])
        outs.append(o / den)
    o_ref[0] = jnp.concatenate(outs, axis=1)


def _na_bias_table(rpb, rows):
    kh = min(WIN_H, rows)
    var = np.arange(kh)[:, None, None, None]
    qcol = np.arange(GRID_W)[None, :, None, None]
    krow = np.arange(kh)[None, None, :, None]
    kcol = np.arange(GRID_W)[None, None, None, :]
    cstart = np.clip(qcol - WIN_W // 2, 0, GRID_W - WIN_W)
    valid = (kcol >= cstart) & (kcol < cstart + WIN_W)
    ridx = np.broadcast_to(krow - var + (WIN_H - 1), (kh, GRID_W, kh, GRID_W))
    cidx = np.broadcast_to(np.clip(kcol - qcol + (WIN_W - 1), 0, 2 * WIN_W - 2), (kh, GRID_W, kh, GRID_W))
    valid = np.broadcast_to(valid, (kh, GRID_W, kh, GRID_W))
    tab = rpb[:, ridx, cidx]
    tab = jnp.where(valid[None], tab, MASK_VALUE)
    nh = rpb.shape[0]
    return jnp.transpose(tab, (1, 0, 2, 3, 4)).reshape(kh, nh, GRID_W, kh * GRID_W)


def _na_attention(q, k, v, kc, vc, bias_tab):
    bsz, seq, cw = q.shape
    ctx_len = kc.shape[1]
    rows = seq // GRID_W
    kh = bias_tab.shape[0]
    nh = bias_tab.shape[1]

    def bias_map(b, r):
        return (r - jnp.clip(r - kh // 2, 0, rows - kh), 0, 0, 0)

    return pl.pallas_call(
        _na_kernel,
        grid=(bsz, rows),
        in_specs=[pl.BlockSpec((1, GRID_W, cw), lambda b, r: (b, r, 0)),
                  pl.BlockSpec((1, seq, cw), lambda b, r: (b, 0, 0)),
                  pl.BlockSpec((1, seq, cw), lambda b, r: (b, 0, 0)),
                  pl.BlockSpec((1, ctx_len, cw), lambda b, r: (b, 0, 0)),
                  pl.BlockSpec((1, ctx_len, cw), lambda b, r: (b, 0, 0)),
                  pl.BlockSpec((1, nh, GRID_W, kh * GRID_W), bias_map)],
        out_specs=pl.BlockSpec((1, GRID_W, cw), lambda b, r: (b, r, 0)),
        out_shape=jax.ShapeDtypeStruct((bsz, seq, cw), F32),
        compiler_params=_params("arbitrary", "arbitrary"),
        name="na_attention",
    )(q, k, v, kc, vc, bias_tab)


def _ctx_attn_kernel(q_ref, k_ref, v_ref, o_ref):
    q = q_ref[0]
    k = k_ref[0]
    v = v_ref[0]
    outs = []
    for h in range(q.shape[1] // HEAD_DIM):
        sl = slice(h * HEAD_DIM, (h + 1) * HEAD_DIM)
        s = _dot_nt(q[:, sl], k[:, sl])
        p = jnp.exp(s - jnp.max(s, axis=-1, keepdims=True))
        outs.append(_dot(p.astype(BF16), v[:, sl]) / jnp.sum(p, axis=-1, keepdims=True))
    o_ref[0] = jnp.concatenate(outs, axis=1)


def _ctx_attention(q, k, v):
    bsz, n, cw = q.shape
    spec = pl.BlockSpec((1, n, cw), lambda b: (b, 0, 0))
    return pl.pallas_call(
        _ctx_attn_kernel, grid=(bsz,), in_specs=[spec, spec, spec], out_specs=spec,
        out_shape=jax.ShapeDtypeStruct((bsz, n, cw), F32),
        compiler_params=_params("arbitrary"), name="ctx_attention",
    )(q, k, v)


def _outproj_kernel(x_ref, hf_ref, hb_ref, ga_ref, yf_ref, yb_ref, bvf_ref, bvb_ref, lo_ref, yc_ref,
                    wo_ref, g2_ref, lng_ref, lnb_ref, gate_ref, g_ref, sh_ref, sc_ref, rw_ref, rb_ref,
                    xo_ref, h_ref, lg_ref):
    ya = jax.nn.gelu(ga_ref[0]) * (hf_ref[0] + hb_ref[0])

    y = yf_ref[0] + yb_ref[0]
    avg = _head_ones(B_WIDTH).astype(F32) * (1.0 / HEAD_DIM)
    mean = _dot(y, avg, HI)
    yc_ = y - mean
    var = _dot(yc_ * yc_, avg, HI)
    yn = yc_ * lax.rsqrt(var + GN_EPS) * lng_ref[...] + lnb_ref[...] + bvf_ref[0] + bvb_ref[0]
    gate_b = _dot(jax.nn.sigmoid(lo_ref[0][:, 2 * LORA:3 * LORA]), g2_ref[...], HI)
    yb = yn * gate_b

    mix = (_dot(ya.astype(BF16), wo_ref[:A_WIDTH]) + _dot(yb.astype(BF16), wo_ref[A_WIDTH:A_WIDTH + B_WIDTH])
           + _dot(yc_ref[0].astype(BF16), wo_ref[A_WIDTH + B_WIDTH:]))
    x = x_ref[0] + gate_ref[0] * mix
    xo_ref[0] = x
    hn = x * lax.rsqrt(jnp.mean(x * x, axis=-1, keepdims=True) + RMS_EPS) * g_ref[...]
    hn = hn * (1.0 + sc_ref[0]) + sh_ref[0]
    h_ref[0] = hn
    lg_ref[0] = _dot(hn, rw_ref[...], HI) + rb_ref[...]


def _outproj(x, hf, hb, ga, yf, yb, bvf, bvb, blo, yc, wo, g2, lng, lnb, gate, g, shift, scale, rw, rb, tm):
    bsz, t, d = x.shape
    ne = rw.shape[1]
    blk = lambda w: pl.BlockSpec((1, tm, w), lambda b, i: (b, i, 0))
    full = lambda a: pl.BlockSpec(a.shape, lambda b, i: (0,) * a.ndim)
    vec = pl.BlockSpec((1, 1, d), lambda b, i: (b, 0, 0))
    return pl.pallas_call(
        _outproj_kernel,
        grid=(bsz, t // tm),
        in_specs=[blk(d), blk(A_WIDTH), blk(A_WIDTH), blk(A_WIDTH), blk(B_WIDTH), blk(B_WIDTH),
                  blk(B_WIDTH), blk(B_WIDTH), blk(256), blk(C_WIDTH),
                  full(wo), full(g2), full(lng), full(lnb), vec, full(g), vec, vec, full(rw), full(rb)],
        out_specs=[blk(d), blk(d), blk(ne)],
        out_shape=[jax.ShapeDtypeStruct((bsz, t, d), F32), jax.ShapeDtypeStruct((bsz, t, d), F32),
                   jax.ShapeDtypeStruct((bsz, t, ne), F32)],
        compiler_params=_params("arbitrary", "arbitrary"),
        name="outproj",
    )(x, hf, hb, ga, yf, yb, bvf, bvb, blo, yc, wo, g2, lng, lnb, gate, g, shift, scale, rw, rb)


def _expert_kernel(be_ref, nb_ref, x_ref, wgu_ref, bgu_ref, wdn_ref, bdn_ref, o_ref, wgu_s, wdn_s):
    i = pl.program_id(0)
    e = be_ref[i]
    fresh = jnp.logical_or(i == 0, be_ref[jnp.maximum(i - 1, 0)] != e)

    @pl.when(fresh)
    def _():
        wgu_s[...] = wgu_ref[0].astype(BF16)
        wdn_s[...] = wdn_ref[0].astype(BF16)

    @pl.when(i < nb_ref[0])
    def _():
        dff = wdn_s.shape[0]
        gu = _dot(x_ref[...].astype(BF16), wgu_s[...]) + bgu_ref[0]
        g_ = jnp.minimum(gu[:, :dff], SWIGLU_LIMIT)
        u_ = jnp.clip(gu[:, dff:], -SWIGLU_LIMIT, SWIGLU_LIMIT)
        act = (u_ + 1.0) * (g_ * jax.nn.sigmoid(SWIGLU_ALPHA * g_))
        o_ref[...] = _dot(act.astype(BF16), wdn_s[...]) + bdn_ref[0]

    @pl.when(i >= nb_ref[0])
    def _():
        o_ref[...] = jnp.zeros_like(o_ref)


def _experts(block_e, n_used, xb, w_gu, b_gu, w_dn, b_dn):
    cap, d = xb.shape
    ne, _, f2 = w_gu.shape
    dff = w_dn.shape[1]
    nb = cap // MOE_BM
    grid_spec = pltpu.PrefetchScalarGridSpec(
        num_scalar_prefetch=2,
        grid=(nb,),
        in_specs=[pl.BlockSpec((MOE_BM, d), lambda i, be, nu: (i, 0)),
                  pl.BlockSpec((1, d, f2), lambda i, be, nu: (be[i], 0, 0)),
                  pl.BlockSpec((1, 1, f2), lambda i, be, nu: (be[i], 0, 0)),
                  pl.BlockSpec((1, dff, d), lambda i, be, nu: (be[i], 0, 0)),
                  pl.BlockSpec((1, 1, d), lambda i, be, nu: (be[i], 0, 0))],
        out_specs=pl.BlockSpec((MOE_BM, d), lambda i, be, nu: (i, 0)),
        scratch_shapes=[pltpu.VMEM((d, f2), BF16), pltpu.VMEM((dff, d), BF16)],
    )
    return pl.pallas_call(
        _expert_kernel, grid_spec=grid_spec,
        out_shape=jax.ShapeDtypeStruct((cap, d), F32),
        compiler_params=_params("arbitrary"), name="experts",
    )(block_e, n_used, xb, w_gu, b_gu.reshape(ne, 1, f2), w_dn, b_dn.reshape(ne, 1, d))


def _route_kernel(lg_ref, idx_ref, gate_ref, rank_ref, cnt_ref, carry):
    i = pl.program_id(0)

    @pl.when(i == 0)
    def _():
        carry[...] = jnp.zeros_like(carry)

    l = lg_ref[...]
    tm, ne = l.shape
    lane = lax.broadcasted_iota(jnp.int32, l.shape, 1)
    vals, idxs, hots = [], [], []
    for _ in range(TOP_K):
        m = jnp.max(l, axis=-1, keepdims=True)
        ix = jnp.min(jnp.where(l == m, lane, ne), axis=-1, keepdims=True)
        hot = lane == ix
        vals.append(m)
        idxs.append(ix)
        hots.append(hot)
        l = jnp.where(hot, -jnp.inf, l)
    ex = [jnp.exp(v - vals[0]) for v in vals]
    den = ex[0] + ex[1] + ex[2] + ex[3]
    hot_all = (hots[0] | hots[1] | hots[2] | hots[3]).astype(BF16)
    tr = lax.broadcasted_iota(jnp.int32, (tm, tm), 0)
    tc = lax.broadcasted_iota(jnp.int32, (tm, tm), 1)
    before = _dot((tc < tr).astype(BF16), hot_all) + carry[...]
    ranks = [jnp.sum(jnp.where(h, before, 0.0), axis=-1, keepdims=True) for h in hots]
    carry[...] = carry[...] + jnp.sum(hot_all.astype(F32), axis=0, keepdims=True)
    idx_ref[...] = jnp.concatenate(idxs, axis=1)
    gate_ref[...] = jnp.concatenate([e / den for e in ex], axis=1)
    rank_ref[...] = jnp.concatenate(ranks, axis=1).astype(jnp.int32)
    cnt_ref[...] = carry[...].astype(jnp.int32)


def _route(logits):
    n, ne = logits.shape
    tm = ROUTE_TM
    assert n % tm == 0, (n, tm)
    blk = pl.BlockSpec((tm, TOP_K), lambda i: (i, 0))
    return pl.pallas_call(
        _route_kernel,
        grid=(n // tm,),
        in_specs=[pl.BlockSpec((tm, ne), lambda i: (i, 0))],
        out_specs=[blk, blk, blk, pl.BlockSpec((1, ne), lambda i: (0, 0))],
        out_shape=[jax.ShapeDtypeStruct((n, TOP_K), jnp.int32), jax.ShapeDtypeStruct((n, TOP_K), F32),
                   jax.ShapeDtypeStruct((n, TOP_K), jnp.int32), jax.ShapeDtypeStruct((1, ne), jnp.int32)],
        scratch_shapes=[pltpu.VMEM((1, ne), F32)],
        compiler_params=_params("arbitrary"),
        name="route",
    )(logits)


def _dispatch_kernel(dest_hbm, h_hbm, xb_init, xb_hbm, dest_s, sem_idx, sem):
    del xb_init
    i = pl.program_id(0)
    n = pl.num_programs(0)
    tm = dest_s.shape[0] // TOP_K
    base = i * tm
    cp = pltpu.make_async_copy(dest_hbm.at[pl.ds(base * TOP_K, tm * TOP_K)], dest_s, sem_idx)
    cp.start()
    cp.wait()

    def issue(t, carry):
        src = h_hbm.at[pl.ds(base + t, 1)]
        for j in range(TOP_K):
            pltpu.make_async_copy(src, xb_hbm.at[pl.ds(dest_s[t * TOP_K + j], 1)], sem).start()
        return carry

    lax.fori_loop(0, tm, issue, 0, unroll=8)

    def drain():
        for _ in range(TOP_K):
            pltpu.make_async_copy(h_hbm.at[pl.ds(0, tm)], xb_hbm.at[pl.ds(0, tm)], sem).wait()

    pl.when(i > 0)(drain)
    pl.when(i == n - 1)(drain)


def _dispatch(dest_flat, h, cap):
    n, d = h.shape
    tm = DISPATCH_TM
    assert n % tm == 0, (n, tm)
    any_spec = pl.BlockSpec(memory_space=pl.ANY)
    return pl.pallas_call(
        _dispatch_kernel,
        grid=(n // tm,),
        in_specs=[any_spec, any_spec, any_spec],
        out_specs=any_spec,
        out_shape=jax.ShapeDtypeStruct((cap, d), F32),
        scratch_shapes=[pltpu.SMEM((tm * TOP_K,), jnp.int32), pltpu.SemaphoreType.DMA, pltpu.SemaphoreType.DMA],
        input_output_aliases={2: 0},
        compiler_params=pltpu.CompilerParams(dimension_semantics=("arbitrary",), has_side_effects=True),
        name="dispatch",
    )(dest_flat, h, jnp.zeros((cap, d), F32))


def _combine_kernel(dest_hbm, yb_hbm, x_ref, gt_ref, mg_ref, g_ref, o_ref, dest_s, buf, sem_idx, sem,
                    *, tok0, final):
    i = pl.program_id(0)
    n = pl.num_programs(0)
    tm = x_ref.shape[0]

    def fetch(blk, slot):
        cp = pltpu.make_async_copy(dest_hbm.at[pl.ds((tok0 + blk * tm) * TOP_K, tm * TOP_K)], dest_s.at[slot],
                                   sem_idx)
        cp.start()
        cp.wait()

        def issue(t, carry):
            for j in range(TOP_K):
                pltpu.make_async_copy(yb_hbm.at[pl.ds(dest_s[slot, t * TOP_K + j], 1)],
                                      buf.at[slot, j, pl.ds(t, 1)], sem.at[slot]).start()
            return carry

        lax.fori_loop(0, tm, issue, 0, unroll=8)

    @pl.when(i == 0)
    def _():
        fetch(0, 0)

    @pl.when(i + 1 < n)
    def _():
        fetch(i + 1, (i + 1) % 2)

    slot = i % 2
    for j in range(TOP_K):
        pltpu.make_async_copy(yb_hbm.at[pl.ds(0, tm)], buf.at[slot, j], sem.at[slot]).wait()
    gt = gt_ref[...]
    y = gt[:, 0:1] * buf[slot, 0]
    for j in range(1, TOP_K):
        y = y + gt[:, j:j + 1] * buf[slot, j]
    x = x_ref[...] + mg_ref[0] * y
    if final:
        x = x * lax.rsqrt(jnp.mean(x * x, axis=-1, keepdims=True) + RMS_EPS) * g_ref[...]
    o_ref[...] = x


def _combine(dest_flat, yb, x, gates, mod_gate, g, tok0, final):
    bsz, t, d = x.shape
    n = bsz * t
    tm = min(COMBINE_TM, t)
    per_b = t // tm
    any_spec = pl.BlockSpec(memory_space=pl.ANY)
    out = pl.pallas_call(
        functools.partial(_combine_kernel, tok0=tok0, final=final),
        grid=(n // tm,),
        in_specs=[any_spec, any_spec,
                  pl.BlockSpec((tm, d), lambda i: (i, 0)),
                  pl.BlockSpec((tm, TOP_K), lambda i: (i, 0)),
                  pl.BlockSpec((1, 1, d), lambda i: (i // per_b, 0, 0)),
                  pl.BlockSpec((1, d), lambda i: (0, 0))],
        out_specs=pl.BlockSpec((tm, d), lambda i: (i, 0)),
        out_shape=jax.ShapeDtypeStruct((n, d), F32),
        scratch_shapes=[pltpu.SMEM((2, tm * TOP_K), jnp.int32), pltpu.VMEM((2, TOP_K, tm, d), F32),
                        pltpu.SemaphoreType.DMA, pltpu.SemaphoreType.DMA((2,))],
        compiler_params=_params("arbitrary"),
        name="combine",
    )(dest_flat, yb, x.reshape(n, d), gates, mod_gate, g)
    return out.reshape(bsz, t, d)


def _moe_plan(logits):
    n, ne = logits.shape
    idx, gates, rank, counts = _route(logits)
    counts = counts[0]
    padded = (counts + MOE_BM - 1) // MOE_BM * MOE_BM
    pad_end = jnp.cumsum(padded)
    pad_start = pad_end - padded
    hot = idx[:, :, None] == jnp.arange(ne, dtype=jnp.int32)[None, None, :]
    dest = jnp.sum(jnp.where(hot, pad_start[None, None, :], 0), axis=-1) + rank
    nb = (n * TOP_K + ne * (MOE_BM - 1) + MOE_BM - 1) // MOE_BM
    block_e = jnp.minimum(jnp.searchsorted(pad_end, jnp.arange(nb, dtype=jnp.int32) * MOE_BM, side='right'),
                          ne - 1).astype(jnp.int32)
    n_used = (pad_end[-1] // MOE_BM).astype(jnp.int32).reshape(1)
    return dest.reshape(-1).astype(jnp.int32), gates, block_e, n_used, nb * MOE_BM


def _block_diag(w):
    nh, n, _ = w.shape
    out = jnp.zeros((nh * n, nh * n), w.dtype)
    for h in range(nh):
        out = out.at[h * n:(h + 1) * n, h * n:(h + 1) * n].set(w[h])
    return out


def _permute_in_cols(w_in):
    a0 = 0
    b0 = 2 * A_WIDTH
    c0 = b0 + 3 * B_WIDTH + 3 * LORA
    d = w_in.shape[0]
    pad = jnp.zeros((d, 256 - 3 * LORA), w_in.dtype)
    return jnp.concatenate([w_in[:, a0:a0 + A_WIDTH], w_in[:, b0:b0 + 3 * B_WIDTH],
                            w_in[:, b0 + 3 * B_WIDTH:c0], pad,
                            w_in[:, A_WIDTH:2 * A_WIDTH], w_in[:, c0:]], axis=1)


def _mixers(p_ctx, p_lat, lp, need_ctx):
    (xcf_c, xcb_c, ga_c, brkv_c, blo_c, q_c, k_c, v_c) = p_ctx
    (xcf_l, xcb_l, ga_l, brkv_l, blo_l, q_l, k_l, v_l) = p_lat
    bsz = xcf_c.shape[0]

    lru_w = (lp['wr_bd'], lp['br'], lp['wi_bd'], lp['bi'], lp['lam'])
    hf_c, hb_c, hend = _lru_scan(xcf_c, xcb_c, jnp.zeros((bsz, 2, A_WIDTH), F32), *lru_w)
    hf_l, hb_l, _ = _lru_scan(xcf_l, xcb_l, hend, *lru_w)

    rw_w = (lp['w0'], lp['w2'], lp['a0'], lp['a2'], lp['k_k'], lp['k_a'], lp['r_k'])
    yf_c, yb_c, bvf_c, bvb_c, mend = _rwkv_scan(brkv_c, blo_c, jnp.zeros((bsz, 2, B_WIDTH, B_WIDTH), F32), *rw_w)
    yf_l, yb_l, bvf_l, bvb_l, _ = _rwkv_scan(brkv_l, blo_l, mend, *rw_w)

    yc_l = _na_attention(q_l, k_l, v_l, k_c, v_c, lp['bias_tab'])
    lat = (hf_l, hb_l, ga_l, yf_l, yb_l, bvf_l, bvb_l, blo_l, yc_l)
    ctx = None
    if need_ctx:
        yc_c = _ctx_attention(q_c, k_c, v_c)
        ctx = (hf_c, hb_c, ga_c, yf_c, yb_c, bvf_c, bvb_c, blo_c, yc_c)
    return ctx, lat


def kernel(x, c, ctx, c_ctx, ada_w, ada_b, norm_mix_g, norm_ffn_g, w_in, w_out, lru_conv_w, lru_conv_b, lru_wr, lru_br, lru_wi, lru_bi, lru_lambda, rwkv_mu, rwkv_w0, rwkv_w2, rwkv_a0, rwkv_a2, rwkv_g2, rwkv_kk, rwkv_ka, rwkv_rk, rwkv_lnx_g, rwkv_lnx_b, na_rpb, router_w, router_b, moe_w_gu, moe_b_gu, moe_w_dn, moe_b_dn, final_g):
    bsz, seq, d = x.shape
    ctx_len = ctx.shape[1]
    depth = ada_w.shape[0]
    rows = seq // GRID_W
    tm_l = 256
    tm_c = min(256, ctx_len)

    cond = jnp.concatenate([c, c_ctx[None], jnp.zeros((8 - bsz - 1, d), F32)], axis=0)
    mod = _adaln(cond, ada_w, ada_b)

    xl, xc = x, ctx
    for l in range(depth):
        last = l == depth - 1
        ml = [mod[l, :bsz, j * d:(j + 1) * d][:, None, :] for j in range(6)]
        mc = [jnp.broadcast_to(mod[l, bsz:bsz + 1, j * d:(j + 1) * d][:, None, :], (bsz, 1, d)) for j in range(6)]
        n_b = 3 * B_WIDTH
        mu = rwkv_mu[l]
        lp = dict(
            wr_bd=jnp.stack([_block_diag(lru_wr[l, dd]) for dd in range(2)]),
            wi_bd=jnp.stack([_block_diag(lru_wi[l, dd]) for dd in range(2)]),
            br=lru_br[l], bi=lru_bi[l], lam=lru_lambda[l],
            w0=rwkv_w0[l], w2=rwkv_w2[l], a0=rwkv_a0[l], a2=rwkv_a2[l],
            k_k=rwkv_kk[l][None], k_a=rwkv_ka[l][None], r_k=rwkv_rk[l].reshape(1, B_WIDTH),
            bias_tab=_na_bias_table(na_rpb[l], rows),
        )
        w_perm = _permute_in_cols(w_in[l]).astype(BF16)
        mu_perm = jnp.concatenate([mu[:, :n_b], mu[:, n_b:], jnp.zeros((2, 256 - 3 * LORA), F32)], axis=1)
        g_mix = norm_mix_g[l][None]
        g_ffn = norm_ffn_g[l][None]
        wo = w_out[l].astype(BF16)

        p_lat = _inproj(xl, g_mix, ml[0], ml[1], w_perm, lru_conv_w[l], lru_conv_b[l], mu_perm, tm_l)
        p_ctx = _inproj(xc, g_mix, mc[0], mc[1], w_perm, lru_conv_w[l], lru_conv_b[l], mu_perm, tm_c)
        mix_c, mix_l = _mixers(p_ctx, p_lat, lp, not last)

        fin = (wo, rwkv_g2[l], rwkv_lnx_g[l][None], rwkv_lnx_b[l][None])
        rt = (router_w[l], router_b[l][None])
        xl, hl, lg_l = _outproj(xl, *mix_l, *fin, ml[2], g_ffn, ml[3], ml[4], *rt, tm_l)
        ne = router_w.shape[2]
        if last:
            tok = hl.reshape(-1, d)
            lg = lg_l.reshape(-1, ne)
            nc = 0
        else:
            xc, hc, lg_c = _outproj(xc, *mix_c, *fin, mc[2], g_ffn, mc[3], mc[4], *rt, tm_c)
            tok = jnp.concatenate([hc.reshape(-1, d), hl.reshape(-1, d)], axis=0)
            lg = jnp.concatenate([lg_c.reshape(-1, ne), lg_l.reshape(-1, ne)], axis=0)
            nc = bsz * ctx_len
        dest, gates, block_e, n_used, cap = _moe_plan(lg)
        xb = _dispatch(dest, tok, cap)
        yb = _experts(block_e, n_used, xb, moe_w_gu[l], moe_b_gu[l], moe_w_dn[l], moe_b_dn[l])
        if not last:
            xc = _combine(dest, yb, xc, gates[:nc], mc[5], final_g[None], 0, False)
        xl = _combine(dest, yb, xl, gates[nc:], ml[5], final_g[None], nc, last)
    return xl
```

```python
import functools

import numpy as np
import jax
import jax.numpy as jnp
from jax import lax
from jax.experimental import pallas as pl
from jax.experimental.pallas import tpu as pltpu

F32 = jnp.float32
BF16 = jnp.bfloat16
HI = lax.Precision.HIGHEST

RMS_EPS = 1e-6
GN_EPS = 64e-5
LRU_C = 8.0
SWIGLU_ALPHA = 1.702
SWIGLU_LIMIT = 7.0
TOP_K = 4
GRID_W = 64
WIN_H = 8
WIN_W = 16
HEAD_DIM = 64
A_WIDTH = 256
B_WIDTH = 256
C_WIDTH = 512
LORA = 64
HALO = 8
RWKV_CHUNK = 64
LRU_CHUNK = 256
MOE_BM = 256
ROUTE_TM = 512
DISPATCH_TM = 256
COMBINE_TM = 128
MASK_VALUE = -1e30
VMEM_LIMIT = 56 * 1024 * 1024


def _dot(a, b, prec=None):
    return jnp.dot(a, b, preferred_element_type=F32, precision=prec)


def _dot_nt(a, b, prec=None):
    return lax.dot_general(a, b, (((1,), (1,)), ((), ())), preferred_element_type=F32, precision=prec)


def _dot_tn(a, b, prec=None):
    return lax.dot_general(a, b, (((0,), (0,)), ((), ())), preferred_element_type=F32, precision=prec)


def _softplus(z):
    return jnp.maximum(z, 0.0) + jnp.log1p(jnp.exp(-jnp.abs(z)))


def _params(*sem):
    return pltpu.CompilerParams(dimension_semantics=sem, vmem_limit_bytes=VMEM_LIMIT)


def _head_ones(width):
    r = lax.broadcasted_iota(jnp.int32, (width, width), 0) // HEAD_DIM
    c = lax.broadcasted_iota(jnp.int32, (width, width), 1) // HEAD_DIM
    return r == c


def _adaln_kernel(cond_ref, w_ref, b_ref, o_ref):
    c = cond_ref[...]
    o_ref[0] = _dot(c * jax.nn.sigmoid(c), w_ref[0], HI) + b_ref[0]


def _adaln(cond, ada_w, ada_b):
    depth, d, n = ada_w.shape
    tn = 1536
    return pl.pallas_call(
        _adaln_kernel,
        grid=(depth, n // tn),
        in_specs=[pl.BlockSpec((8, d), lambda l, j: (0, 0)),
                  pl.BlockSpec((1, d, tn), lambda l, j: (l, 0, j)),
                  pl.BlockSpec((1, 1, tn), lambda l, j: (l, 0, j))],
        out_specs=pl.BlockSpec((1, 8, tn), lambda l, j: (l, 0, j)),
        out_shape=jax.ShapeDtypeStruct((depth, 8, n), F32),
        compiler_params=_params("arbitrary", "arbitrary"),
        name="adaln",
    )(cond, ada_w, ada_b.reshape(depth, 1, n))


N_SHIFT = 1280
N_PROJ = 3072


def _inproj_kernel(x_ref, xp_ref, xn_ref, g_ref, sh_ref, sc_ref, w_ref, cw_ref, cb_ref, mu_ref,
                   xcf_ref, xcb_ref, ga_ref, brkv_ref, blo_ref, q_ref, k_ref, v_ref):
    i = pl.program_id(1)
    n = pl.num_programs(1)
    tm = x_ref.shape[1]
    g = g_ref[...]
    sh = sh_ref[0]
    sc = sc_ref[0]

    def norm_mod(x):
        y = x * lax.rsqrt(jnp.mean(x * x, axis=-1, keepdims=True) + RMS_EPS) * g
        return y * (1.0 + sc) + sh

    p = _dot(norm_mod(x_ref[0]).astype(BF16), w_ref[...])
    halo = jnp.concatenate([xp_ref[0], xn_ref[0]], axis=0)
    ph = _dot(norm_mod(halo).astype(BF16), w_ref[:, :N_SHIFT])
    p_prev = jnp.where(i > 0, ph[:HALO], 0.0)
    p_next = jnp.where(i < n - 1, ph[HALO:], 0.0)
    ext = jnp.concatenate([p_prev, p[:, :N_SHIFT], p_next], axis=0)

    xa = ext[:, :A_WIDTH]
    cw = cw_ref[...]
    cb = cb_ref[...]
    xcf = cb[0:1]
    xcb = cb[1:2]
    for j in range(cw.shape[1]):
        xcf = xcf + cw[0, j:j + 1] * xa[HALO - j:HALO - j + tm]
        xcb = xcb + cw[1, j:j + 1] * xa[HALO + j:HALO + j + tm]
    xcf_ref[0] = xcf
    xcb_ref[0] = xcb

    pb = ext[HALO:HALO + tm, A_WIDTH:]
    prev = ext[HALO - 1:HALO - 1 + tm, A_WIDTH:]
    nxt = ext[HALO + 1:HALO + 1 + tm, A_WIDTH:]
    mu = mu_ref[...]
    sb = pb + mu[0:1] * (prev - pb) + mu[1:2] * (nxt - pb)
    brkv_ref[0] = sb[:, :3 * B_WIDTH]
    blo_ref[0] = sb[:, 3 * B_WIDTH:]

    o = N_SHIFT
    ga_ref[0] = p[:, o:o + A_WIDTH]
    o += A_WIDTH
    q_ref[0] = (p[:, o:o + C_WIDTH] * (HEAD_DIM ** -0.5)).astype(BF16)
    k_ref[0] = p[:, o + C_WIDTH:o + 2 * C_WIDTH].astype(BF16)
    v_ref[0] = p[:, o + 2 * C_WIDTH:o + 3 * C_WIDTH].astype(BF16)


def _inproj(x, g, shift, scale, w_perm, conv_w, conv_b, mu_perm, tm):
    bsz, t, d = x.shape
    nt = t // tm
    hb = tm // HALO
    last = t // HALO - 1
    f = lambda shape, dt=F32: jax.ShapeDtypeStruct(shape, dt)
    blk = lambda w: pl.BlockSpec((1, tm, w), lambda b, i: (b, i, 0))
    full = lambda a: pl.BlockSpec(a.shape, lambda b, i: (0,) * a.ndim)
    return pl.pallas_call(
        _inproj_kernel,
        grid=(bsz, nt),
        in_specs=[blk(d),
                  pl.BlockSpec((1, HALO, d), lambda b, i: (b, jnp.maximum(i * hb - 1, 0), 0)),
                  pl.BlockSpec((1, HALO, d), lambda b, i: (b, jnp.minimum((i + 1) * hb, last), 0)),
                  full(g),
                  pl.BlockSpec((1, 1, d), lambda b, i: (b, 0, 0)),
                  pl.BlockSpec((1, 1, d), lambda b, i: (b, 0, 0)),
                  full(w_perm), full(conv_w), full(conv_b), full(mu_perm)],
        out_specs=[blk(A_WIDTH), blk(A_WIDTH), blk(A_WIDTH), blk(3 * B_WIDTH), blk(256),
                   blk(C_WIDTH), blk(C_WIDTH), blk(C_WIDTH)],
        out_shape=[f((bsz, t, A_WIDTH)), f((bsz, t, A_WIDTH)), f((bsz, t, A_WIDTH)),
                   f((bsz, t, 3 * B_WIDTH)), f((bsz, t, 256)),
                   f((bsz, t, C_WIDTH), BF16), f((bsz, t, C_WIDTH), BF16), f((bsz, t, C_WIDTH), BF16)],
        compiler_params=_params("arbitrary", "arbitrary"),
        name="inproj",
    )(x, x, x, g, shift, scale, w_perm, conv_w, conv_b, mu_perm)


def _chunk_scan(a, b, rev):
    n = a.shape[0]
    row = lax.broadcasted_iota(jnp.int32, a.shape, 0)
    s = 1
    while s < n:
        if rev:
            keep = row < n - s
            a_s = jnp.where(keep, pltpu.roll(a, n - s, 0), 1.0)
            b_s = jnp.where(keep, pltpu.roll(b, n - s, 0), 0.0)
        else:
            keep = row >= s
            a_s = jnp.where(keep, pltpu.roll(a, s, 0), 1.0)
            b_s = jnp.where(keep, pltpu.roll(b, s, 0), 0.0)
        b = a * b_s + b
        a = a * a_s
        s *= 2
    return a, b


def _lru_kernel(xf_ref, xb_ref, h0_ref, wr_ref, br_ref, wi_ref, bi_ref, lam_ref,
                hf_ref, hb_ref, hend_ref, carry):
    i = pl.program_id(1)
    n = pl.num_programs(1)
    ch = xf_ref.shape[1]

    @pl.when(i == 0)
    def _():
        carry[...] = h0_ref[0]

    for d, (x_ref, o_ref) in enumerate(((xf_ref, hf_ref), (xb_ref, hb_ref))):
        x = x_ref[0]
        gate_r = jax.nn.sigmoid(_dot(x, wr_ref[d], HI) + br_ref[d:d + 1])
        gate_i = jax.nn.sigmoid(_dot(x, wi_ref[d], HI) + bi_ref[d:d + 1])
        log_a = -LRU_C * gate_r * _softplus(-lam_ref[d:d + 1])
        a = jnp.exp(log_a)
        b = jnp.sqrt(1.0 - jnp.exp(2.0 * log_a)) * (gate_i * x)
        a_cum, h = _chunk_scan(a, b, rev=bool(d))
        h = h + a_cum * carry[d:d + 1]
        o_ref[0] = h
        carry[d:d + 1] = h[0:1] if d else h[ch - 1:ch]

    @pl.when(i == n - 1)
    def _():
        hend_ref[0] = carry[...]


def _lru_scan(xcf, xcb, h0, wr_bd, br, wi_bd, bi, lam):
    bsz, t, a = xcf.shape
    ch = min(LRU_CHUNK, t)
    nt = t // ch
    full = lambda z: pl.BlockSpec(z.shape, lambda b, i: (0,) * z.ndim)
    fwd = pl.BlockSpec((1, ch, a), lambda b, i: (b, i, 0))
    bwd = pl.BlockSpec((1, ch, a), lambda b, i: (b, nt - 1 - i, 0))
    st = pl.BlockSpec((1, 2, a), lambda b, i: (b, 0, 0))
    return pl.pallas_call(
        _lru_kernel,
        grid=(bsz, nt),
        in_specs=[fwd, bwd, st, full(wr_bd), full(br), full(wi_bd), full(bi), full(lam)],
        out_specs=[fwd, bwd, st],
        out_shape=[jax.ShapeDtypeStruct((bsz, t, a), F32), jax.ShapeDtypeStruct((bsz, t, a), F32),
                   jax.ShapeDtypeStruct((bsz, 2, a), F32)],
        scratch_shapes=[pltpu.VMEM((2, a), F32)],
        compiler_params=_params("arbitrary", "arbitrary"),
        name="lru_scan",
    )(xcf, xcb, h0, wr_bd, br, wi_bd, bi, lam)


def _rwkv_prep(r, k, v, wlo, alo, w0, w2, a0, a2, k_k, k_a, r_k, rev):
    c, width = r.shape
    ones_bd = _head_ones(width).astype(F32)
    w_raw = -_softplus(-(w0 + _dot(jnp.tanh(wlo), w2, HI))) - 0.5
    logw = -jnp.exp(w_raw)
    a_gate = jax.nn.sigmoid(a0 + _dot(alo, a2, HI))
    kk = k * k_k
    kk = kk / jnp.maximum(jnp.sqrt(_dot(kk * kk, ones_bd, HI)), 1e-12)
    k_eff = k * (1.0 + (a_gate - 1.0) * k_a)
    b_vec = kk * a_gate

    tr = lax.broadcasted_iota(jnp.int32, (c, c), 0)
    tc = lax.broadcasted_iota(jnp.int32, (c, c), 1)
    tri = (tc >= tr) if rev else (tc <= tr)
    cum = _dot(tri.astype(F32), logw, HI)
    total = cum[0:1] if rev else cum[c - 1:c]
    g_inv = jnp.exp(-cum)
    g_end = jnp.exp(total - cum)
    earlier = (tc > tr) if rev else (tc < tr)
    return dict(
        a=(-kk * jnp.exp(cum - logw)).astype(BF16), b=(b_vec * g_inv).astype(BF16),
        k=(k_eff * g_inv).astype(BF16), r=(r * jnp.exp(cum)).astype(BF16), v=v.astype(BF16),
        b_end=(b_vec * g_end).astype(BF16), k_end=(k_eff * g_end).astype(BF16), g_total=jnp.exp(total),
        earlier=earlier, incl=earlier | (tr == tc),
        bonus_v=_dot(r * k_eff * r_k, ones_bd, HI) * v)


def _rwkv_solve(chains, c):
    tr = lax.broadcasted_iota(jnp.int32, (c, c), 0)
    tc = lax.broadcasted_iota(jnp.int32, (c, c), 1)
    eye_c = (tr == tc).astype(F32)
    hi = lax.broadcasted_iota(jnp.int32, (HEAD_DIM, HEAD_DIM), 0)
    hj = lax.broadcasted_iota(jnp.int32, (HEAD_DIM, HEAD_DIM), 1)
    n_ab = [jnp.where(ch['earlier'], _dot_nt(ch['a'], ch['b']), 0.0) for ch in chains]
    g_ak = [jnp.where(ch['earlier'], _dot_nt(ch['a'], ch['k']), 0.0).astype(BF16) for ch in chains]
    g_rb = [jnp.where(ch['incl'], _dot_nt(ch['r'], ch['b']), 0.0).astype(BF16) for ch in chains]
    g_rk = [jnp.where(ch['incl'], _dot_nt(ch['r'], ch['k']), 0.0).astype(BF16) for ch in chains]
    gv = [_dot(g, ch['v']).astype(BF16) for g, ch in zip(g_ak, chains)]

    t_inv = [eye_c + n for n in n_ab]
    pw = [n.astype(BF16) for n in n_ab]
    s = 2
    while s < c:
        pw = [_dot(p, p).astype(BF16) for p in pw]
        t_inv = [t + _dot(t.astype(BF16), p) for t, p in zip(t_inv, pw)]
        s *= 2

    aw = [_dot(t.astype(BF16), jnp.concatenate([ch['a'], g], axis=1)).astype(BF16)
          for t, ch, g in zip(t_inv, chains, gv)]
    ry = [_dot(g, x) for g, x in zip(g_rb, aw)]
    m = [ch['m'].astype(BF16) for ch in chains]
    ys = [_dot((ch['r'].astype(F32) + y[:, :HEAD_DIM]).astype(BF16), mm) + y[:, HEAD_DIM:] + _dot(g, ch['v'])
          for ch, y, mm, g in zip(chains, ry, m, g_rk)]
    pq = [_dot_tn(ch['b_end'], x) for ch, x in zip(chains, aw)]
    kv = [_dot_tn(ch['k_end'], ch['v']) for ch in chains]
    m_new = [_dot((p[:, :HEAD_DIM] + jnp.where(hi == hj, ch['g_total'], 0.0)).astype(BF16), mm) + p[:, HEAD_DIM:] + q
             for ch, p, q, mm in zip(chains, pq, kv, m)]
    return ys, m_new


def _rwkv_kernel(rkvf_ref, lof_ref, rkvb_ref, lob_ref, m0_ref, w0_ref, w2_ref, a0_ref, a2_ref,
                 kk_ref, ka_ref, rk_ref, yf_ref, yb_ref, bvf_ref, bvb_ref, mend_ref, m_f, m_b):
    i = pl.program_id(1)
    n = pl.num_programs(1)

    @pl.when(i == 0)
    def _():
        m_f[...] = m0_ref[0, 0]
        m_b[...] = m0_ref[0, 1]

    w = B_WIDTH
    nh = w // HEAD_DIM
    c = rkvf_ref.shape[1]
    chains = []
    for d, (rkv_ref, lo_ref, bv_ref, m_ref) in enumerate(
            ((rkvf_ref, lof_ref, bvf_ref, m_f), (rkvb_ref, lob_ref, bvb_ref, m_b))):
        rkv = rkv_ref[0]
        lo = lo_ref[0]
        p = _rwkv_prep(rkv[:, :w], rkv[:, w:2 * w], rkv[:, 2 * w:], lo[:, :LORA], lo[:, LORA:2 * LORA],
                       w0_ref[d:d + 1], w2_ref[d], a0_ref[d:d + 1], a2_ref[d],
                       kk_ref[...], ka_ref[...], rk_ref[...], rev=bool(d))
        bv_ref[0] = p['bonus_v']
        for h in range(nh):
            sl = slice(h * HEAD_DIM, (h + 1) * HEAD_DIM)
            ch = {key: p[key][:, sl] for key in ('a', 'b', 'k', 'r', 'v', 'b_end', 'k_end', 'g_total')}
            ch.update(earlier=p['earlier'], incl=p['incl'], m=m_ref[h])
            chains.append(ch)
    ys, m_new = _rwkv_solve(chains, c)
    yf_ref[0] = jnp.concatenate(ys[:nh], axis=1)
    yb_ref[0] = jnp.concatenate(ys[nh:], axis=1)
    for h in range(nh):
        m_f[h] = m_new[h]
        m_b[h] = m_new[nh + h]

    @pl.when(i == n - 1)
    def _():
        mend_ref[0, 0] = m_f[...]
        mend_ref[0, 1] = m_b[...]


def _rwkv_scan(brkv, blo, m0, w0, w2, a0, a2, k_k, k_a, r_k):
    bsz, t, _ = brkv.shape
    c = RWKV_CHUNK
    nt = t // c
    w = B_WIDTH
    full = lambda z: pl.BlockSpec(z.shape, lambda b, i: (0,) * z.ndim)
    fwd = lambda width: pl.BlockSpec((1, c, width), lambda b, i: (b, i, 0))
    bwd = lambda width: pl.BlockSpec((1, c, width), lambda b, i: (b, nt - 1 - i, 0))
    nh = w // HEAD_DIM
    st = pl.BlockSpec((1, 2, nh, HEAD_DIM, HEAD_DIM), lambda b, i: (b, 0, 0, 0, 0))
    o = jax.ShapeDtypeStruct((bsz, t, w), F32)
    return pl.pallas_call(
        _rwkv_kernel,
        grid=(bsz, nt),
        in_specs=[fwd(3 * w), fwd(256), bwd(3 * w), bwd(256), st,
                  full(w0), full(w2), full(a0), full(a2), full(k_k), full(k_a), full(r_k)],
        out_specs=[fwd(w), bwd(w), fwd(w), bwd(w), st],
        out_shape=[o, o, o, o, jax.ShapeDtypeStruct((bsz, 2, nh, HEAD_DIM, HEAD_DIM), F32)],
        scratch_shapes=[pltpu.VMEM((nh, HEAD_DIM, HEAD_DIM), F32), pltpu.VMEM((nh, HEAD_DIM, HEAD_DIM), F32)],
        compiler_params=_params("arbitrary", "arbitrary"),
        name="rwkv_scan",
    )(brkv, blo, brkv, blo, m0, w0, w2, a0, a2, k_k, k_a, r_k)


def _na_kernel(q_ref, k_ref, v_ref, kc_ref, vc_ref, bias_ref, o_ref):
    r = pl.program_id(1)
    rows = pl.num_programs(1)
    kh = bias_ref.shape[3] // GRID_W
    rs = jnp.clip(r - kh // 2, 0, rows - kh)
    start = pl.multiple_of(rs * GRID_W, GRID_W)
    q = q_ref[0]
    kb = k_ref[0, pl.ds(start, kh * GRID_W), :]
    vb = v_ref[0, pl.ds(start, kh * GRID_W), :]
    kc = kc_ref[0]
    vc = vc_ref[0]
    sls = [slice(h * HEAD_DIM, (h + 1) * HEAD_DIM) for h in range(q.shape[1] // HEAD_DIM)]
    s_w = [_dot_nt(q[:, sl], kb[:, sl]) + bias_ref[0, h] for h, sl in enumerate(sls)]
    s_c = [_dot_nt(q[:, sl], kc[:, sl]) for sl in sls]
    m = [jnp.maximum(jnp.max(a, axis=-1, keepdims=True), jnp.max(b, axis=-1, keepdims=True))
         for a, b in zip(s_w, s_c)]
    p_w = [jnp.exp(a - mm) for a, mm in zip(s_w, m)]
    p_c = [jnp.exp(b - mm) for b, mm in zip(s_c, m)]
    den = [jnp.sum(a, axis=-1, keepdims=True) + jnp.sum(b, axis=-1, keepdims=True) for a, b in zip(p_w, p_c)]
    outs = [(_dot(a.astype(BF16), vb[:, sl]) + _dot(b.astype(BF16), vc[:, sl])) / d
            for a, b, d, sl in zip(p_w, p_c, den, sls)]
    o_ref[0] = jnp.concatenate(outs, axis=1)


def _na_bias_table(rpb, rows):
    kh = min(WIN_H, rows)
    nh, nr, ncol = rpb.shape
    qcol = np.arange(GRID_W)[:, None]
    kcol = np.arange(GRID_W)[None, :]
    cstart = np.clip(qcol - WIN_W // 2, 0, GRID_W - WIN_W)
    valid = (kcol >= cstart) & (kcol < cstart + WIN_W)
    hot = ((kcol - qcol + (WIN_W - 1))[None] == np.arange(ncol)[:, None, None]) & valid[None]
    hot = jnp.asarray(hot.reshape(ncol, GRID_W * GRID_W), F32)
    toe = _dot(rpb.reshape(nh * nr, ncol), hot, HI).reshape(nh, nr, GRID_W, GRID_W)
    toe = jnp.where(jnp.asarray(valid)[None, None], toe, MASK_VALUE)
    tab = [jnp.concatenate([toe[:, i - var + (WIN_H - 1)] for i in range(kh)], axis=-1) for var in range(kh)]
    return jnp.stack(tab)


def _na_attention(q, k, v, kc, vc, bias_tab):
    bsz, seq, cw = q.shape
    ctx_len = kc.shape[1]
    rows = seq // GRID_W
    kh = bias_tab.shape[0]
    nh = bias_tab.shape[1]

    def bias_map(b, r):
        return (r - jnp.clip(r - kh // 2, 0, rows - kh), 0, 0, 0)

    return pl.pallas_call(
        _na_kernel,
        grid=(bsz, rows),
        in_specs=[pl.BlockSpec((1, GRID_W, cw), lambda b, r: (b, r, 0)),
                  pl.BlockSpec((1, seq, cw), lambda b, r: (b, 0, 0)),
                  pl.BlockSpec((1, seq, cw), lambda b, r: (b, 0, 0)),
                  pl.BlockSpec((1, ctx_len, cw), lambda b, r: (b, 0, 0)),
                  pl.BlockSpec((1, ctx_len, cw), lambda b, r: (b, 0, 0)),
                  pl.BlockSpec((1, nh, GRID_W, kh * GRID_W), bias_map)],
        out_specs=pl.BlockSpec((1, GRID_W, cw), lambda b, r: (b, r, 0)),
        out_shape=jax.ShapeDtypeStruct((bsz, seq, cw), F32),
        compiler_params=_params("arbitrary", "arbitrary"),
        name="na_attention",
    )(q, k, v, kc, vc, bias_tab)


def _ctx_attn_kernel(q_ref, k_ref, v_ref, o_ref):
    q = q_ref[0]
    k = k_ref[0]
    v = v_ref[0]
    outs = []
    for h in range(q.shape[1] // HEAD_DIM):
        sl = slice(h * HEAD_DIM, (h + 1) * HEAD_DIM)
        s = _dot_nt(q[:, sl], k[:, sl])
        p = jnp.exp(s - jnp.max(s, axis=-1, keepdims=True))
        outs.append(_dot(p.astype(BF16), v[:, sl]) / jnp.sum(p, axis=-1, keepdims=True))
    o_ref[0] = jnp.concatenate(outs, axis=1)


def _ctx_attention(q, k, v):
    bsz, n, cw = q.shape
    spec = pl.BlockSpec((1, n, cw), lambda b: (b, 0, 0))
    return pl.pallas_call(
        _ctx_attn_kernel, grid=(bsz,), in_specs=[spec, spec, spec], out_specs=spec,
        out_shape=jax.ShapeDtypeStruct((bsz, n, cw), F32),
        compiler_params=_params("arbitrary"), name="ctx_attention",
    )(q, k, v)


def _outproj_kernel(x_ref, hf_ref, hb_ref, ga_ref, yf_ref, yb_ref, bvf_ref, bvb_ref, lo_ref, yc_ref,
                    wo_ref, g2_ref, lng_ref, lnb_ref, gate_ref, g_ref, sh_ref, sc_ref, rw_ref, rb_ref,
                    xo_ref, h_ref, lg_ref):
    ya = jax.nn.gelu(ga_ref[0]) * (hf_ref[0] + hb_ref[0])

    y = yf_ref[0] + yb_ref[0]
    avg = _head_ones(B_WIDTH).astype(F32) * (1.0 / HEAD_DIM)
    mean = _dot(y, avg, HI)
    yc_ = y - mean
    var = _dot(yc_ * yc_, avg, HI)
    yn = yc_ * lax.rsqrt(var + GN_EPS) * lng_ref[...] + lnb_ref[...] + bvf_ref[0] + bvb_ref[0]
    gate_b = _dot(jax.nn.sigmoid(lo_ref[0][:, 2 * LORA:3 * LORA]), g2_ref[...], HI)
    yb = yn * gate_b

    mix = (_dot(ya.astype(BF16), wo_ref[:A_WIDTH]) + _dot(yb.astype(BF16), wo_ref[A_WIDTH:A_WIDTH + B_WIDTH])
           + _dot(yc_ref[0].astype(BF16), wo_ref[A_WIDTH + B_WIDTH:]))
    x = x_ref[0] + gate_ref[0] * mix
    xo_ref[0] = x
    hn = x * lax.rsqrt(jnp.mean(x * x, axis=-1, keepdims=True) + RMS_EPS) * g_ref[...]
    hn = hn * (1.0 + sc_ref[0]) + sh_ref[0]
    h_ref[0] = hn
    lg_ref[0] = _dot(hn, rw_ref[...], HI) + rb_ref[...]


def _outproj(x, hf, hb, ga, yf, yb, bvf, bvb, blo, yc, wo, g2, lng, lnb, gate, g, shift, scale, rw, rb, tm):
    bsz, t, d = x.shape
    ne = rw.shape[1]
    blk = lambda w: pl.BlockSpec((1, tm, w), lambda b, i: (b, i, 0))
    full = lambda a: pl.BlockSpec(a.shape, lambda b, i: (0,) * a.ndim)
    vec = pl.BlockSpec((1, 1, d), lambda b, i: (b, 0, 0))
    return pl.pallas_call(
        _outproj_kernel,
        grid=(bsz, t // tm),
        in_specs=[blk(d), blk(A_WIDTH), blk(A_WIDTH), blk(A_WIDTH), blk(B_WIDTH), blk(B_WIDTH),
                  blk(B_WIDTH), blk(B_WIDTH), blk(256), blk(C_WIDTH),
                  full(wo), full(g2), full(lng), full(lnb), vec, full(g), vec, vec, full(rw), full(rb)],
        out_specs=[blk(d), blk(d), blk(ne)],
        out_shape=[jax.ShapeDtypeStruct((bsz, t, d), F32), jax.ShapeDtypeStruct((bsz, t, d), F32),
                   jax.ShapeDtypeStruct((bsz, t, ne), F32)],
        compiler_params=_params("arbitrary", "arbitrary"),
        name="outproj",
    )(x, hf, hb, ga, yf, yb, bvf, bvb, blo, yc, wo, g2, lng, lnb, gate, g, shift, scale, rw, rb)


def _expert_kernel(be_ref, nb_ref, x_ref, wgu_ref, bgu_ref, wdn_ref, bdn_ref, o_ref, wgu_s, wdn_s):
    i = pl.program_id(0)
    e = be_ref[i]
    fresh = jnp.logical_or(i == 0, be_ref[jnp.maximum(i - 1, 0)] != e)

    @pl.when(fresh)
    def _():
        wgu_s[...] = wgu_ref[0].astype(BF16)
        wdn_s[...] = wdn_ref[0].astype(BF16)

    @pl.when(i < nb_ref[0])
    def _():
        dff = wdn_s.shape[0]
        gu = _dot(x_ref[...].astype(BF16), wgu_s[...]) + bgu_ref[0]
        g_ = jnp.minimum(gu[:, :dff], SWIGLU_LIMIT)
        u_ = jnp.clip(gu[:, dff:], -SWIGLU_LIMIT, SWIGLU_LIMIT)
        act = (u_ + 1.0) * (g_ * jax.nn.sigmoid(SWIGLU_ALPHA * g_))
        o_ref[...] = _dot(act.astype(BF16), wdn_s[...]) + bdn_ref[0]

    @pl.when(i >= nb_ref[0])
    def _():
        o_ref[...] = jnp.zeros_like(o_ref)


def _experts(block_e, n_used, xb, w_gu, b_gu, w_dn, b_dn):
    cap, d = xb.shape
    ne, _, f2 = w_gu.shape
    dff = w_dn.shape[1]
    nb = cap // MOE_BM
    grid_spec = pltpu.PrefetchScalarGridSpec(
        num_scalar_prefetch=2,
        grid=(nb,),
        in_specs=[pl.BlockSpec((MOE_BM, d), lambda i, be, nu: (i, 0)),
                  pl.BlockSpec((1, d, f2), lambda i, be, nu: (be[i], 0, 0)),
                  pl.BlockSpec((1, 1, f2), lambda i, be, nu: (be[i], 0, 0)),
                  pl.BlockSpec((1, dff, d), lambda i, be, nu: (be[i], 0, 0)),
                  pl.BlockSpec((1, 1, d), lambda i, be, nu: (be[i], 0, 0))],
        out_specs=pl.BlockSpec((MOE_BM, d), lambda i, be, nu: (i, 0)),
        scratch_shapes=[pltpu.VMEM((d, f2), BF16), pltpu.VMEM((dff, d), BF16)],
    )
    return pl.pallas_call(
        _expert_kernel, grid_spec=grid_spec,
        out_shape=jax.ShapeDtypeStruct((cap, d), F32),
        compiler_params=_params("arbitrary"), name="experts",
    )(block_e, n_used, xb, w_gu, b_gu.reshape(ne, 1, f2), w_dn, b_dn.reshape(ne, 1, d))


def _route_kernel(lg_ref, idx_ref, gate_ref, rank_ref, cnt_ref, carry):
    i = pl.program_id(0)

    @pl.when(i == 0)
    def _():
        carry[...] = jnp.zeros_like(carry)

    l = lg_ref[...]
    tm, ne = l.shape
    lane = lax.broadcasted_iota(jnp.int32, l.shape, 1)
    vals, idxs, hots = [], [], []
    for _ in range(TOP_K):
        m = jnp.max(l, axis=-1, keepdims=True)
        ix = jnp.min(jnp.where(l == m, lane, ne), axis=-1, keepdims=True)
        hot = lane == ix
        vals.append(m)
        idxs.append(ix)
        hots.append(hot)
        l = jnp.where(hot, -jnp.inf, l)
    ex = [jnp.exp(v - vals[0]) for v in vals]
    den = ex[0] + ex[1] + ex[2] + ex[3]
    hot_all = (hots[0] | hots[1] | hots[2] | hots[3]).astype(BF16)
    tr = lax.broadcasted_iota(jnp.int32, (tm, tm), 0)
    tc = lax.broadcasted_iota(jnp.int32, (tm, tm), 1)
    before = _dot((tc < tr).astype(BF16), hot_all) + carry[...]
    ranks = [jnp.sum(jnp.where(h, before, 0.0), axis=-1, keepdims=True) for h in hots]
    carry[...] = carry[...] + jnp.sum(hot_all.astype(F32), axis=0, keepdims=True)
    idx_ref[...] = jnp.concatenate(idxs, axis=1)
    gate_ref[...] = jnp.concatenate([e / den for e in ex], axis=1)
    rank_ref[...] = jnp.concatenate(ranks, axis=1).astype(jnp.int32)
    cnt_ref[...] = carry[...].astype(jnp.int32)


def _route(logits):
    n, ne = logits.shape
    tm = ROUTE_TM
    assert n % tm == 0, (n, tm)
    blk = pl.BlockSpec((tm, TOP_K), lambda i: (i, 0))
    return pl.pallas_call(
        _route_kernel,
        grid=(n // tm,),
        in_specs=[pl.BlockSpec((tm, ne), lambda i: (i, 0))],
        out_specs=[blk, blk, blk, pl.BlockSpec((1, ne), lambda i: (0, 0))],
        out_shape=[jax.ShapeDtypeStruct((n, TOP_K), jnp.int32), jax.ShapeDtypeStruct((n, TOP_K), F32),
                   jax.ShapeDtypeStruct((n, TOP_K), jnp.int32), jax.ShapeDtypeStruct((1, ne), jnp.int32)],
        scratch_shapes=[pltpu.VMEM((1, ne), F32)],
        compiler_params=_params("arbitrary"),
        name="route",
    )(logits)


def _dispatch_kernel(dest_hbm, h_ref, xb_init, xb_hbm, dest_s, sem_idx, sem):
    del xb_init
    i = pl.program_id(0)
    tm = h_ref.shape[0]
    cp = pltpu.make_async_copy(dest_hbm.at[pl.ds(i * tm * TOP_K, tm * TOP_K)], dest_s, sem_idx)
    cp.start()
    cp.wait()

    def issue(t, carry):
        src = h_ref.at[pl.ds(t, 1)]
        for j in range(TOP_K):
            pltpu.make_async_copy(src, xb_hbm.at[pl.ds(dest_s[t * TOP_K + j], 1)], sem).start()
        return carry

    lax.fori_loop(0, tm, issue, 0, unroll=8)
    for _ in range(TOP_K):
        pltpu.make_async_copy(h_ref, xb_hbm.at[pl.ds(0, tm)], sem).wait()


def _dispatch(dest_flat, h, cap):
    n, d = h.shape
    tm = DISPATCH_TM
    assert n % tm == 0, (n, tm)
    any_spec = pl.BlockSpec(memory_space=pl.ANY)
    return pl.pallas_call(
        _dispatch_kernel,
        grid=(n // tm,),
        in_specs=[any_spec, pl.BlockSpec((tm, d), lambda i: (i, 0)), any_spec],
        out_specs=any_spec,
        out_shape=jax.ShapeDtypeStruct((cap, d), F32),
        scratch_shapes=[pltpu.SMEM((tm * TOP_K,), jnp.int32), pltpu.SemaphoreType.DMA, pltpu.SemaphoreType.DMA],
        input_output_aliases={2: 0},
        compiler_params=pltpu.CompilerParams(dimension_semantics=("arbitrary",), has_side_effects=True),
        name="dispatch",
    )(dest_flat, h, jnp.zeros((cap, d), F32))


def _combine_kernel(dest_hbm, yb_hbm, x_ref, gt_ref, mg_ref, g_ref, o_ref, dest_s, buf, sem_idx, sem,
                    *, tok0, final):
    i = pl.program_id(0)
    n = pl.num_programs(0)
    tm = x_ref.shape[0]

    def fetch(blk, slot):
        cp = pltpu.make_async_copy(dest_hbm.at[pl.ds((tok0 + blk * tm) * TOP_K, tm * TOP_K)], dest_s.at[slot],
                                   sem_idx)
        cp.start()
        cp.wait()

        def issue(t, carry):
            for j in range(TOP_K):
                pltpu.make_async_copy(yb_hbm.at[pl.ds(dest_s[slot, t * TOP_K + j], 1)],
                                      buf.at[slot, j, pl.ds(t, 1)], sem.at[slot]).start()
            return carry

        lax.fori_loop(0, tm, issue, 0, unroll=8)

    @pl.when(i == 0)
    def _():
        fetch(0, 0)

    @pl.when(i + 1 < n)
    def _():
        fetch(i + 1, (i + 1) % 2)

    slot = i % 2
    for j in range(TOP_K):
        pltpu.make_async_copy(yb_hbm.at[pl.ds(0, tm)], buf.at[slot, j], sem.at[slot]).wait()
    gt = gt_ref[...]
    y = gt[:, 0:1] * buf[slot, 0]
    for j in range(1, TOP_K):
        y = y + gt[:, j:j + 1] * buf[slot, j]
    x = x_ref[...] + mg_ref[0] * y
    if final:
        x = x * lax.rsqrt(jnp.mean(x * x, axis=-1, keepdims=True) + RMS_EPS) * g_ref[...]
    o_ref[...] = x


def _combine(dest_flat, yb, x, gates, mod_gate, g, tok0, final):
    bsz, t, d = x.shape
    n = bsz * t
    tm = min(COMBINE_TM, t)
    per_b = t // tm
    any_spec = pl.BlockSpec(memory_space=pl.ANY)
    out = pl.pallas_call(
        functools.partial(_combine_kernel, tok0=tok0, final=final),
        grid=(n // tm,),
        in_specs=[any_spec, any_spec,
                  pl.BlockSpec((tm, d), lambda i: (i, 0)),
                  pl.BlockSpec((tm, TOP_K), lambda i: (i, 0)),
                  pl.BlockSpec((1, 1, d), lambda i: (i // per_b, 0, 0)),
                  pl.BlockSpec((1, d), lambda i: (0, 0))],
        out_specs=pl.BlockSpec((tm, d), lambda i: (i, 0)),
        out_shape=jax.ShapeDtypeStruct((n, d), F32),
        scratch_shapes=[pltpu.SMEM((2, tm * TOP_K), jnp.int32), pltpu.VMEM((2, TOP_K, tm, d), F32),
                        pltpu.SemaphoreType.DMA, pltpu.SemaphoreType.DMA((2,))],
        compiler_params=_params("arbitrary"),
        name="combine",
    )(dest_flat, yb, x.reshape(n, d), gates, mod_gate, g)
    return out.reshape(bsz, t, d)


def _moe_plan(logits):
    n, ne = logits.shape
    idx, gates, rank, counts = _route(logits)
    counts = counts[0]
    padded = (counts + MOE_BM - 1) // MOE_BM * MOE_BM
    pad_end = jnp.cumsum(padded)
    pad_start = pad_end - padded
    hot = idx[:, :, None] == jnp.arange(ne, dtype=jnp.int32)[None, None, :]
    dest = jnp.sum(jnp.where(hot, pad_start[None, None, :], 0), axis=-1) + rank
    nb = (n * TOP_K + ne * (MOE_BM - 1) + MOE_BM - 1) // MOE_BM
    block_e = jnp.minimum(jnp.searchsorted(pad_end, jnp.arange(nb, dtype=jnp.int32) * MOE_BM, side='right'),
                          ne - 1).astype(jnp.int32)
    n_used = (pad_end[-1] // MOE_BM).astype(jnp.int32).reshape(1)
    return dest.reshape(-1).astype(jnp.int32), gates, block_e, n_used, nb * MOE_BM


def _block_diag(w):
    nh, n, _ = w.shape
    out = jnp.zeros((nh * n, nh * n), w.dtype)
    for h in range(nh):
        out = out.at[h * n:(h + 1) * n, h * n:(h + 1) * n].set(w[h])
    return out


def _permute_in_cols(w_in):
    a0 = 0
    b0 = 2 * A_WIDTH
    c0 = b0 + 3 * B_WIDTH + 3 * LORA
    d = w_in.shape[0]
    pad = jnp.zeros((d, 256 - 3 * LORA), w_in.dtype)
    return jnp.concatenate([w_in[:, a0:a0 + A_WIDTH], w_in[:, b0:b0 + 3 * B_WIDTH],
                            w_in[:, b0 + 3 * B_WIDTH:c0], pad,
                            w_in[:, A_WIDTH:2 * A_WIDTH], w_in[:, c0:]], axis=1)


def _mixers(p_ctx, p_lat, lp, need_ctx):
    (xcf_c, xcb_c, ga_c, brkv_c, blo_c, q_c, k_c, v_c) = p_ctx
    (xcf_l, xcb_l, ga_l, brkv_l, blo_l, q_l, k_l, v_l) = p_lat
    bsz = xcf_c.shape[0]

    lru_w = (lp['wr_bd'], lp['br'], lp['wi_bd'], lp['bi'], lp['lam'])
    hf_c, hb_c, hend = _lru_scan(xcf_c, xcb_c, jnp.zeros((bsz, 2, A_WIDTH), F32), *lru_w)
    hf_l, hb_l, _ = _lru_scan(xcf_l, xcb_l, hend, *lru_w)

    rw_w = (lp['w0'], lp['w2'], lp['a0'], lp['a2'], lp['k_k'], lp['k_a'], lp['r_k'])
    yf_c, yb_c, bvf_c, bvb_c, mend = _rwkv_scan(brkv_c, blo_c, jnp.zeros((bsz, 2, B_WIDTH // HEAD_DIM, HEAD_DIM, HEAD_DIM), F32), *rw_w)
    yf_l, yb_l, bvf_l, bvb_l, _ = _rwkv_scan(brkv_l, blo_l, mend, *rw_w)

    yc_l = _na_attention(q_l, k_l, v_l, k_c, v_c, lp['bias_tab'])
    lat = (hf_l, hb_l, ga_l, yf_l, yb_l, bvf_l, bvb_l, blo_l, yc_l)
    ctx = None
    if need_ctx:
        yc_c = _ctx_attention(q_c, k_c, v_c)
        ctx = (hf_c, hb_c, ga_c, yf_c, yb_c, bvf_c, bvb_c, blo_c, yc_c)
    return ctx, lat


def kernel(x, c, ctx, c_ctx, ada_w, ada_b, norm_mix_g, norm_ffn_g, w_in, w_out, lru_conv_w, lru_conv_b, lru_wr, lru_br, lru_wi, lru_bi, lru_lambda, rwkv_mu, rwkv_w0, rwkv_w2, rwkv_a0, rwkv_a2, rwkv_g2, rwkv_kk, rwkv_ka, rwkv_rk, rwkv_lnx_g, rwkv_lnx_b, na_rpb, router_w, router_b, moe_w_gu, moe_b_gu, moe_w_dn, moe_b_dn, final_g):
    bsz, seq, d = x.shape
    ctx_len = ctx.shape[1]
    depth = ada_w.shape[0]
    rows = seq // GRID_W
    tm_l = 256
    tm_c = min(256, ctx_len)

    cond = jnp.concatenate([c, c_ctx[None], jnp.zeros((8 - bsz - 1, d), F32)], axis=0)
    mod = _adaln(cond, ada_w, ada_b)

    xl, xc = x, ctx
    for l in range(depth):
        last = l == depth - 1
        ml = [mod[l, :bsz, j * d:(j + 1) * d][:, None, :] for j in range(6)]
        mc = [jnp.broadcast_to(mod[l, bsz:bsz + 1, j * d:(j + 1) * d][:, None, :], (bsz, 1, d)) for j in range(6)]
        n_b = 3 * B_WIDTH
        mu = rwkv_mu[l]
        lp = dict(
            wr_bd=jnp.stack([_block_diag(lru_wr[l, dd]) for dd in range(2)]),
            wi_bd=jnp.stack([_block_diag(lru_wi[l, dd]) for dd in range(2)]),
            br=lru_br[l], bi=lru_bi[l], lam=lru_lambda[l],
            w0=rwkv_w0[l], w2=rwkv_w2[l], a0=rwkv_a0[l], a2=rwkv_a2[l],
            k_k=rwkv_kk[l][None], k_a=rwkv_ka[l][None], r_k=rwkv_rk[l].reshape(1, B_WIDTH),
            bias_tab=_na_bias_table(na_rpb[l], rows),
        )
        w_perm = _permute_in_cols(w_in[l]).astype(BF16)
        mu_perm = jnp.concatenate([mu[:, :n_b], mu[:, n_b:], jnp.zeros((2, 256 - 3 * LORA), F32)], axis=1)
        g_mix = norm_mix_g[l][None]
        g_ffn = norm_ffn_g[l][None]
        wo = w_out[l].astype(BF16)

        p_lat = _inproj(xl, g_mix, ml[0], ml[1], w_perm, lru_conv_w[l], lru_conv_b[l], mu_perm, tm_l)
        p_ctx = _inproj(xc, g_mix, mc[0], mc[1], w_perm, lru_conv_w[l], lru_conv_b[l], mu_perm, tm_c)
        mix_c, mix_l = _mixers(p_ctx, p_lat, lp, not last)

        fin = (wo, rwkv_g2[l], rwkv_lnx_g[l][None], rwkv_lnx_b[l][None])
        rt = (router_w[l], router_b[l][None])
        xl, hl, lg_l = _outproj(xl, *mix_l, *fin, ml[2], g_ffn, ml[3], ml[4], *rt, tm_l)
        ne = router_w.shape[2]
        if last:
            tok = hl.reshape(-1, d)
            lg = lg_l.reshape(-1, ne)
            nc = 0
        else:
            xc, hc, lg_c = _outproj(xc, *mix_c, *fin, mc[2], g_ffn, mc[3], mc[4], *rt, tm_c)
            tok = jnp.concatenate([hc.reshape(-1, d), hl.reshape(-1, d)], axis=0)
            lg = jnp.concatenate([lg_c.reshape(-1, ne), lg_l.reshape(-1, ne)], axis=0)
            nc = bsz * ctx_len
        dest, gates, block_e, n_used, cap = _moe_plan(lg)
        xb = _dispatch(dest, tok, cap)
        yb = _experts(block_e, n_used, xb, moe_w_gu[l], moe_b_gu[l], moe_w_dn[l], moe_b_dn[l])
        if not last:
            xc = _combine(dest, yb, xc, gates[:nc], mc[5], final_g[None], 0, False)
        xl = _combine(dest, yb, xl, gates[nc:], ml[5], final_g[None], nc, last)
    return xl
```

```python
import functools

import numpy as np
import jax
import jax.numpy as jnp
from jax import lax
from jax.experimental import pallas as pl
from jax.experimental.pallas import tpu as pltpu

F32 = jnp.float32
BF16 = jnp.bfloat16
HI = lax.Precision.HIGHEST

RMS_EPS = 1e-6
GN_EPS = 64e-5
LRU_C = 8.0
SWIGLU_ALPHA = 1.702
SWIGLU_LIMIT = 7.0
TOP_K = 4
GRID_W = 64
WIN_H = 8
WIN_W = 16
HEAD_DIM = 64
A_WIDTH = 256
B_WIDTH = 256
C_WIDTH = 512
LORA = 64
LANES = 128
HALO = 8
RWKV_CHUNK = 64
LRU_CHUNK = 256
MOE_BM = 256
ROUTE_TM = 512
DISPATCH_TM = 256
COMBINE_TM = 128
MASK_VALUE = -1e30
VMEM_LIMIT = 56 * 1024 * 1024


def _dot(a, b, prec=None):
    return jnp.dot(a, b, preferred_element_type=F32, precision=prec)


def _dot_nt(a, b, prec=None):
    return lax.dot_general(a, b, (((1,), (1,)), ((), ())), preferred_element_type=F32, precision=prec)


def _dot_tn(a, b, prec=None):
    return lax.dot_general(a, b, (((0,), (0,)), ((), ())), preferred_element_type=F32, precision=prec)


def _split(x, n):
    terms = []
    for _ in range(n):
        t = x.astype(BF16)
        terms.append(t)
        x = x - t.astype(F32)
    return terms


def _dot_split(a, b, order):
    a_t = [a] if a.dtype == BF16 else _split(a, order)
    b_t = [b] if b.dtype == BF16 else _split(b, order)
    out = None
    for i, x in enumerate(a_t):
        for j, y in enumerate(b_t):
            if i + j < order:
                p = _dot(x, y)
                out = p if out is None else out + p
    return out


def _softplus(z):
    return jnp.maximum(z, 0.0) + jnp.log1p(jnp.exp(-jnp.abs(z)))


def _params(*sem):
    return pltpu.CompilerParams(dimension_semantics=sem, vmem_limit_bytes=VMEM_LIMIT)


def _row_tile(d):
    assert d % LANES == 0, d
    return (d // LANES, LANES)


def _head_ones(width):
    r = lax.broadcasted_iota(jnp.int32, (width, width), 0) // HEAD_DIM
    c = lax.broadcasted_iota(jnp.int32, (width, width), 1) // HEAD_DIM
    return r == c


def _adaln_kernel(cond_ref, w_ref, b_ref, o_ref):
    c = cond_ref[...]
    o_ref[0] = _dot(c * jax.nn.sigmoid(c), w_ref[0], HI) + b_ref[0]


def _adaln(cond, ada_w, ada_b):
    depth, d, n = ada_w.shape
    tn = 1536
    return pl.pallas_call(
        _adaln_kernel,
        grid=(depth, n // tn),
        in_specs=[pl.BlockSpec((8, d), lambda l, j: (0, 0)),
                  pl.BlockSpec((1, d, tn), lambda l, j: (l, 0, j)),
                  pl.BlockSpec((1, 1, tn), lambda l, j: (l, 0, j))],
        out_specs=pl.BlockSpec((1, 8, tn), lambda l, j: (l, 0, j)),
        out_shape=jax.ShapeDtypeStruct((depth, 8, n), F32),
        compiler_params=_params("arbitrary", "arbitrary"),
        name="adaln",
    )(cond, ada_w, ada_b.reshape(depth, 1, n))


N_SHIFT = 1280
N_PROJ = 3072


def _inproj_kernel(x_ref, xp_ref, xn_ref, g_ref, sh_ref, sc_ref, w_ref, cw_ref, cb_ref, mu_ref,
                   xcf_ref, xcb_ref, ga_ref, brkv_ref, blo_ref, q_ref, k_ref, v_ref):
    i = pl.program_id(1)
    n = pl.num_programs(1)
    tm = x_ref.shape[1]
    g = g_ref[...]
    sh = sh_ref[0]
    sc = sc_ref[0]

    def norm_mod(x):
        y = x * lax.rsqrt(jnp.mean(x * x, axis=-1, keepdims=True) + RMS_EPS) * g
        return y * (1.0 + sc) + sh

    p = _dot(norm_mod(x_ref[0]).astype(BF16), w_ref[...])
    halo = jnp.concatenate([xp_ref[0], xn_ref[0]], axis=0)
    ph = _dot(norm_mod(halo).astype(BF16), w_ref[:, :N_SHIFT])
    p_prev = jnp.where(i > 0, ph[:HALO], 0.0)
    p_next = jnp.where(i < n - 1, ph[HALO:], 0.0)
    ext = jnp.concatenate([p_prev, p[:, :N_SHIFT], p_next], axis=0)

    xa = ext[:, :A_WIDTH]
    cw = cw_ref[...]
    cb = cb_ref[...]
    xcf = cb[0:1]
    xcb = cb[1:2]
    for j in range(cw.shape[1]):
        xcf = xcf + cw[0, j:j + 1] * xa[HALO - j:HALO - j + tm]
        xcb = xcb + cw[1, j:j + 1] * xa[HALO + j:HALO + j + tm]
    xcf_ref[0] = xcf
    xcb_ref[0] = xcb

    pb = ext[HALO:HALO + tm, A_WIDTH:]
    prev = ext[HALO - 1:HALO - 1 + tm, A_WIDTH:]
    nxt = ext[HALO + 1:HALO + 1 + tm, A_WIDTH:]
    mu = mu_ref[...]
    sb = pb + mu[0:1] * (prev - pb) + mu[1:2] * (nxt - pb)
    brkv_ref[0] = sb[:, :3 * B_WIDTH]
    blo_ref[0] = sb[:, 3 * B_WIDTH:]

    o = N_SHIFT
    ga_ref[0] = p[:, o:o + A_WIDTH]
    o += A_WIDTH
    q_ref[0] = (p[:, o:o + C_WIDTH] * (HEAD_DIM ** -0.5)).astype(BF16)
    k_ref[0] = p[:, o + C_WIDTH:o + 2 * C_WIDTH].astype(BF16)
    v_ref[0] = p[:, o + 2 * C_WIDTH:o + 3 * C_WIDTH].astype(BF16)


def _inproj(x, g, shift, scale, w_perm, conv_w, conv_b, mu_perm, tm):
    bsz, t, d = x.shape
    nt = t // tm
    hb = tm // HALO
    last = t // HALO - 1
    f = lambda shape, dt=F32: jax.ShapeDtypeStruct(shape, dt)
    blk = lambda w: pl.BlockSpec((1, tm, w), lambda b, i: (b, i, 0))
    full = lambda a: pl.BlockSpec(a.shape, lambda b, i: (0,) * a.ndim)
    return pl.pallas_call(
        _inproj_kernel,
        grid=(bsz, nt),
        in_specs=[blk(d),
                  pl.BlockSpec((1, HALO, d), lambda b, i: (b, jnp.maximum(i * hb - 1, 0), 0)),
                  pl.BlockSpec((1, HALO, d), lambda b, i: (b, jnp.minimum((i + 1) * hb, last), 0)),
                  full(g),
                  pl.BlockSpec((1, 1, d), lambda b, i: (b, 0, 0)),
                  pl.BlockSpec((1, 1, d), lambda b, i: (b, 0, 0)),
                  full(w_perm), full(conv_w), full(conv_b), full(mu_perm)],
        out_specs=[blk(A_WIDTH), blk(A_WIDTH), blk(A_WIDTH), blk(3 * B_WIDTH), blk(256),
                   blk(C_WIDTH), blk(C_WIDTH), blk(C_WIDTH)],
        out_shape=[f((bsz, t, A_WIDTH)), f((bsz, t, A_WIDTH)), f((bsz, t, A_WIDTH)),
                   f((bsz, t, 3 * B_WIDTH)), f((bsz, t, 256)),
                   f((bsz, t, C_WIDTH), BF16), f((bsz, t, C_WIDTH), BF16), f((bsz, t, C_WIDTH), BF16)],
        compiler_params=_params("arbitrary", "arbitrary"),
        name="inproj",
    )(x, x, x, g, shift, scale, w_perm, conv_w, conv_b, mu_perm)


def _chunk_scan(a, b, rev):
    n = a.shape[0]
    row = lax.broadcasted_iota(jnp.int32, a.shape, 0)
    s = 1
    while s < n:
        if rev:
            keep = row < n - s
            a_s = jnp.where(keep, pltpu.roll(a, n - s, 0), 1.0)
            b_s = jnp.where(keep, pltpu.roll(b, n - s, 0), 0.0)
        else:
            keep = row >= s
            a_s = jnp.where(keep, pltpu.roll(a, s, 0), 1.0)
            b_s = jnp.where(keep, pltpu.roll(b, s, 0), 0.0)
        b = a * b_s + b
        a = a * a_s
        s *= 2
    return a, b


def _lru_kernel(xf_ref, xb_ref, h0_ref, wr_ref, br_ref, wi_ref, bi_ref, lam_ref,
                hf_ref, hb_ref, hend_ref, carry):
    i = pl.program_id(1)
    n = pl.num_programs(1)
    ch = xf_ref.shape[1]

    @pl.when(i == 0)
    def _():
        carry[...] = h0_ref[0]

    for d, (x_ref, o_ref) in enumerate(((xf_ref, hf_ref), (xb_ref, hb_ref))):
        x = x_ref[0]
        gate_r = jax.nn.sigmoid(_dot(x, wr_ref[d], HI) + br_ref[d:d + 1])
        gate_i = jax.nn.sigmoid(_dot(x, wi_ref[d], HI) + bi_ref[d:d + 1])
        log_a = -LRU_C * gate_r * _softplus(-lam_ref[d:d + 1])
        a = jnp.exp(log_a)
        b = jnp.sqrt(1.0 - jnp.exp(2.0 * log_a)) * (gate_i * x)
        a_cum, h = _chunk_scan(a, b, rev=bool(d))
        h = h + a_cum * carry[d:d + 1]
        o_ref[0] = h
        carry[d:d + 1] = h[0:1] if d else h[ch - 1:ch]

    @pl.when(i == n - 1)
    def _():
        hend_ref[0] = carry[...]


def _lru_scan(xcf, xcb, h0, wr_bd, br, wi_bd, bi, lam):
    bsz, t, a = xcf.shape
    ch = min(LRU_CHUNK, t)
    nt = t // ch
    full = lambda z: pl.BlockSpec(z.shape, lambda b, i: (0,) * z.ndim)
    fwd = pl.BlockSpec((1, ch, a), lambda b, i: (b, i, 0))
    bwd = pl.BlockSpec((1, ch, a), lambda b, i: (b, nt - 1 - i, 0))
    st = pl.BlockSpec((1, 2, a), lambda b, i: (b, 0, 0))
    return pl.pallas_call(
        _lru_kernel,
        grid=(bsz, nt),
        in_specs=[fwd, bwd, st, full(wr_bd), full(br), full(wi_bd), full(bi), full(lam)],
        out_specs=[fwd, bwd, st],
        out_shape=[jax.ShapeDtypeStruct((bsz, t, a), F32), jax.ShapeDtypeStruct((bsz, t, a), F32),
                   jax.ShapeDtypeStruct((bsz, 2, a), F32)],
        scratch_shapes=[pltpu.VMEM((2, a), F32)],
        compiler_params=_params("arbitrary", "arbitrary"),
        name="lru_scan",
    )(xcf, xcb, h0, wr_bd, br, wi_bd, bi, lam)


def _rwkv_prep(r, k, v, wlo, alo, w0, w2, a0, a2, k_k, k_a, r_k, rev):
    c, width = r.shape
    ones_bd = _head_ones(width).astype(BF16)
    w_raw = -_softplus(-(w0 + _dot_split(jnp.tanh(wlo), w2, 2))) - 0.5
    logw = -jnp.exp(w_raw)
    a_gate = jax.nn.sigmoid(a0 + _dot_split(alo, a2, 2))
    kk = k * k_k
    kk = kk / jnp.maximum(jnp.sqrt(_dot_split(kk * kk, ones_bd, 2)), 1e-12)
    k_eff = k * (1.0 + (a_gate - 1.0) * k_a)
    b_vec = kk * a_gate

    tr = lax.broadcasted_iota(jnp.int32, (c, c), 0)
    tc = lax.broadcasted_iota(jnp.int32, (c, c), 1)
    tri = (tc >= tr) if rev else (tc <= tr)
    cum = _dot_split(tri.astype(BF16), logw, 3)
    total = cum[0:1] if rev else cum[c - 1:c]
    g_inv = jnp.exp(-cum)
    g_end = jnp.exp(total - cum)
    earlier = (tc > tr) if rev else (tc < tr)
    return dict(
        a=(-kk * jnp.exp(cum - logw)).astype(BF16), b=(b_vec * g_inv).astype(BF16),
        k=(k_eff * g_inv).astype(BF16), r=(r * jnp.exp(cum)).astype(BF16), v=v.astype(BF16),
        b_end=(b_vec * g_end).astype(BF16), k_end=(k_eff * g_end).astype(BF16), g_total=jnp.exp(total),
        earlier=earlier, incl=earlier | (tr == tc),
        bonus_v=_dot_split(r * k_eff * r_k, ones_bd, 2) * v)


def _rwkv_solve(chains, c):
    tr = lax.broadcasted_iota(jnp.int32, (c, c), 0)
    tc = lax.broadcasted_iota(jnp.int32, (c, c), 1)
    eye_c = (tr == tc).astype(F32)
    hi = lax.broadcasted_iota(jnp.int32, (HEAD_DIM, HEAD_DIM), 0)
    hj = lax.broadcasted_iota(jnp.int32, (HEAD_DIM, HEAD_DIM), 1)
    n_ab = [jnp.where(ch['earlier'], _dot_nt(ch['a'], ch['b']), 0.0) for ch in chains]
    g_ak = [jnp.where(ch['earlier'], _dot_nt(ch['a'], ch['k']), 0.0).astype(BF16) for ch in chains]
    g_rb = [jnp.where(ch['incl'], _dot_nt(ch['r'], ch['b']), 0.0).astype(BF16) for ch in chains]
    g_rk = [jnp.where(ch['incl'], _dot_nt(ch['r'], ch['k']), 0.0).astype(BF16) for ch in chains]
    gv = [_dot(g, ch['v']).astype(BF16) for g, ch in zip(g_ak, chains)]

    t_inv = [eye_c + n for n in n_ab]
    pw = [n.astype(BF16) for n in n_ab]
    s = 2
    while s < c:
        pw = [_dot(p, p).astype(BF16) for p in pw]
        t_inv = [t + _dot(t.astype(BF16), p) for t, p in zip(t_inv, pw)]
        s *= 2

    aw = [_dot(t.astype(BF16), jnp.concatenate([ch['a'], g], axis=1)).astype(BF16)
          for t, ch, g in zip(t_inv, chains, gv)]
    ry = [_dot(g, x) for g, x in zip(g_rb, aw)]
    m = [ch['m'].astype(BF16) for ch in chains]
    ys = [_dot((ch['r'].astype(F32) + y[:, :HEAD_DIM]).astype(BF16), mm) + y[:, HEAD_DIM:] + _dot(g, ch['v'])
          for ch, y, mm, g in zip(chains, ry, m, g_rk)]
    pq = [_dot_tn(ch['b_end'], x) for ch, x in zip(chains, aw)]
    kv = [_dot_tn(ch['k_end'], ch['v']) for ch in chains]
    m_new = [_dot((p[:, :HEAD_DIM] + jnp.where(hi == hj, ch['g_total'], 0.0)).astype(BF16), mm) + p[:, HEAD_DIM:] + q
             for ch, p, q, mm in zip(chains, pq, kv, m)]
    return ys, m_new


def _rwkv_kernel(rkvf_ref, lof_ref, rkvb_ref, lob_ref, m0_ref, w0_ref, w2_ref, a0_ref, a2_ref,
                 kk_ref, ka_ref, rk_ref, yf_ref, yb_ref, bvf_ref, bvb_ref, mend_ref, m_f, m_b):
    i = pl.program_id(1)
    n = pl.num_programs(1)

    @pl.when(i == 0)
    def _():
        m_f[...] = m0_ref[0, 0]
        m_b[...] = m0_ref[0, 1]

    w = B_WIDTH
    nh = w // HEAD_DIM
    c = rkvf_ref.shape[1]
    chains = []
    for d, (rkv_ref, lo_ref, bv_ref, m_ref) in enumerate(
            ((rkvf_ref, lof_ref, bvf_ref, m_f), (rkvb_ref, lob_ref, bvb_ref, m_b))):
        rkv = rkv_ref[0]
        lo = lo_ref[0]
        p = _rwkv_prep(rkv[:, :w], rkv[:, w:2 * w], rkv[:, 2 * w:], lo[:, :LORA], lo[:, LORA:2 * LORA],
                       w0_ref[d:d + 1], w2_ref[d], a0_ref[d:d + 1], a2_ref[d],
                       kk_ref[...], ka_ref[...], rk_ref[...], rev=bool(d))
        bv_ref[0] = p['bonus_v']
        for h in range(nh):
            sl = slice(h * HEAD_DIM, (h + 1) * HEAD_DIM)
            ch = {key: p[key][:, sl] for key in ('a', 'b', 'k', 'r', 'v', 'b_end', 'k_end', 'g_total')}
            ch.update(earlier=p['earlier'], incl=p['incl'], m=m_ref[h])
            chains.append(ch)
    ys, m_new = _rwkv_solve(chains, c)
    yf_ref[0] = jnp.concatenate(ys[:nh], axis=1)
    yb_ref[0] = jnp.concatenate(ys[nh:], axis=1)
    for h in range(nh):
        m_f[h] = m_new[h]
        m_b[h] = m_new[nh + h]

    @pl.when(i == n - 1)
    def _():
        mend_ref[0, 0] = m_f[...]
        mend_ref[0, 1] = m_b[...]


def _rwkv_scan(brkv, blo, m0, w0, w2, a0, a2, k_k, k_a, r_k):
    bsz, t, _ = brkv.shape
    c = RWKV_CHUNK
    nt = t // c
    w = B_WIDTH
    full = lambda z: pl.BlockSpec(z.shape, lambda b, i: (0,) * z.ndim)
    fwd = lambda width: pl.BlockSpec((1, c, width), lambda b, i: (b, i, 0))
    bwd = lambda width: pl.BlockSpec((1, c, width), lambda b, i: (b, nt - 1 - i, 0))
    nh = w // HEAD_DIM
    st = pl.BlockSpec((1, 2, nh, HEAD_DIM, HEAD_DIM), lambda b, i: (b, 0, 0, 0, 0))
    o = jax.ShapeDtypeStruct((bsz, t, w), F32)
    return pl.pallas_call(
        _rwkv_kernel,
        grid=(bsz, nt),
        in_specs=[fwd(3 * w), fwd(256), bwd(3 * w), bwd(256), st,
                  full(w0), full(w2), full(a0), full(a2), full(k_k), full(k_a), full(r_k)],
        out_specs=[fwd(w), bwd(w), fwd(w), bwd(w), st],
        out_shape=[o, o, o, o, jax.ShapeDtypeStruct((bsz, 2, nh, HEAD_DIM, HEAD_DIM), F32)],
        scratch_shapes=[pltpu.VMEM((nh, HEAD_DIM, HEAD_DIM), F32), pltpu.VMEM((nh, HEAD_DIM, HEAD_DIM), F32)],
        compiler_params=_params("arbitrary", "arbitrary"),
        name="rwkv_scan",
    )(brkv, blo, brkv, blo, m0, w0, w2, a0, a2, k_k, k_a, r_k)


def _na_kernel(q_ref, k_ref, v_ref, kc_ref, vc_ref, bias_ref, o_ref):
    r = pl.program_id(1)
    rows = pl.num_programs(1)
    kh = bias_ref.shape[3] // GRID_W
    rs = jnp.clip(r - kh // 2, 0, rows - kh)
    start = pl.multiple_of(rs * GRID_W, GRID_W)
    q = q_ref[0]
    kb = k_ref[0, pl.ds(start, kh * GRID_W), :]
    vb = v_ref[0, pl.ds(start, kh * GRID_W), :]
    kc = kc_ref[0]
    vc = vc_ref[0]
    sls = [slice(h * HEAD_DIM, (h + 1) * HEAD_DIM) for h in range(q.shape[1] // HEAD_DIM)]
    s_w = [_dot_nt(q[:, sl], kb[:, sl]) + bias_ref[0, h] for h, sl in enumerate(sls)]
    s_c = [_dot_nt(q[:, sl], kc[:, sl]) for sl in sls]
    m = [jnp.maximum(jnp.max(a, axis=-1, keepdims=True), jnp.max(b, axis=-1, keepdims=True))
         for a, b in zip(s_w, s_c)]
    p_w = [jnp.exp(a - mm) for a, mm in zip(s_w, m)]
    p_c = [jnp.exp(b - mm) for b, mm in zip(s_c, m)]
    den = [jnp.sum(a, axis=-1, keepdims=True) + jnp.sum(b, axis=-1, keepdims=True) for a, b in zip(p_w, p_c)]
    outs = [(_dot(a.astype(BF16), vb[:, sl]) + _dot(b.astype(BF16), vc[:, sl])) / d
            for a, b, d, sl in zip(p_w, p_c, den, sls)]
    o_ref[0] = jnp.concatenate(outs, axis=1)


def _na_bias_table(rpb, rows):
    kh = min(WIN_H, rows)
    nh, nr, ncol = rpb.shape
    qcol = np.arange(GRID_W)[:, None]
    kcol = np.arange(GRID_W)[None, :]
    cstart = np.clip(qcol - WIN_W // 2, 0, GRID_W - WIN_W)
    valid = (kcol >= cstart) & (kcol < cstart + WIN_W)
    hot = ((kcol - qcol + (WIN_W - 1))[None] == np.arange(ncol)[:, None, None]) & valid[None]
    hot = jnp.asarray(hot.reshape(ncol, GRID_W * GRID_W), F32)
    toe = _dot(rpb.reshape(nh * nr, ncol), hot, HI).reshape(nh, nr, GRID_W, GRID_W)
    toe = jnp.where(jnp.asarray(valid)[None, None], toe, MASK_VALUE)
    tab = [jnp.concatenate([toe[:, i - var + (WIN_H - 1)] for i in range(kh)], axis=-1) for var in range(kh)]
    return jnp.stack(tab)


def _na_attention(q, k, v, kc, vc, bias_tab):
    bsz, seq, cw = q.shape
    ctx_len = kc.shape[1]
    rows = seq // GRID_W
    kh = bias_tab.shape[0]
    nh = bias_tab.shape[1]

    def bias_map(b, r):
        return (r - jnp.clip(r - kh // 2, 0, rows - kh), 0, 0, 0)

    return pl.pallas_call(
        _na_kernel,
        grid=(bsz, rows),
        in_specs=[pl.BlockSpec((1, GRID_W, cw), lambda b, r: (b, r, 0)),
                  pl.BlockSpec((1, seq, cw), lambda b, r: (b, 0, 0)),
                  pl.BlockSpec((1, seq, cw), lambda b, r: (b, 0, 0)),
                  pl.BlockSpec((1, ctx_len, cw), lambda b, r: (b, 0, 0)),
                  pl.BlockSpec((1, ctx_len, cw), lambda b, r: (b, 0, 0)),
                  pl.BlockSpec((1, nh, GRID_W, kh * GRID_W), bias_map)],
        out_specs=pl.BlockSpec((1, GRID_W, cw), lambda b, r: (b, r, 0)),
        out_shape=jax.ShapeDtypeStruct((bsz, seq, cw), F32),
        compiler_params=_params("arbitrary", "arbitrary"),
        name="na_attention",
    )(q, k, v, kc, vc, bias_tab)


def _ctx_attn_kernel(q_ref, k_ref, v_ref, o_ref):
    q = q_ref[0]
    k = k_ref[0]
    v = v_ref[0]
    outs = []
    for h in range(q.shape[1] // HEAD_DIM):
        sl = slice(h * HEAD_DIM, (h + 1) * HEAD_DIM)
        s = _dot_nt(q[:, sl], k[:, sl])
        p = jnp.exp(s - jnp.max(s, axis=-1, keepdims=True))
        outs.append(_dot(p.astype(BF16), v[:, sl]) / jnp.sum(p, axis=-1, keepdims=True))
    o_ref[0] = jnp.concatenate(outs, axis=1)


def _ctx_attention(q, k, v):
    bsz, n, cw = q.shape
    spec = pl.BlockSpec((1, n, cw), lambda b: (b, 0, 0))
    return pl.pallas_call(
        _ctx_attn_kernel, grid=(bsz,), in_specs=[spec, spec, spec], out_specs=spec,
        out_shape=jax.ShapeDtypeStruct((bsz, n, cw), F32),
        compiler_params=_params("arbitrary"), name="ctx_attention",
    )(q, k, v)


def _outproj_kernel(x_ref, hf_ref, hb_ref, ga_ref, yf_ref, yb_ref, bvf_ref, bvb_ref, lo_ref, yc_ref,
                    wo_ref, g2_ref, lng_ref, lnb_ref, gate_ref, g_ref, sh_ref, sc_ref, rw_ref, rb_ref,
                    xo_ref, h_ref, lg_ref):
    ya = jax.nn.gelu(ga_ref[0]) * (hf_ref[0] + hb_ref[0])

    y = yf_ref[0] + yb_ref[0]
    avg = _head_ones(B_WIDTH).astype(F32) * (1.0 / HEAD_DIM)
    mean = _dot(y, avg, HI)
    yc_ = y - mean
    var = _dot(yc_ * yc_, avg, HI)
    yn = yc_ * lax.rsqrt(var + GN_EPS) * lng_ref[...] + lnb_ref[...] + bvf_ref[0] + bvb_ref[0]
    gate_b = _dot(jax.nn.sigmoid(lo_ref[0][:, 2 * LORA:3 * LORA]), g2_ref[...], HI)
    yb = yn * gate_b

    mix = (_dot(ya.astype(BF16), wo_ref[:A_WIDTH]) + _dot(yb.astype(BF16), wo_ref[A_WIDTH:A_WIDTH + B_WIDTH])
           + _dot(yc_ref[0].astype(BF16), wo_ref[A_WIDTH + B_WIDTH:]))
    x = x_ref[0] + gate_ref[0] * mix
    xo_ref[0] = x
    hn = x * lax.rsqrt(jnp.mean(x * x, axis=-1, keepdims=True) + RMS_EPS) * g_ref[...]
    hn = hn * (1.0 + sc_ref[0]) + sh_ref[0]
    h_ref[0] = hn.reshape(h_ref.shape[1:])
    lg_ref[0] = _dot(hn, rw_ref[...], HI) + rb_ref[...]


def _outproj(x, hf, hb, ga, yf, yb, bvf, bvb, blo, yc, wo, g2, lng, lnb, gate, g, shift, scale, rw, rb, tm):
    bsz, t, d = x.shape
    ne = rw.shape[1]
    blk = lambda w: pl.BlockSpec((1, tm, w), lambda b, i: (b, i, 0))
    full = lambda a: pl.BlockSpec(a.shape, lambda b, i: (0,) * a.ndim)
    vec = pl.BlockSpec((1, 1, d), lambda b, i: (b, 0, 0))
    return pl.pallas_call(
        _outproj_kernel,
        grid=(bsz, t // tm),
        in_specs=[blk(d), blk(A_WIDTH), blk(A_WIDTH), blk(A_WIDTH), blk(B_WIDTH), blk(B_WIDTH),
                  blk(B_WIDTH), blk(B_WIDTH), blk(256), blk(C_WIDTH),
                  full(wo), full(g2), full(lng), full(lnb), vec, full(g), vec, vec, full(rw), full(rb)],
        out_specs=[blk(d), pl.BlockSpec((1, tm) + _row_tile(d), lambda b, i: (b, i, 0, 0)), blk(ne)],
        out_shape=[jax.ShapeDtypeStruct((bsz, t, d), F32), jax.ShapeDtypeStruct((bsz, t) + _row_tile(d), F32),
                   jax.ShapeDtypeStruct((bsz, t, ne), F32)],
        compiler_params=_params("arbitrary", "arbitrary"),
        name="outproj",
    )(x, hf, hb, ga, yf, yb, bvf, bvb, blo, yc, wo, g2, lng, lnb, gate, g, shift, scale, rw, rb)


def _expert_kernel(be_ref, nb_ref, x_ref, wgu_ref, bgu_ref, wdn_ref, bdn_ref, o_ref, wgu_s, wdn_s):
    i = pl.program_id(0)
    e = be_ref[i]
    fresh = jnp.logical_or(i == 0, be_ref[jnp.maximum(i - 1, 0)] != e)

    @pl.when(fresh)
    def _():
        wgu_s[...] = wgu_ref[0].astype(BF16)
        wdn_s[...] = wdn_ref[0].astype(BF16)

    @pl.when(i < nb_ref[0])
    def _():
        dff, d = wdn_s.shape
        x = x_ref[...].reshape(x_ref.shape[0], d)
        gu = _dot(x.astype(BF16), wgu_s[...]) + bgu_ref[0]
        g_ = jnp.minimum(gu[:, :dff], SWIGLU_LIMIT)
        u_ = jnp.clip(gu[:, dff:], -SWIGLU_LIMIT, SWIGLU_LIMIT)
        act = (u_ + 1.0) * (g_ * jax.nn.sigmoid(SWIGLU_ALPHA * g_))
        o_ref[...] = (_dot(act.astype(BF16), wdn_s[...]) + bdn_ref[0]).reshape(o_ref.shape)

    @pl.when(i >= nb_ref[0])
    def _():
        o_ref[...] = jnp.zeros_like(o_ref)


def _experts(block_e, n_used, xb, w_gu, b_gu, w_dn, b_dn, layer):
    cap = xb.shape[0]
    row = xb.shape[1:]
    depth, ne, d, f2 = w_gu.shape
    dff = w_dn.shape[2]
    nb = cap // MOE_BM
    w_gu, b_gu, w_dn, b_dn = (z.reshape((depth * ne,) + s) for z, s in
                              ((w_gu, (d, f2)), (b_gu, (1, f2)), (w_dn, (dff, d)), (b_dn, (1, d))))
    e0 = layer * ne
    grid_spec = pltpu.PrefetchScalarGridSpec(
        num_scalar_prefetch=2,
        grid=(nb,),
        in_specs=[pl.BlockSpec((MOE_BM,) + row, lambda i, be, nu: (i, 0, 0)),
                  pl.BlockSpec((1, d, f2), lambda i, be, nu: (e0 + be[i], 0, 0)),
                  pl.BlockSpec((1, 1, f2), lambda i, be, nu: (e0 + be[i], 0, 0)),
                  pl.BlockSpec((1, dff, d), lambda i, be, nu: (e0 + be[i], 0, 0)),
                  pl.BlockSpec((1, 1, d), lambda i, be, nu: (e0 + be[i], 0, 0))],
        out_specs=pl.BlockSpec((MOE_BM,) + row, lambda i, be, nu: (i, 0, 0)),
        scratch_shapes=[pltpu.VMEM((d, f2), BF16), pltpu.VMEM((dff, d), BF16)],
    )
    return pl.pallas_call(
        _expert_kernel, grid_spec=grid_spec,
        out_shape=jax.ShapeDtypeStruct((cap,) + row, F32),
        compiler_params=_params("arbitrary"), name="experts",
    )(block_e, n_used, xb, w_gu, b_gu, w_dn, b_dn)


def _route_kernel(lg_ref, idx_ref, gate_ref, rank_ref, cnt_ref, carry):
    i = pl.program_id(0)

    @pl.when(i == 0)
    def _():
        carry[...] = jnp.zeros_like(carry)

    l = lg_ref[...]
    tm, ne = l.shape
    lane = lax.broadcasted_iota(jnp.int32, l.shape, 1)
    vals, idxs, hots = [], [], []
    for _ in range(TOP_K):
        m = jnp.max(l, axis=-1, keepdims=True)
        ix = jnp.min(jnp.where(l == m, lane, ne), axis=-1, keepdims=True)
        hot = lane == ix
        vals.append(m)
        idxs.append(ix)
        hots.append(hot)
        l = jnp.where(hot, -jnp.inf, l)
    ex = [jnp.exp(v - vals[0]) for v in vals]
    den = ex[0] + ex[1] + ex[2] + ex[3]
    hot_all = (hots[0] | hots[1] | hots[2] | hots[3]).astype(BF16)
    tr = lax.broadcasted_iota(jnp.int32, (tm, tm), 0)
    tc = lax.broadcasted_iota(jnp.int32, (tm, tm), 1)
    before = _dot((tc < tr).astype(BF16), hot_all) + carry[...]
    ranks = [jnp.sum(jnp.where(h, before, 0.0), axis=-1, keepdims=True) for h in hots]
    carry[...] = carry[...] + jnp.sum(hot_all.astype(F32), axis=0, keepdims=True)
    idx_ref[...] = jnp.concatenate(idxs, axis=1)
    gate_ref[...] = jnp.concatenate([e / den for e in ex], axis=1)
    rank_ref[...] = jnp.concatenate(ranks, axis=1).astype(jnp.int32)
    cnt_ref[...] = carry[...].astype(jnp.int32)


def _route(logits):
    n, ne = logits.shape
    tm = ROUTE_TM
    assert n % tm == 0, (n, tm)
    blk = pl.BlockSpec((tm, TOP_K), lambda i: (i, 0))
    return pl.pallas_call(
        _route_kernel,
        grid=(n // tm,),
        in_specs=[pl.BlockSpec((tm, ne), lambda i: (i, 0))],
        out_specs=[blk, blk, blk, pl.BlockSpec((1, ne), lambda i: (0, 0))],
        out_shape=[jax.ShapeDtypeStruct((n, TOP_K), jnp.int32), jax.ShapeDtypeStruct((n, TOP_K), F32),
                   jax.ShapeDtypeStruct((n, TOP_K), jnp.int32), jax.ShapeDtypeStruct((1, ne), jnp.int32)],
        scratch_shapes=[pltpu.VMEM((1, ne), F32)],
        compiler_params=_params("arbitrary"),
        name="route",
    )(logits)


def _dispatch_kernel(dest_hbm, h_ref, xb_init, xb_hbm, dest_s, sem_idx, sem):
    del xb_init
    i = pl.program_id(0)
    tm = h_ref.shape[0]
    cp = pltpu.make_async_copy(dest_hbm.at[pl.ds(i * tm * TOP_K, tm * TOP_K)], dest_s, sem_idx)
    cp.start()
    cp.wait()

    def issue(t, carry):
        src = h_ref.at[t]
        for j in range(TOP_K):
            pltpu.make_async_copy(src, xb_hbm.at[dest_s[t * TOP_K + j]], sem).start()
        return carry

    lax.fori_loop(0, tm, issue, 0, unroll=8)
    for _ in range(TOP_K):
        pltpu.make_async_copy(h_ref, xb_hbm.at[pl.ds(0, tm)], sem).wait()


def _dispatch(dest_flat, h, cap):
    n = h.shape[0]
    row = h.shape[1:]
    tm = DISPATCH_TM
    assert n % tm == 0, (n, tm)
    any_spec = pl.BlockSpec(memory_space=pl.ANY)
    return pl.pallas_call(
        _dispatch_kernel,
        grid=(n // tm,),
        in_specs=[any_spec, pl.BlockSpec((tm,) + row, lambda i: (i, 0, 0)), any_spec],
        out_specs=any_spec,
        out_shape=jax.ShapeDtypeStruct((cap,) + row, F32),
        scratch_shapes=[pltpu.SMEM((tm * TOP_K,), jnp.int32), pltpu.SemaphoreType.DMA, pltpu.SemaphoreType.DMA],
        input_output_aliases={2: 0},
        compiler_params=pltpu.CompilerParams(dimension_semantics=("arbitrary",), has_side_effects=True),
        name="dispatch",
    )(dest_flat, h, jnp.zeros((cap,) + row, F32))


def _combine_kernel(dest_hbm, yb_hbm, x_ref, gt_ref, mg_ref, g_ref, o_ref, dest_s, buf, sem_idx, sem,
                    *, tok0, final):
    i = pl.program_id(0)
    n = pl.num_programs(0)
    tm = x_ref.shape[0]

    def fetch(blk, slot):
        cp = pltpu.make_async_copy(dest_hbm.at[pl.ds((tok0 + blk * tm) * TOP_K, tm * TOP_K)], dest_s.at[slot],
                                   sem_idx)
        cp.start()
        cp.wait()

        def issue(t, carry):
            for j in range(TOP_K):
                pltpu.make_async_copy(yb_hbm.at[dest_s[slot, t * TOP_K + j]], buf.at[slot, j, t],
                                      sem.at[slot]).start()
            return carry

        lax.fori_loop(0, tm, issue, 0, unroll=8)

    @pl.when(i == 0)
    def _():
        fetch(0, 0)

    @pl.when(i + 1 < n)
    def _():
        fetch(i + 1, (i + 1) % 2)

    slot = i % 2
    for j in range(TOP_K):
        pltpu.make_async_copy(yb_hbm.at[pl.ds(0, tm)], buf.at[slot, j], sem.at[slot]).wait()
    gt = gt_ref[...]
    y = gt[:, 0:1] * buf[slot, 0].reshape(x_ref.shape)
    for j in range(1, TOP_K):
        y = y + gt[:, j:j + 1] * buf[slot, j].reshape(x_ref.shape)
    x = x_ref[...] + mg_ref[0] * y
    if final:
        x = x * lax.rsqrt(jnp.mean(x * x, axis=-1, keepdims=True) + RMS_EPS) * g_ref[...]
    o_ref[...] = x


def _combine(dest_flat, yb, x, gates, mod_gate, g, tok0, final):
    bsz, t, d = x.shape
    n = bsz * t
    tm = min(COMBINE_TM, t)
    per_b = t // tm
    any_spec = pl.BlockSpec(memory_space=pl.ANY)
    out = pl.pallas_call(
        functools.partial(_combine_kernel, tok0=tok0, final=final),
        grid=(n // tm,),
        in_specs=[any_spec, any_spec,
                  pl.BlockSpec((tm, d), lambda i: (i, 0)),
                  pl.BlockSpec((tm, TOP_K), lambda i: (i, 0)),
                  pl.BlockSpec((1, 1, d), lambda i: (i // per_b, 0, 0)),
                  pl.BlockSpec((1, d), lambda i: (0, 0))],
        out_specs=pl.BlockSpec((tm, d), lambda i: (i, 0)),
        out_shape=jax.ShapeDtypeStruct((n, d), F32),
        scratch_shapes=[pltpu.SMEM((2, tm * TOP_K), jnp.int32), pltpu.VMEM((2, TOP_K, tm) + _row_tile(d), F32),
                        pltpu.SemaphoreType.DMA, pltpu.SemaphoreType.DMA((2,))],
        compiler_params=_params("arbitrary"),
        name="combine",
    )(dest_flat, yb, x.reshape(n, d), gates, mod_gate, g)
    return out.reshape(bsz, t, d)


def _moe_plan(logits):
    n, ne = logits.shape
    idx, gates, rank, counts = _route(logits)
    counts = counts[0]
    padded = (counts + MOE_BM - 1) // MOE_BM * MOE_BM
    pad_end = jnp.cumsum(padded)
    pad_start = pad_end - padded
    hot = idx[:, :, None] == jnp.arange(ne, dtype=jnp.int32)[None, None, :]
    dest = jnp.sum(jnp.where(hot, pad_start[None, None, :], 0), axis=-1) + rank
    nb = (n * TOP_K + ne * (MOE_BM - 1) + MOE_BM - 1) // MOE_BM
    block_start = jnp.arange(nb, dtype=jnp.int32) * MOE_BM
    block_e = jnp.minimum(jnp.sum(pad_end[None, :] <= block_start[:, None], axis=1), ne - 1).astype(jnp.int32)
    n_used = (pad_end[-1] // MOE_BM).astype(jnp.int32).reshape(1)
    return dest.reshape(-1).astype(jnp.int32), gates, block_e, n_used, nb * MOE_BM


def _block_diag(w):
    nh, n, _ = w.shape
    eye = jnp.eye(nh, dtype=w.dtype)
    return (w[:, :, None, :] * eye[:, None, :, None]).reshape(nh * n, nh * n)


def _permute_in_cols(w_in):
    a0 = 0
    b0 = 2 * A_WIDTH
    c0 = b0 + 3 * B_WIDTH + 3 * LORA
    d = w_in.shape[0]
    pad = jnp.zeros((d, 256 - 3 * LORA), w_in.dtype)
    return jnp.concatenate([w_in[:, a0:a0 + A_WIDTH], w_in[:, b0:b0 + 3 * B_WIDTH],
                            w_in[:, b0 + 3 * B_WIDTH:c0], pad,
                            w_in[:, A_WIDTH:2 * A_WIDTH], w_in[:, c0:]], axis=1)


def _mixers(p_ctx, p_lat, lp, need_ctx):
    (xcf_c, xcb_c, ga_c, brkv_c, blo_c, q_c, k_c, v_c) = p_ctx
    (xcf_l, xcb_l, ga_l, brkv_l, blo_l, q_l, k_l, v_l) = p_lat
    bsz = xcf_c.shape[0]

    lru_w = (lp['wr_bd'], lp['br'], lp['wi_bd'], lp['bi'], lp['lam'])
    hf_c, hb_c, hend = _lru_scan(xcf_c, xcb_c, jnp.zeros((bsz, 2, A_WIDTH), F32), *lru_w)
    hf_l, hb_l, _ = _lru_scan(xcf_l, xcb_l, hend, *lru_w)

    rw_w = (lp['w0'], lp['w2'], lp['a0'], lp['a2'], lp['k_k'], lp['k_a'], lp['r_k'])
    yf_c, yb_c, bvf_c, bvb_c, mend = _rwkv_scan(brkv_c, blo_c, jnp.zeros((bsz, 2, B_WIDTH // HEAD_DIM, HEAD_DIM, HEAD_DIM), F32), *rw_w)
    yf_l, yb_l, bvf_l, bvb_l, _ = _rwkv_scan(brkv_l, blo_l, mend, *rw_w)

    yc_l = _na_attention(q_l, k_l, v_l, k_c, v_c, lp['bias_tab'])
    lat = (hf_l, hb_l, ga_l, yf_l, yb_l, bvf_l, bvb_l, blo_l, yc_l)
    ctx = None
    if need_ctx:
        yc_c = _ctx_attention(q_c, k_c, v_c)
        ctx = (hf_c, hb_c, ga_c, yf_c, yb_c, bvf_c, bvb_c, blo_c, yc_c)
    return ctx, lat


def kernel(x, c, ctx, c_ctx, ada_w, ada_b, norm_mix_g, norm_ffn_g, w_in, w_out, lru_conv_w, lru_conv_b, lru_wr, lru_br, lru_wi, lru_bi, lru_lambda, rwkv_mu, rwkv_w0, rwkv_w2, rwkv_a0, rwkv_a2, rwkv_g2, rwkv_kk, rwkv_ka, rwkv_rk, rwkv_lnx_g, rwkv_lnx_b, na_rpb, router_w, router_b, moe_w_gu, moe_b_gu, moe_w_dn, moe_b_dn, final_g):
    bsz, seq, d = x.shape
    ctx_len = ctx.shape[1]
    depth = ada_w.shape[0]
    rows = seq // GRID_W
    tm_l = 256
    tm_c = min(256, ctx_len)

    cond = jnp.concatenate([c, c_ctx[None], jnp.zeros((8 - bsz - 1, d), F32)], axis=0)
    mod = _adaln(cond, ada_w, ada_b)

    xl, xc = x, ctx
    for l in range(depth):
        last = l == depth - 1
        ml = [mod[l, :bsz, j * d:(j + 1) * d][:, None, :] for j in range(6)]
        mc = [jnp.broadcast_to(mod[l, bsz:bsz + 1, j * d:(j + 1) * d][:, None, :], (bsz, 1, d)) for j in range(6)]
        n_b = 3 * B_WIDTH
        mu = rwkv_mu[l]
        lp = dict(
            wr_bd=jnp.stack([_block_diag(lru_wr[l, dd]) for dd in range(2)]),
            wi_bd=jnp.stack([_block_diag(lru_wi[l, dd]) for dd in range(2)]),
            br=lru_br[l], bi=lru_bi[l], lam=lru_lambda[l],
            w0=rwkv_w0[l], w2=rwkv_w2[l], a0=rwkv_a0[l], a2=rwkv_a2[l],
            k_k=rwkv_kk[l][None], k_a=rwkv_ka[l][None], r_k=rwkv_rk[l].reshape(1, B_WIDTH),
            bias_tab=_na_bias_table(na_rpb[l], rows),
        )
        w_perm = _permute_in_cols(w_in[l]).astype(BF16)
        mu_perm = jnp.concatenate([mu[:, :n_b], mu[:, n_b:], jnp.zeros((2, 256 - 3 * LORA), F32)], axis=1)
        g_mix = norm_mix_g[l][None]
        g_ffn = norm_ffn_g[l][None]
        wo = w_out[l].astype(BF16)

        p_lat = _inproj(xl, g_mix, ml[0], ml[1], w_perm, lru_conv_w[l], lru_conv_b[l], mu_perm, tm_l)
        p_ctx = _inproj(xc, g_mix, mc[0], mc[1], w_perm, lru_conv_w[l], lru_conv_b[l], mu_perm, tm_c)
        mix_c, mix_l = _mixers(p_ctx, p_lat, lp, not last)

        fin = (wo, rwkv_g2[l], rwkv_lnx_g[l][None], rwkv_lnx_b[l][None])
        rt = (router_w[l], router_b[l][None])
        xl, hl, lg_l = _outproj(xl, *mix_l, *fin, ml[2], g_ffn, ml[3], ml[4], *rt, tm_l)
        ne = router_w.shape[2]
        if last:
            tok = hl.reshape((-1,) + _row_tile(d))
            lg = lg_l.reshape(-1, ne)
            nc = 0
        else:
            xc, hc, lg_c = _outproj(xc, *mix_c, *fin, mc[2], g_ffn, mc[3], mc[4], *rt, tm_c)
            tok = jnp.concatenate([hc.reshape((-1,) + _row_tile(d)), hl.reshape((-1,) + _row_tile(d))], axis=0)
            lg = jnp.concatenate([lg_c.reshape(-1, ne), lg_l.reshape(-1, ne)], axis=0)
            nc = bsz * ctx_len
        dest, gates, block_e, n_used, cap = _moe_plan(lg)
        xb = _dispatch(dest, tok, cap)
        yb = _experts(block_e, n_used, xb, moe_w_gu, moe_b_gu, moe_w_dn, moe_b_dn, l)
        if not last:
            xc = _combine(dest, yb, xc, gates[:nc], mc[5], final_g[None], 0, False)
        xl = _combine(dest, yb, xl, gates[nc:], ml[5], final_g[None], nc, last)
    return xl
```

```python
import functools

import numpy as np
import jax
import jax.numpy as jnp
from jax import lax
from jax.experimental import pallas as pl
from jax.experimental.pallas import tpu as pltpu

F32 = jnp.float32
BF16 = jnp.bfloat16
HI = lax.Precision.HIGHEST

RMS_EPS = 1e-6
GN_EPS = 64e-5
LRU_C = 8.0
SWIGLU_ALPHA = 1.702
SWIGLU_LIMIT = 7.0
TOP_K = 4
GRID_W = 64
WIN_H = 8
WIN_W = 16
HEAD_DIM = 64
A_WIDTH = 256
B_WIDTH = 256
C_WIDTH = 512
LORA = 64
LANES = 128
HALO = 8
RWKV_CHUNK = 64
RWKV_BATCH = 4
NA_BATCH = 2
LRU_CHUNK = 256
MOE_BM = 256
ROUTE_TM = 512
DISPATCH_TM = 256
COMBINE_TM = 128
MASK_VALUE = -1e30
VMEM_LIMIT = 56 * 1024 * 1024


def _dot(a, b, prec=None):
    return jnp.dot(a, b, preferred_element_type=F32, precision=prec)


def _dot_nt(a, b, prec=None):
    return lax.dot_general(a, b, (((1,), (1,)), ((), ())), preferred_element_type=F32, precision=prec)


def _dot_tn(a, b, prec=None):
    return lax.dot_general(a, b, (((0,), (0,)), ((), ())), preferred_element_type=F32, precision=prec)


def _split(x, n):
    terms = []
    for _ in range(n):
        t = x.astype(BF16)
        terms.append(t)
        x = x - t.astype(F32)
    return terms


def _dot_split(a, b, order):
    a_t = [a] if a.dtype == BF16 else _split(a, order)
    b_t = [b] if b.dtype == BF16 else _split(b, order)
    out = None
    for i, x in enumerate(a_t):
        for j, y in enumerate(b_t):
            if i + j < order:
                p = _dot(x, y)
                out = p if out is None else out + p
    return out


def _softplus(z):
    return jnp.maximum(z, 0.0) + jnp.log1p(jnp.exp(-jnp.abs(z)))


def _params(*sem):
    return pltpu.CompilerParams(dimension_semantics=sem, vmem_limit_bytes=VMEM_LIMIT)


def _row_tile(d):
    assert d % LANES == 0, d
    return (d // LANES, LANES)


def _head_ones(width):
    r = lax.broadcasted_iota(jnp.int32, (width, width), 0) // HEAD_DIM
    c = lax.broadcasted_iota(jnp.int32, (width, width), 1) // HEAD_DIM
    return r == c


def _adaln_kernel(cond_ref, w_ref, b_ref, o_ref):
    c = cond_ref[...]
    o_ref[0] = _dot(c * jax.nn.sigmoid(c), w_ref[0], HI) + b_ref[0]


def _adaln(cond, ada_w, ada_b):
    depth, d, n = ada_w.shape
    tn = 1536
    return pl.pallas_call(
        _adaln_kernel,
        grid=(depth, n // tn),
        in_specs=[pl.BlockSpec((8, d), lambda l, j: (0, 0)),
                  pl.BlockSpec((1, d, tn), lambda l, j: (l, 0, j)),
                  pl.BlockSpec((1, 1, tn), lambda l, j: (l, 0, j))],
        out_specs=pl.BlockSpec((1, 8, tn), lambda l, j: (l, 0, j)),
        out_shape=jax.ShapeDtypeStruct((depth, 8, n), F32),
        compiler_params=_params("arbitrary", "arbitrary"),
        name="adaln",
    )(cond, ada_w, ada_b.reshape(depth, 1, n))


N_SHIFT = 1280
N_PROJ = 3072


def _inproj_kernel(x_ref, xp_ref, xn_ref, g_ref, sh_ref, sc_ref, w_ref, cw_ref, cb_ref, mu_ref,
                   xcf_ref, xcb_ref, ga_ref, brkv_ref, blo_ref, q_ref, k_ref, v_ref):
    i = pl.program_id(1)
    n = pl.num_programs(1)
    tm = x_ref.shape[1]
    g = g_ref[...]
    sh = sh_ref[0]
    sc = sc_ref[0]

    def norm_mod(x):
        y = x * lax.rsqrt(jnp.mean(x * x, axis=-1, keepdims=True) + RMS_EPS) * g
        return y * (1.0 + sc) + sh

    p = _dot(norm_mod(x_ref[0]).astype(BF16), w_ref[...])
    halo = jnp.concatenate([xp_ref[0], xn_ref[0]], axis=0)
    ph = _dot(norm_mod(halo).astype(BF16), w_ref[:, :N_SHIFT])
    p_prev = jnp.where(i > 0, ph[:HALO], 0.0)
    p_next = jnp.where(i < n - 1, ph[HALO:], 0.0)
    ext = jnp.concatenate([p_prev, p[:, :N_SHIFT], p_next], axis=0)

    xa = ext[:, :A_WIDTH]
    cw = cw_ref[...]
    cb = cb_ref[...]
    xcf = cb[0:1]
    xcb = cb[1:2]
    for j in range(cw.shape[1]):
        xcf = xcf + cw[0, j:j + 1] * xa[HALO - j:HALO - j + tm]
        xcb = xcb + cw[1, j:j + 1] * xa[HALO + j:HALO + j + tm]
    xcf_ref[0] = xcf
    xcb_ref[0] = xcb

    pb = ext[HALO:HALO + tm, A_WIDTH:]
    prev = ext[HALO - 1:HALO - 1 + tm, A_WIDTH:]
    nxt = ext[HALO + 1:HALO + 1 + tm, A_WIDTH:]
    mu = mu_ref[...]
    sb = pb + mu[0:1] * (prev - pb) + mu[1:2] * (nxt - pb)
    brkv_ref[0] = sb[:, :3 * B_WIDTH]
    blo_ref[0] = sb[:, 3 * B_WIDTH:]

    o = N_SHIFT
    ga_ref[0] = p[:, o:o + A_WIDTH]
    o += A_WIDTH
    q_ref[0] = (p[:, o:o + C_WIDTH] * (HEAD_DIM ** -0.5)).astype(BF16)
    k_ref[0] = p[:, o + C_WIDTH:o + 2 * C_WIDTH].astype(BF16)
    v_ref[0] = p[:, o + 2 * C_WIDTH:o + 3 * C_WIDTH].astype(BF16)


def _inproj(x, g, shift, scale, w_perm, conv_w, conv_b, mu_perm, tm):
    bsz, t, d = x.shape
    nt = t // tm
    hb = tm // HALO
    last = t // HALO - 1
    f = lambda shape, dt=F32: jax.ShapeDtypeStruct(shape, dt)
    blk = lambda w: pl.BlockSpec((1, tm, w), lambda b, i: (b, i, 0))
    full = lambda a: pl.BlockSpec(a.shape, lambda b, i: (0,) * a.ndim)
    return pl.pallas_call(
        _inproj_kernel,
        grid=(bsz, nt),
        in_specs=[blk(d),
                  pl.BlockSpec((1, HALO, d), lambda b, i: (b, jnp.maximum(i * hb - 1, 0), 0)),
                  pl.BlockSpec((1, HALO, d), lambda b, i: (b, jnp.minimum((i + 1) * hb, last), 0)),
                  full(g),
                  pl.BlockSpec((1, 1, d), lambda b, i: (b, 0, 0)),
                  pl.BlockSpec((1, 1, d), lambda b, i: (b, 0, 0)),
                  full(w_perm), full(conv_w), full(conv_b), full(mu_perm)],
        out_specs=[blk(A_WIDTH), blk(A_WIDTH), blk(A_WIDTH), blk(3 * B_WIDTH), blk(256),
                   blk(C_WIDTH), blk(C_WIDTH), blk(C_WIDTH)],
        out_shape=[f((bsz, t, A_WIDTH)), f((bsz, t, A_WIDTH)), f((bsz, t, A_WIDTH)),
                   f((bsz, t, 3 * B_WIDTH)), f((bsz, t, 256)),
                   f((bsz, t, C_WIDTH), BF16), f((bsz, t, C_WIDTH), BF16), f((bsz, t, C_WIDTH), BF16)],
        compiler_params=_params("arbitrary", "arbitrary"),
        name="inproj",
    )(x, x, x, g, shift, scale, w_perm, conv_w, conv_b, mu_perm)


def _chunk_scan(a, b, rev):
    n = a.shape[0]
    row = lax.broadcasted_iota(jnp.int32, a.shape, 0)
    s = 1
    while s < n:
        if rev:
            keep = row < n - s
            a_s = jnp.where(keep, pltpu.roll(a, n - s, 0), 1.0)
            b_s = jnp.where(keep, pltpu.roll(b, n - s, 0), 0.0)
        else:
            keep = row >= s
            a_s = jnp.where(keep, pltpu.roll(a, s, 0), 1.0)
            b_s = jnp.where(keep, pltpu.roll(b, s, 0), 0.0)
        b = a * b_s + b
        a = a * a_s
        s *= 2
    return a, b


def _lru_kernel(xf_ref, xb_ref, h0_ref, wr_ref, br_ref, wi_ref, bi_ref, lam_ref,
                hf_ref, hb_ref, hend_ref, carry):
    i = pl.program_id(1)
    n = pl.num_programs(1)
    ch = xf_ref.shape[1]

    @pl.when(i == 0)
    def _():
        carry[...] = h0_ref[0]

    for d, (x_ref, o_ref) in enumerate(((xf_ref, hf_ref), (xb_ref, hb_ref))):
        x = x_ref[0]
        gate_r = jax.nn.sigmoid(_dot(x, wr_ref[d], HI) + br_ref[d:d + 1])
        gate_i = jax.nn.sigmoid(_dot(x, wi_ref[d], HI) + bi_ref[d:d + 1])
        log_a = -LRU_C * gate_r * _softplus(-lam_ref[d:d + 1])
        a = jnp.exp(log_a)
        b = jnp.sqrt(1.0 - jnp.exp(2.0 * log_a)) * (gate_i * x)
        a_cum, h = _chunk_scan(a, b, rev=bool(d))
        h = h + a_cum * carry[d:d + 1]
        o_ref[0] = h
        carry[d:d + 1] = h[0:1] if d else h[ch - 1:ch]

    @pl.when(i == n - 1)
    def _():
        hend_ref[0] = carry[...]


def _lru_scan(xcf, xcb, h0, wr_bd, br, wi_bd, bi, lam):
    bsz, t, a = xcf.shape
    ch = min(LRU_CHUNK, t)
    nt = t // ch
    full = lambda z: pl.BlockSpec(z.shape, lambda b, i: (0,) * z.ndim)
    fwd = pl.BlockSpec((1, ch, a), lambda b, i: (b, i, 0))
    bwd = pl.BlockSpec((1, ch, a), lambda b, i: (b, nt - 1 - i, 0))
    st = pl.BlockSpec((1, 2, a), lambda b, i: (b, 0, 0))
    return pl.pallas_call(
        _lru_kernel,
        grid=(bsz, nt),
        in_specs=[fwd, bwd, st, full(wr_bd), full(br), full(wi_bd), full(bi), full(lam)],
        out_specs=[fwd, bwd, st],
        out_shape=[jax.ShapeDtypeStruct((bsz, t, a), F32), jax.ShapeDtypeStruct((bsz, t, a), F32),
                   jax.ShapeDtypeStruct((bsz, 2, a), F32)],
        scratch_shapes=[pltpu.VMEM((2, a), F32)],
        compiler_params=_params("arbitrary", "arbitrary"),
        name="lru_scan",
    )(xcf, xcb, h0, wr_bd, br, wi_bd, bi, lam)


def _rwkv_prep(r, k, v, wlo, alo, w0, w2, a0, a2, k_k, k_a, r_k, rev):
    c, width = r.shape
    ones_bd = _head_ones(width).astype(BF16)
    w_raw = -_softplus(-(w0 + _dot_split(jnp.tanh(wlo), w2, 2))) - 0.5
    logw = -jnp.exp(w_raw)
    a_gate = jax.nn.sigmoid(a0 + _dot_split(alo, a2, 2))
    kk = k * k_k
    kk = kk / jnp.maximum(jnp.sqrt(_dot_split(kk * kk, ones_bd, 2)), 1e-12)
    k_eff = k * (1.0 + (a_gate - 1.0) * k_a)
    b_vec = kk * a_gate

    tr = lax.broadcasted_iota(jnp.int32, (c, c), 0)
    tc = lax.broadcasted_iota(jnp.int32, (c, c), 1)
    tri = (tc >= tr) if rev else (tc <= tr)
    cum = _dot_split(tri.astype(BF16), logw, 3)
    total = cum[0:1] if rev else cum[c - 1:c]
    g_inv = jnp.exp(-cum)
    g_end = jnp.exp(total - cum)
    earlier = (tc > tr) if rev else (tc < tr)
    return dict(
        a=(-kk * jnp.exp(cum - logw)).astype(BF16), b=(b_vec * g_inv).astype(BF16),
        k=(k_eff * g_inv).astype(BF16), r=(r * jnp.exp(cum)).astype(BF16), v=v.astype(BF16),
        b_end=(b_vec * g_end).astype(BF16), k_end=(k_eff * g_end).astype(BF16), g_total=jnp.exp(total),
        earlier=earlier, incl=earlier | (tr == tc),
        bonus_v=_dot_split(r * k_eff * r_k, ones_bd, 2) * v)


def _rwkv_solve(chains, c):
    tr = lax.broadcasted_iota(jnp.int32, (c, c), 0)
    tc = lax.broadcasted_iota(jnp.int32, (c, c), 1)
    eye_c = (tr == tc).astype(F32)
    hi = lax.broadcasted_iota(jnp.int32, (HEAD_DIM, HEAD_DIM), 0)
    hj = lax.broadcasted_iota(jnp.int32, (HEAD_DIM, HEAD_DIM), 1)
    n_ab = [jnp.where(ch['earlier'], _dot_nt(ch['a'], ch['b']), 0.0) for ch in chains]
    g_ak = [jnp.where(ch['earlier'], _dot_nt(ch['a'], ch['k']), 0.0).astype(BF16) for ch in chains]
    g_rb = [jnp.where(ch['incl'], _dot_nt(ch['r'], ch['b']), 0.0).astype(BF16) for ch in chains]
    g_rk = [jnp.where(ch['incl'], _dot_nt(ch['r'], ch['k']), 0.0).astype(BF16) for ch in chains]
    gv = [_dot(g, ch['v']).astype(BF16) for g, ch in zip(g_ak, chains)]

    t_inv = [eye_c + n for n in n_ab]
    pw = [n.astype(BF16) for n in n_ab]
    s = 2
    while s < c:
        pw = [_dot(p, p).astype(BF16) for p in pw]
        t_inv = [t + _dot(t.astype(BF16), p) for t, p in zip(t_inv, pw)]
        s *= 2

    aw = [_dot(t.astype(BF16), jnp.concatenate([ch['a'], g], axis=1)).astype(BF16)
          for t, ch, g in zip(t_inv, chains, gv)]
    ry = [_dot(g, x) for g, x in zip(g_rb, aw)]
    m = [ch['m'].astype(BF16) for ch in chains]
    ys = [_dot((ch['r'].astype(F32) + y[:, :HEAD_DIM]).astype(BF16), mm) + y[:, HEAD_DIM:] + _dot(g, ch['v'])
          for ch, y, mm, g in zip(chains, ry, m, g_rk)]
    pq = [_dot_tn(ch['b_end'], x) for ch, x in zip(chains, aw)]
    kv = [_dot_tn(ch['k_end'], ch['v']) for ch in chains]
    m_new = [_dot((p[:, :HEAD_DIM] + jnp.where(hi == hj, ch['g_total'], 0.0)).astype(BF16), mm) + p[:, HEAD_DIM:] + q
             for ch, p, q, mm in zip(chains, pq, kv, m)]
    return ys, m_new


def _rwkv_kernel(rkvf_ref, lof_ref, rkvb_ref, lob_ref, m0_ref, w0_ref, w2_ref, a0_ref, a2_ref,
                 kk_ref, ka_ref, rk_ref, yf_ref, yb_ref, bvf_ref, bvb_ref, mend_ref, m_f, m_b):
    i = pl.program_id(1)
    n = pl.num_programs(1)

    @pl.when(i == 0)
    def _():
        m_f[...] = m0_ref[:, 0]
        m_b[...] = m0_ref[:, 1]

    w = B_WIDTH
    nh = w // HEAD_DIM
    nb, c = rkvf_ref.shape[:2]
    chains = []
    for d, (rkv_ref, lo_ref, bv_ref, m_ref) in enumerate(
            ((rkvf_ref, lof_ref, bvf_ref, m_f), (rkvb_ref, lob_ref, bvb_ref, m_b))):
        for b in range(nb):
            rkv = rkv_ref[b]
            lo = lo_ref[b]
            p = _rwkv_prep(rkv[:, :w], rkv[:, w:2 * w], rkv[:, 2 * w:], lo[:, :LORA], lo[:, LORA:2 * LORA],
                           w0_ref[d:d + 1], w2_ref[d], a0_ref[d:d + 1], a2_ref[d],
                           kk_ref[...], ka_ref[...], rk_ref[...], rev=bool(d))
            bv_ref[b] = p['bonus_v']
            for h in range(nh):
                sl = slice(h * HEAD_DIM, (h + 1) * HEAD_DIM)
                ch = {key: p[key][:, sl] for key in ('a', 'b', 'k', 'r', 'v', 'b_end', 'k_end', 'g_total')}
                ch.update(earlier=p['earlier'], incl=p['incl'], m=m_ref[b, h])
                chains.append(ch)
    ys, m_new = _rwkv_solve(chains, c)
    for d, (y_ref, m_ref) in enumerate(((yf_ref, m_f), (yb_ref, m_b))):
        for b in range(nb):
            first = (d * nb + b) * nh
            y_ref[b] = jnp.concatenate(ys[first:first + nh], axis=1)
            for h in range(nh):
                m_ref[b, h] = m_new[first + h]

    @pl.when(i == n - 1)
    def _():
        mend_ref[:, 0] = m_f[...]
        mend_ref[:, 1] = m_b[...]


def _rwkv_scan(brkv, blo, m0, w0, w2, a0, a2, k_k, k_a, r_k):
    bsz, t, _ = brkv.shape
    c = RWKV_CHUNK
    nt = t // c
    w = B_WIDTH
    nb = RWKV_BATCH if bsz % RWKV_BATCH == 0 else 1
    full = lambda z: pl.BlockSpec(z.shape, lambda b, i: (0,) * z.ndim)
    fwd = lambda width: pl.BlockSpec((nb, c, width), lambda b, i: (b, i, 0))
    bwd = lambda width: pl.BlockSpec((nb, c, width), lambda b, i: (b, nt - 1 - i, 0))
    nh = w // HEAD_DIM
    st = pl.BlockSpec((nb, 2, nh, HEAD_DIM, HEAD_DIM), lambda b, i: (b, 0, 0, 0, 0))
    o = jax.ShapeDtypeStruct((bsz, t, w), F32)
    return pl.pallas_call(
        _rwkv_kernel,
        grid=(bsz // nb, nt),
        in_specs=[fwd(3 * w), fwd(256), bwd(3 * w), bwd(256), st,
                  full(w0), full(w2), full(a0), full(a2), full(k_k), full(k_a), full(r_k)],
        out_specs=[fwd(w), bwd(w), fwd(w), bwd(w), st],
        out_shape=[o, o, o, o, jax.ShapeDtypeStruct((bsz, 2, nh, HEAD_DIM, HEAD_DIM), F32)],
        scratch_shapes=[pltpu.VMEM((nb, nh, HEAD_DIM, HEAD_DIM), F32), pltpu.VMEM((nb, nh, HEAD_DIM, HEAD_DIM), F32)],
        compiler_params=_params("arbitrary", "arbitrary"),
        name="rwkv_scan",
    )(brkv, blo, brkv, blo, m0, w0, w2, a0, a2, k_k, k_a, r_k)


def _na_kernel(q_ref, k_ref, v_ref, kc_ref, vc_ref, bias_ref, o_ref):
    r = pl.program_id(1)
    rows = pl.num_programs(1)
    kh = bias_ref.shape[3] // GRID_W
    rs = jnp.clip(r - kh // 2, 0, rows - kh)
    start = pl.multiple_of(rs * GRID_W, GRID_W)
    nb = q_ref.shape[0]
    nh = q_ref.shape[2] // HEAD_DIM
    q = [q_ref[b] for b in range(nb)]
    kb = [k_ref[b, pl.ds(start, kh * GRID_W), :] for b in range(nb)]
    vb = [v_ref[b, pl.ds(start, kh * GRID_W), :] for b in range(nb)]
    kc = [kc_ref[b] for b in range(nb)]
    vc = [vc_ref[b] for b in range(nb)]
    ch = [(b, h, slice(h * HEAD_DIM, (h + 1) * HEAD_DIM)) for b in range(nb) for h in range(nh)]
    s_w = [_dot_nt(q[b][:, sl], kb[b][:, sl]) + bias_ref[0, h] for b, h, sl in ch]
    s_c = [_dot_nt(q[b][:, sl], kc[b][:, sl]) for b, h, sl in ch]
    m = [jnp.maximum(jnp.max(x, axis=-1, keepdims=True), jnp.max(y, axis=-1, keepdims=True))
         for x, y in zip(s_w, s_c)]
    p_w = [jnp.exp(x - mm) for x, mm in zip(s_w, m)]
    p_c = [jnp.exp(y - mm) for y, mm in zip(s_c, m)]
    den = [jnp.sum(x, axis=-1, keepdims=True) + jnp.sum(y, axis=-1, keepdims=True) for x, y in zip(p_w, p_c)]
    outs = [(_dot(x.astype(BF16), vb[b][:, sl]) + _dot(y.astype(BF16), vc[b][:, sl])) / d
            for x, y, d, (b, h, sl) in zip(p_w, p_c, den, ch)]
    for b in range(nb):
        o_ref[b] = jnp.concatenate(outs[b * nh:(b + 1) * nh], axis=1)


def _na_bias_table(rpb, rows):
    kh = min(WIN_H, rows)
    nh, nr, ncol = rpb.shape
    qcol = np.arange(GRID_W)[:, None]
    kcol = np.arange(GRID_W)[None, :]
    cstart = np.clip(qcol - WIN_W // 2, 0, GRID_W - WIN_W)
    valid = (kcol >= cstart) & (kcol < cstart + WIN_W)
    hot = ((kcol - qcol + (WIN_W - 1))[None] == np.arange(ncol)[:, None, None]) & valid[None]
    hot = jnp.asarray(hot.reshape(ncol, GRID_W * GRID_W), F32)
    toe = _dot(rpb.reshape(nh * nr, ncol), hot, HI).reshape(nh, nr, GRID_W, GRID_W)
    toe = jnp.where(jnp.asarray(valid)[None, None], toe, MASK_VALUE)
    tab = [jnp.concatenate([toe[:, i - var + (WIN_H - 1)] for i in range(kh)], axis=-1) for var in range(kh)]
    return jnp.stack(tab)


def _na_attention(q, k, v, kc, vc, bias_tab):
    bsz, seq, cw = q.shape
    ctx_len = kc.shape[1]
    rows = seq // GRID_W
    kh = bias_tab.shape[0]
    nh = bias_tab.shape[1]

    def bias_map(b, r):
        return (r - jnp.clip(r - kh // 2, 0, rows - kh), 0, 0, 0)

    nb = NA_BATCH if bsz % NA_BATCH == 0 else 1
    return pl.pallas_call(
        _na_kernel,
        grid=(bsz // nb, rows),
        in_specs=[pl.BlockSpec((nb, GRID_W, cw), lambda b, r: (b, r, 0)),
                  pl.BlockSpec((nb, seq, cw), lambda b, r: (b, 0, 0)),
                  pl.BlockSpec((nb, seq, cw), lambda b, r: (b, 0, 0)),
                  pl.BlockSpec((nb, ctx_len, cw), lambda b, r: (b, 0, 0)),
                  pl.BlockSpec((nb, ctx_len, cw), lambda b, r: (b, 0, 0)),
                  pl.BlockSpec((1, nh, GRID_W, kh * GRID_W), bias_map)],
        out_specs=pl.BlockSpec((nb, GRID_W, cw), lambda b, r: (b, r, 0)),
        out_shape=jax.ShapeDtypeStruct((bsz, seq, cw), F32),
        compiler_params=_params("arbitrary", "arbitrary"),
        name="na_attention",
    )(q, k, v, kc, vc, bias_tab)


def _ctx_attn_kernel(q_ref, k_ref, v_ref, o_ref):
    q = q_ref[0]
    k = k_ref[0]
    v = v_ref[0]
    outs = []
    for h in range(q.shape[1] // HEAD_DIM):
        sl = slice(h * HEAD_DIM, (h + 1) * HEAD_DIM)
        s = _dot_nt(q[:, sl], k[:, sl])
        p = jnp.exp(s - jnp.max(s, axis=-1, keepdims=True))
        outs.append(_dot(p.astype(BF16), v[:, sl]) / jnp.sum(p, axis=-1, keepdims=True))
    o_ref[0] = jnp.concatenate(outs, axis=1)


def _ctx_attention(q, k, v):
    bsz, n, cw = q.shape
    spec = pl.BlockSpec((1, n, cw), lambda b: (b, 0, 0))
    return pl.pallas_call(
        _ctx_attn_kernel, grid=(bsz,), in_specs=[spec, spec, spec], out_specs=spec,
        out_shape=jax.ShapeDtypeStruct((bsz, n, cw), F32),
        compiler_params=_params("arbitrary"), name="ctx_attention",
    )(q, k, v)


def _outproj_kernel(x_ref, hf_ref, hb_ref, ga_ref, yf_ref, yb_ref, bvf_ref, bvb_ref, lo_ref, yc_ref,
                    wo_ref, g2_ref, lng_ref, lnb_ref, gate_ref, g_ref, sh_ref, sc_ref, rw_ref, rb_ref,
                    xo_ref, h_ref, lg_ref):
    ya = jax.nn.gelu(ga_ref[0]) * (hf_ref[0] + hb_ref[0])

    y = yf_ref[0] + yb_ref[0]
    avg = (_head_ones(B_WIDTH).astype(F32) * (1.0 / HEAD_DIM)).astype(BF16)
    mean = _dot_split(y, avg, 3)
    yc_ = y - mean
    var = _dot_split(yc_ * yc_, avg, 2)
    yn = yc_ * lax.rsqrt(var + GN_EPS) * lng_ref[...] + lnb_ref[...] + bvf_ref[0] + bvb_ref[0]
    gate_b = _dot(jax.nn.sigmoid(lo_ref[0][:, 2 * LORA:3 * LORA]), g2_ref[...], HI)
    yb = yn * gate_b

    mix = (_dot(ya.astype(BF16), wo_ref[:A_WIDTH]) + _dot(yb.astype(BF16), wo_ref[A_WIDTH:A_WIDTH + B_WIDTH])
           + _dot(yc_ref[0].astype(BF16), wo_ref[A_WIDTH + B_WIDTH:]))
    x = x_ref[0] + gate_ref[0] * mix
    xo_ref[0] = x
    hn = x * lax.rsqrt(jnp.mean(x * x, axis=-1, keepdims=True) + RMS_EPS) * g_ref[...]
    hn = hn * (1.0 + sc_ref[0]) + sh_ref[0]
    h_ref[0] = hn.reshape(h_ref.shape[1:])
    lg_ref[0] = _dot(hn, rw_ref[...], HI) + rb_ref[...]


def _outproj(x, hf, hb, ga, yf, yb, bvf, bvb, blo, yc, wo, g2, lng, lnb, gate, g, shift, scale, rw, rb, tm):
    bsz, t, d = x.shape
    ne = rw.shape[1]
    blk = lambda w: pl.BlockSpec((1, tm, w), lambda b, i: (b, i, 0))
    full = lambda a: pl.BlockSpec(a.shape, lambda b, i: (0,) * a.ndim)
    vec = pl.BlockSpec((1, 1, d), lambda b, i: (b, 0, 0))
    return pl.pallas_call(
        _outproj_kernel,
        grid=(bsz, t // tm),
        in_specs=[blk(d), blk(A_WIDTH), blk(A_WIDTH), blk(A_WIDTH), blk(B_WIDTH), blk(B_WIDTH),
                  blk(B_WIDTH), blk(B_WIDTH), blk(256), blk(C_WIDTH),
                  full(wo), full(g2), full(lng), full(lnb), vec, full(g), vec, vec, full(rw), full(rb)],
        out_specs=[blk(d), pl.BlockSpec((1, tm) + _row_tile(d), lambda b, i: (b, i, 0, 0)), blk(ne)],
        out_shape=[jax.ShapeDtypeStruct((bsz, t, d), F32), jax.ShapeDtypeStruct((bsz, t) + _row_tile(d), F32),
                   jax.ShapeDtypeStruct((bsz, t, ne), F32)],
        compiler_params=_params("arbitrary", "arbitrary"),
        name="outproj",
    )(x, hf, hb, ga, yf, yb, bvf, bvb, blo, yc, wo, g2, lng, lnb, gate, g, shift, scale, rw, rb)


def _expert_kernel(be_ref, nb_ref, x_ref, wgu_ref, bgu_ref, wdn_ref, bdn_ref, o_ref, wgu_s, wdn_s):
    i = pl.program_id(0)
    e = be_ref[i]
    fresh = jnp.logical_or(i == 0, be_ref[jnp.maximum(i - 1, 0)] != e)

    @pl.when(fresh)
    def _():
        wgu_s[...] = wgu_ref[0].astype(BF16)
        wdn_s[...] = wdn_ref[0].astype(BF16)

    @pl.when(i < nb_ref[0])
    def _():
        dff, d = wdn_s.shape
        x = x_ref[...].reshape(x_ref.shape[0], d)
        gu = _dot(x.astype(BF16), wgu_s[...]) + bgu_ref[0]
        g_ = jnp.minimum(gu[:, :dff], SWIGLU_LIMIT)
        u_ = jnp.clip(gu[:, dff:], -SWIGLU_LIMIT, SWIGLU_LIMIT)
        act = (u_ + 1.0) * (g_ * jax.nn.sigmoid(SWIGLU_ALPHA * g_))
        o_ref[...] = (_dot(act.astype(BF16), wdn_s[...]) + bdn_ref[0]).reshape(o_ref.shape)

    @pl.when(i >= nb_ref[0])
    def _():
        o_ref[...] = jnp.zeros_like(o_ref)


def _experts(block_e, n_used, xb, w_gu, b_gu, w_dn, b_dn, layer):
    cap = xb.shape[0]
    row = xb.shape[1:]
    depth, ne, d, f2 = w_gu.shape
    dff = w_dn.shape[2]
    nb = cap // MOE_BM
    w_gu, b_gu, w_dn, b_dn = (z.reshape((depth * ne,) + s) for z, s in
                              ((w_gu, (d, f2)), (b_gu, (1, f2)), (w_dn, (dff, d)), (b_dn, (1, d))))
    e0 = layer * ne
    grid_spec = pltpu.PrefetchScalarGridSpec(
        num_scalar_prefetch=2,
        grid=(nb,),
        in_specs=[pl.BlockSpec((MOE_BM,) + row, lambda i, be, nu: (i, 0, 0)),
                  pl.BlockSpec((1, d, f2), lambda i, be, nu: (e0 + be[i], 0, 0)),
                  pl.BlockSpec((1, 1, f2), lambda i, be, nu: (e0 + be[i], 0, 0)),
                  pl.BlockSpec((1, dff, d), lambda i, be, nu: (e0 + be[i], 0, 0)),
                  pl.BlockSpec((1, 1, d), lambda i, be, nu: (e0 + be[i], 0, 0))],
        out_specs=pl.BlockSpec((MOE_BM,) + row, lambda i, be, nu: (i, 0, 0)),
        scratch_shapes=[pltpu.VMEM((d, f2), BF16), pltpu.VMEM((dff, d), BF16)],
    )
    return pl.pallas_call(
        _expert_kernel, grid_spec=grid_spec,
        out_shape=jax.ShapeDtypeStruct((cap,) + row, F32),
        compiler_params=_params("arbitrary"), name="experts",
    )(block_e, n_used, xb, w_gu, b_gu, w_dn, b_dn)


def _route_kernel(lg_ref, idx_ref, gate_ref, rank_ref, cnt_ref, carry):
    i = pl.program_id(0)

    @pl.when(i == 0)
    def _():
        carry[...] = jnp.zeros_like(carry)

    l = lg_ref[...]
    tm, ne = l.shape
    lane = lax.broadcasted_iota(jnp.int32, l.shape, 1)
    vals, idxs, hots = [], [], []
    for _ in range(TOP_K):
        m = jnp.max(l, axis=-1, keepdims=True)
        ix = jnp.min(jnp.where(l == m, lane, ne), axis=-1, keepdims=True)
        hot = lane == ix
        vals.append(m)
        idxs.append(ix)
        hots.append(hot)
        l = jnp.where(hot, -jnp.inf, l)
    ex = [jnp.exp(v - vals[0]) for v in vals]
    den = ex[0] + ex[1] + ex[2] + ex[3]
    hot_all = (hots[0] | hots[1] | hots[2] | hots[3]).astype(BF16)
    tr = lax.broadcasted_iota(jnp.int32, (tm, tm), 0)
    tc = lax.broadcasted_iota(jnp.int32, (tm, tm), 1)
    before = _dot((tc < tr).astype(BF16), hot_all) + carry[...]
    ranks = [jnp.sum(jnp.where(h, before, 0.0), axis=-1, keepdims=True) for h in hots]
    carry[...] = carry[...] + jnp.sum(hot_all.astype(F32), axis=0, keepdims=True)
    idx_ref[...] = jnp.concatenate(idxs, axis=1)
    gate_ref[...] = jnp.concatenate([e / den for e in ex], axis=1)
    rank_ref[...] = jnp.concatenate(ranks, axis=1).astype(jnp.int32)
    cnt_ref[...] = carry[...].astype(jnp.int32)


def _route(logits):
    n, ne = logits.shape
    tm = ROUTE_TM
    assert n % tm == 0, (n, tm)
    blk = pl.BlockSpec((tm, TOP_K), lambda i: (i, 0))
    return pl.pallas_call(
        _route_kernel,
        grid=(n // tm,),
        in_specs=[pl.BlockSpec((tm, ne), lambda i: (i, 0))],
        out_specs=[blk, blk, blk, pl.BlockSpec((1, ne), lambda i: (0, 0))],
        out_shape=[jax.ShapeDtypeStruct((n, TOP_K), jnp.int32), jax.ShapeDtypeStruct((n, TOP_K), F32),
                   jax.ShapeDtypeStruct((n, TOP_K), jnp.int32), jax.ShapeDtypeStruct((1, ne), jnp.int32)],
        scratch_shapes=[pltpu.VMEM((1, ne), F32)],
        compiler_params=_params("arbitrary"),
        name="route",
    )(logits)


def _dispatch_kernel(dest_hbm, h_ref, xb_init, xb_hbm, dest_s, sem_idx, sem):
    del xb_init
    i = pl.program_id(0)
    tm = h_ref.shape[0]
    cp = pltpu.make_async_copy(dest_hbm.at[pl.ds(i * tm * TOP_K, tm * TOP_K)], dest_s, sem_idx)
    cp.start()
    cp.wait()

    def issue(t, carry):
        src = h_ref.at[t]
        for j in range(TOP_K):
            pltpu.make_async_copy(src, xb_hbm.at[dest_s[t * TOP_K + j]], sem).start()
        return carry

    lax.fori_loop(0, tm, issue, 0, unroll=8)
    for _ in range(TOP_K):
        pltpu.make_async_copy(h_ref, xb_hbm.at[pl.ds(0, tm)], sem).wait()


def _dispatch(dest_flat, h, cap):
    n = h.shape[0]
    row = h.shape[1:]
    tm = DISPATCH_TM
    assert n % tm == 0, (n, tm)
    any_spec = pl.BlockSpec(memory_space=pl.ANY)
    return pl.pallas_call(
        _dispatch_kernel,
        grid=(n // tm,),
        in_specs=[any_spec, pl.BlockSpec((tm,) + row, lambda i: (i, 0, 0)), any_spec],
        out_specs=any_spec,
        out_shape=jax.ShapeDtypeStruct((cap,) + row, F32),
        scratch_shapes=[pltpu.SMEM((tm * TOP_K,), jnp.int32), pltpu.SemaphoreType.DMA, pltpu.SemaphoreType.DMA],
        input_output_aliases={2: 0},
        compiler_params=pltpu.CompilerParams(dimension_semantics=("arbitrary",), has_side_effects=True),
        name="dispatch",
    )(dest_flat, h, jnp.zeros((cap,) + row, F32))


def _combine_kernel(dest_hbm, yb_hbm, x_ref, gt_ref, mg_ref, g_ref, o_ref, dest_s, buf, sem_idx, sem,
                    *, tok0, final):
    i = pl.program_id(0)
    n = pl.num_programs(0)
    tm = x_ref.shape[0]

    def fetch(blk, slot):
        cp = pltpu.make_async_copy(dest_hbm.at[pl.ds((tok0 + blk * tm) * TOP_K, tm * TOP_K)], dest_s.at[slot],
                                   sem_idx)
        cp.start()
        cp.wait()

        def issue(t, carry):
            for j in range(TOP_K):
                pltpu.make_async_copy(yb_hbm.at[dest_s[slot, t * TOP_K + j]], buf.at[slot, j, t],
                                      sem.at[slot]).start()
            return carry

        lax.fori_loop(0, tm, issue, 0, unroll=8)

    @pl.when(i == 0)
    def _():
        fetch(0, 0)

    @pl.when(i + 1 < n)
    def _():
        fetch(i + 1, (i + 1) % 2)

    slot = i % 2
    for j in range(TOP_K):
        pltpu.make_async_copy(yb_hbm.at[pl.ds(0, tm)], buf.at[slot, j], sem.at[slot]).wait()
    gt = gt_ref[...]
    y = gt[:, 0:1] * buf[slot, 0].reshape(x_ref.shape)
    for j in range(1, TOP_K):
        y = y + gt[:, j:j + 1] * buf[slot, j].reshape(x_ref.shape)
    x = x_ref[...] + mg_ref[0] * y
    if final:
        x = x * lax.rsqrt(jnp.mean(x * x, axis=-1, keepdims=True) + RMS_EPS) * g_ref[...]
    o_ref[...] = x


def _combine(dest_flat, yb, x, gates, mod_gate, g, tok0, final):
    bsz, t, d = x.shape
    n = bsz * t
    tm = min(COMBINE_TM, t)
    per_b = t // tm
    any_spec = pl.BlockSpec(memory_space=pl.ANY)
    out = pl.pallas_call(
        functools.partial(_combine_kernel, tok0=tok0, final=final),
        grid=(n // tm,),
        in_specs=[any_spec, any_spec,
                  pl.BlockSpec((tm, d), lambda i: (i, 0)),
                  pl.BlockSpec((tm, TOP_K), lambda i: (i, 0)),
                  pl.BlockSpec((1, 1, d), lambda i: (i // per_b, 0, 0)),
                  pl.BlockSpec((1, d), lambda i: (0, 0))],
        out_specs=pl.BlockSpec((tm, d), lambda i: (i, 0)),
        out_shape=jax.ShapeDtypeStruct((n, d), F32),
        scratch_shapes=[pltpu.SMEM((2, tm * TOP_K), jnp.int32), pltpu.VMEM((2, TOP_K, tm) + _row_tile(d), F32),
                        pltpu.SemaphoreType.DMA, pltpu.SemaphoreType.DMA((2,))],
        compiler_params=_params("arbitrary"),
        name="combine",
    )(dest_flat, yb, x.reshape(n, d), gates, mod_gate, g)
    return out.reshape(bsz, t, d)


def _moe_plan(logits):
    n, ne = logits.shape
    idx, gates, rank, counts = _route(logits)
    counts = counts[0]
    padded = (counts + MOE_BM - 1) // MOE_BM * MOE_BM
    pad_end = jnp.cumsum(padded)
    pad_start = pad_end - padded
    hot = idx[:, :, None] == jnp.arange(ne, dtype=jnp.int32)[None, None, :]
    dest = jnp.sum(jnp.where(hot, pad_start[None, None, :], 0), axis=-1) + rank
    nb = (n * TOP_K + ne * (MOE_BM - 1) + MOE_BM - 1) // MOE_BM
    block_start = jnp.arange(nb, dtype=jnp.int32) * MOE_BM
    block_e = jnp.minimum(jnp.sum(pad_end[None, :] <= block_start[:, None], axis=1), ne - 1).astype(jnp.int32)
    n_used = (pad_end[-1] // MOE_BM).astype(jnp.int32).reshape(1)
    return dest.reshape(-1).astype(jnp.int32), gates, block_e, n_used, nb * MOE_BM


def _block_diag(w):
    nh, n, _ = w.shape
    eye = jnp.eye(nh, dtype=w.dtype)
    return (w[:, :, None, :] * eye[:, None, :, None]).reshape(nh * n, nh * n)


def _permute_in_cols(w_in):
    a0 = 0
    b0 = 2 * A_WIDTH
    c0 = b0 + 3 * B_WIDTH + 3 * LORA
    d = w_in.shape[0]
    pad = jnp.zeros((d, 256 - 3 * LORA), w_in.dtype)
    return jnp.concatenate([w_in[:, a0:a0 + A_WIDTH], w_in[:, b0:b0 + 3 * B_WIDTH],
                            w_in[:, b0 + 3 * B_WIDTH:c0], pad,
                            w_in[:, A_WIDTH:2 * A_WIDTH], w_in[:, c0:]], axis=1)


def _mixers(p_ctx, p_lat, lp, need_ctx):
    (xcf_c, xcb_c, ga_c, brkv_c, blo_c, q_c, k_c, v_c) = p_ctx
    (xcf_l, xcb_l, ga_l, brkv_l, blo_l, q_l, k_l, v_l) = p_lat
    bsz = xcf_c.shape[0]

    lru_w = (lp['wr_bd'], lp['br'], lp['wi_bd'], lp['bi'], lp['lam'])
    hf_c, hb_c, hend = _lru_scan(xcf_c, xcb_c, jnp.zeros((bsz, 2, A_WIDTH), F32), *lru_w)
    hf_l, hb_l, _ = _lru_scan(xcf_l, xcb_l, hend, *lru_w)

    rw_w = (lp['w0'], lp['w2'], lp['a0'], lp['a2'], lp['k_k'], lp['k_a'], lp['r_k'])
    yf_c, yb_c, bvf_c, bvb_c, mend = _rwkv_scan(brkv_c, blo_c, jnp.zeros((bsz, 2, B_WIDTH // HEAD_DIM, HEAD_DIM, HEAD_DIM), F32), *rw_w)
    yf_l, yb_l, bvf_l, bvb_l, _ = _rwkv_scan(brkv_l, blo_l, mend, *rw_w)

    yc_l = _na_attention(q_l, k_l, v_l, k_c, v_c, lp['bias_tab'])
    lat = (hf_l, hb_l, ga_l, yf_l, yb_l, bvf_l, bvb_l, blo_l, yc_l)
    ctx = None
    if need_ctx:
        yc_c = _ctx_attention(q_c, k_c, v_c)
        ctx = (hf_c, hb_c, ga_c, yf_c, yb_c, bvf_c, bvb_c, blo_c, yc_c)
    return ctx, lat


def kernel(x, c, ctx, c_ctx, ada_w, ada_b, norm_mix_g, norm_ffn_g, w_in, w_out, lru_conv_w, lru_conv_b, lru_wr, lru_br, lru_wi, lru_bi, lru_lambda, rwkv_mu, rwkv_w0, rwkv_w2, rwkv_a0, rwkv_a2, rwkv_g2, rwkv_kk, rwkv_ka, rwkv_rk, rwkv_lnx_g, rwkv_lnx_b, na_rpb, router_w, router_b, moe_w_gu, moe_b_gu, moe_w_dn, moe_b_dn, final_g):
    bsz, seq, d = x.shape
    ctx_len = ctx.shape[1]
    depth = ada_w.shape[0]
    rows = seq // GRID_W
    tm_l = 256
    tm_c = min(256, ctx_len)

    cond = jnp.concatenate([c, c_ctx[None], jnp.zeros((8 - bsz - 1, d), F32)], axis=0)
    mod = _adaln(cond, ada_w, ada_b)

    xl, xc = x, ctx
    for l in range(depth):
        last = l == depth - 1
        ml = [mod[l, :bsz, j * d:(j + 1) * d][:, None, :] for j in range(6)]
        mc = [jnp.broadcast_to(mod[l, bsz:bsz + 1, j * d:(j + 1) * d][:, None, :], (bsz, 1, d)) for j in range(6)]
        n_b = 3 * B_WIDTH
        mu = rwkv_mu[l]
        lp = dict(
            wr_bd=jnp.stack([_block_diag(lru_wr[l, dd]) for dd in range(2)]),
            wi_bd=jnp.stack([_block_diag(lru_wi[l, dd]) for dd in range(2)]),
            br=lru_br[l], bi=lru_bi[l], lam=lru_lambda[l],
            w0=rwkv_w0[l], w2=rwkv_w2[l], a0=rwkv_a0[l], a2=rwkv_a2[l],
            k_k=rwkv_kk[l][None], k_a=rwkv_ka[l][None], r_k=rwkv_rk[l].reshape(1, B_WIDTH),
            bias_tab=_na_bias_table(na_rpb[l], rows),
        )
        w_perm = _permute_in_cols(w_in[l]).astype(BF16)
        mu_perm = jnp.concatenate([mu[:, :n_b], mu[:, n_b:], jnp.zeros((2, 256 - 3 * LORA), F32)], axis=1)
        g_mix = norm_mix_g[l][None]
        g_ffn = norm_ffn_g[l][None]
        wo = w_out[l].astype(BF16)

        p_lat = _inproj(xl, g_mix, ml[0], ml[1], w_perm, lru_conv_w[l], lru_conv_b[l], mu_perm, tm_l)
        p_ctx = _inproj(xc, g_mix, mc[0], mc[1], w_perm, lru_conv_w[l], lru_conv_b[l], mu_perm, tm_c)
        mix_c, mix_l = _mixers(p_ctx, p_lat, lp, not last)

        fin = (wo, rwkv_g2[l], rwkv_lnx_g[l][None], rwkv_lnx_b[l][None])
        rt = (router_w[l], router_b[l][None])
        xl, hl, lg_l = _outproj(xl, *mix_l, *fin, ml[2], g_ffn, ml[3], ml[4], *rt, tm_l)
        ne = router_w.shape[2]
        if last:
            tok = hl.reshape((-1,) + _row_tile(d))
            lg = lg_l.reshape(-1, ne)
            nc = 0
        else:
            xc, hc, lg_c = _outproj(xc, *mix_c, *fin, mc[2], g_ffn, mc[3], mc[4], *rt, tm_c)
            tok = jnp.concatenate([hc.reshape((-1,) + _row_tile(d)), hl.reshape((-1,) + _row_tile(d))], axis=0)
            lg = jnp.concatenate([lg_c.reshape(-1, ne), lg_l.reshape(-1, ne)], axis=0)
            nc = bsz * ctx_len
        dest, gates, block_e, n_used, cap = _moe_plan(lg)
        xb = _dispatch(dest, tok, cap)
        yb = _experts(block_e, n_used, xb, moe_w_gu, moe_b_gu, moe_w_dn, moe_b_dn, l)
        if not last:
            xc = _combine(dest, yb, xc, gates[:nc], mc[5], final_g[None], 0, False)
        xl = _combine(dest, yb, xl, gates[nc:], ml[5], final_g[None], nc, last)
    return xl
```

```python
import functools

import numpy as np
import jax
import jax.numpy as jnp
from jax import lax
from jax.experimental import pallas as pl
from jax.experimental.pallas import tpu as pltpu

F32 = jnp.float32
BF16 = jnp.bfloat16
HI = lax.Precision.HIGHEST

RMS_EPS = 1e-6
GN_EPS = 64e-5
LRU_C = 8.0
SWIGLU_ALPHA = 1.702
SWIGLU_LIMIT = 7.0
TOP_K = 4
GRID_W = 64
WIN_H = 8
WIN_W = 16
HEAD_DIM = 64
A_WIDTH = 256
B_WIDTH = 256
C_WIDTH = 512
LORA = 64
LANES = 128
HALO = 8
RWKV_CHUNK = 64
RWKV_BATCH = 4
NA_BATCH = 2
LRU_CHUNK = 256
MOE_BM = 256
ROUTE_TM = 512
DISPATCH_TM = 256
COMBINE_TM = 128
MASK_VALUE = -1e30
VMEM_LIMIT = 56 * 1024 * 1024


def _dot(a, b, prec=None):
    return jnp.dot(a, b, preferred_element_type=F32, precision=prec)


def _dot_nt(a, b, prec=None):
    return lax.dot_general(a, b, (((1,), (1,)), ((), ())), preferred_element_type=F32, precision=prec)


def _dot_tn(a, b, prec=None):
    return lax.dot_general(a, b, (((0,), (0,)), ((), ())), preferred_element_type=F32, precision=prec)


def _split(x, n):
    terms = []
    for _ in range(n):
        t = x.astype(BF16)
        terms.append(t)
        x = x - t.astype(F32)
    return terms


def _dot_split(a, b, order):
    a_t = [a] if a.dtype == BF16 else _split(a, order)
    b_t = [b] if b.dtype == BF16 else _split(b, order)
    return _dot_pieces(a_t, b_t, order)


def _dot_pieces(a_t, b_t, order):
    out = None
    for i, x in enumerate(a_t):
        for j, y in enumerate(b_t):
            if i + j < order:
                p = _dot(x, y)
                out = p if out is None else out + p
    return out


def _softplus(z):
    return jnp.maximum(z, 0.0) + jnp.log1p(jnp.exp(-jnp.abs(z)))


def _params(*sem):
    return pltpu.CompilerParams(dimension_semantics=sem, vmem_limit_bytes=VMEM_LIMIT)


def _row_tile(d):
    assert d % LANES == 0, d
    return (d // LANES, LANES)


def _head_ones(width):
    r = lax.broadcasted_iota(jnp.int32, (width, width), 0) // HEAD_DIM
    c = lax.broadcasted_iota(jnp.int32, (width, width), 1) // HEAD_DIM
    return r == c


def _adaln_kernel(cond_ref, w_ref, b_ref, o_ref):
    c = cond_ref[...]
    o_ref[0] = _dot(c * jax.nn.sigmoid(c), w_ref[0], HI) + b_ref[0]


def _adaln(cond, ada_w, ada_b):
    depth, d, n = ada_w.shape
    tn = 1536
    return pl.pallas_call(
        _adaln_kernel,
        grid=(depth, n // tn),
        in_specs=[pl.BlockSpec((8, d), lambda l, j: (0, 0)),
                  pl.BlockSpec((1, d, tn), lambda l, j: (l, 0, j)),
                  pl.BlockSpec((1, 1, tn), lambda l, j: (l, 0, j))],
        out_specs=pl.BlockSpec((1, 8, tn), lambda l, j: (l, 0, j)),
        out_shape=jax.ShapeDtypeStruct((depth, 8, n), F32),
        compiler_params=_params("arbitrary", "arbitrary"),
        name="adaln",
    )(cond, ada_w, ada_b.reshape(depth, 1, n))


N_SHIFT = 1280
N_PROJ = 3072


def _inproj_kernel(x_ref, xp_ref, xn_ref, g_ref, sh_ref, sc_ref, w_ref, cw_ref, cb_ref, mu_ref,
                   xcf_ref, xcb_ref, ga_ref, brkv_ref, blo_ref, q_ref, k_ref, v_ref):
    i = pl.program_id(1)
    n = pl.num_programs(1)
    tm = x_ref.shape[1]
    g = g_ref[...]
    sh = sh_ref[0]
    sc = sc_ref[0]

    def norm_mod(x):
        y = x * lax.rsqrt(jnp.mean(x * x, axis=-1, keepdims=True) + RMS_EPS) * g
        return y * (1.0 + sc) + sh

    p = _dot(norm_mod(x_ref[0]).astype(BF16), w_ref[...])
    halo = jnp.concatenate([xp_ref[0], xn_ref[0]], axis=0)
    ph = _dot(norm_mod(halo).astype(BF16), w_ref[:, :N_SHIFT])
    p_prev = jnp.where(i > 0, ph[:HALO], 0.0)
    p_next = jnp.where(i < n - 1, ph[HALO:], 0.0)
    ext = jnp.concatenate([p_prev, p[:, :N_SHIFT], p_next], axis=0)

    xa = ext[:, :A_WIDTH]
    cw = cw_ref[...]
    cb = cb_ref[...]
    xcf = cb[0:1]
    xcb = cb[1:2]
    for j in range(cw.shape[1]):
        xcf = xcf + cw[0, j:j + 1] * xa[HALO - j:HALO - j + tm]
        xcb = xcb + cw[1, j:j + 1] * xa[HALO + j:HALO + j + tm]
    xcf_ref[0] = xcf
    xcb_ref[0] = xcb

    pb = ext[HALO:HALO + tm, A_WIDTH:]
    prev = ext[HALO - 1:HALO - 1 + tm, A_WIDTH:]
    nxt = ext[HALO + 1:HALO + 1 + tm, A_WIDTH:]
    mu = mu_ref[...]
    sb = pb + mu[0:1] * (prev - pb) + mu[1:2] * (nxt - pb)
    brkv_ref[0] = sb[:, :3 * B_WIDTH]
    blo_ref[0] = sb[:, 3 * B_WIDTH:]

    o = N_SHIFT
    ga_ref[0] = p[:, o:o + A_WIDTH]
    o += A_WIDTH
    q_ref[0] = (p[:, o:o + C_WIDTH] * (HEAD_DIM ** -0.5)).astype(BF16)
    k_ref[0] = p[:, o + C_WIDTH:o + 2 * C_WIDTH].astype(BF16)
    v_ref[0] = p[:, o + 2 * C_WIDTH:o + 3 * C_WIDTH].astype(BF16)


def _inproj(x, g, shift, scale, w_perm, conv_w, conv_b, mu_perm, tm):
    bsz, t, d = x.shape
    nt = t // tm
    hb = tm // HALO
    last = t // HALO - 1
    f = lambda shape, dt=F32: jax.ShapeDtypeStruct(shape, dt)
    blk = lambda w: pl.BlockSpec((1, tm, w), lambda b, i: (b, i, 0))
    full = lambda a: pl.BlockSpec(a.shape, lambda b, i: (0,) * a.ndim)
    return pl.pallas_call(
        _inproj_kernel,
        grid=(bsz, nt),
        in_specs=[blk(d),
                  pl.BlockSpec((1, HALO, d), lambda b, i: (b, jnp.maximum(i * hb - 1, 0), 0)),
                  pl.BlockSpec((1, HALO, d), lambda b, i: (b, jnp.minimum((i + 1) * hb, last), 0)),
                  full(g),
                  pl.BlockSpec((1, 1, d), lambda b, i: (b, 0, 0)),
                  pl.BlockSpec((1, 1, d), lambda b, i: (b, 0, 0)),
                  full(w_perm), full(conv_w), full(conv_b), full(mu_perm)],
        out_specs=[blk(A_WIDTH), blk(A_WIDTH), blk(A_WIDTH), blk(3 * B_WIDTH), blk(256),
                   blk(C_WIDTH), blk(C_WIDTH), blk(C_WIDTH)],
        out_shape=[f((bsz, t, A_WIDTH)), f((bsz, t, A_WIDTH)), f((bsz, t, A_WIDTH)),
                   f((bsz, t, 3 * B_WIDTH)), f((bsz, t, 256)),
                   f((bsz, t, C_WIDTH), BF16), f((bsz, t, C_WIDTH), BF16), f((bsz, t, C_WIDTH), BF16)],
        compiler_params=_params("arbitrary", "arbitrary"),
        name="inproj",
    )(x, x, x, g, shift, scale, w_perm, conv_w, conv_b, mu_perm)


def _chunk_scan(a, b, rev):
    n = a.shape[0]
    row = lax.broadcasted_iota(jnp.int32, a.shape, 0)
    s = 1
    while s < n:
        if rev:
            keep = row < n - s
            a_s = jnp.where(keep, pltpu.roll(a, n - s, 0), 1.0)
            b_s = jnp.where(keep, pltpu.roll(b, n - s, 0), 0.0)
        else:
            keep = row >= s
            a_s = jnp.where(keep, pltpu.roll(a, s, 0), 1.0)
            b_s = jnp.where(keep, pltpu.roll(b, s, 0), 0.0)
        b = a * b_s + b
        a = a * a_s
        s *= 2
    return a, b


def _lru_kernel(xf_ref, xb_ref, h0_ref, wr_ref, br_ref, wi_ref, bi_ref, lam_ref,
                hf_ref, hb_ref, hend_ref, carry):
    i = pl.program_id(1)
    n = pl.num_programs(1)
    ch = xf_ref.shape[1]

    @pl.when(i == 0)
    def _():
        carry[...] = h0_ref[0]

    for d, (x_ref, o_ref) in enumerate(((xf_ref, hf_ref), (xb_ref, hb_ref))):
        x = x_ref[0]
        x_t = _split(x, 2)
        gate_r = jax.nn.sigmoid(_dot_pieces(x_t, _split(wr_ref[d], 2), 2) + br_ref[d:d + 1])
        gate_i = jax.nn.sigmoid(_dot_pieces(x_t, _split(wi_ref[d], 2), 2) + bi_ref[d:d + 1])
        log_a = -LRU_C * gate_r * _softplus(-lam_ref[d:d + 1])
        a = jnp.exp(log_a)
        b = jnp.sqrt(1.0 - jnp.exp(2.0 * log_a)) * (gate_i * x)
        a_cum, h = _chunk_scan(a, b, rev=bool(d))
        h = h + a_cum * carry[d:d + 1]
        o_ref[0] = h
        carry[d:d + 1] = h[0:1] if d else h[ch - 1:ch]

    @pl.when(i == n - 1)
    def _():
        hend_ref[0] = carry[...]


def _lru_scan(xcf, xcb, h0, wr_bd, br, wi_bd, bi, lam):
    bsz, t, a = xcf.shape
    ch = min(LRU_CHUNK, t)
    nt = t // ch
    full = lambda z: pl.BlockSpec(z.shape, lambda b, i: (0,) * z.ndim)
    fwd = pl.BlockSpec((1, ch, a), lambda b, i: (b, i, 0))
    bwd = pl.BlockSpec((1, ch, a), lambda b, i: (b, nt - 1 - i, 0))
    st = pl.BlockSpec((1, 2, a), lambda b, i: (b, 0, 0))
    return pl.pallas_call(
        _lru_kernel,
        grid=(bsz, nt),
        in_specs=[fwd, bwd, st, full(wr_bd), full(br), full(wi_bd), full(bi), full(lam)],
        out_specs=[fwd, bwd, st],
        out_shape=[jax.ShapeDtypeStruct((bsz, t, a), F32), jax.ShapeDtypeStruct((bsz, t, a), F32),
                   jax.ShapeDtypeStruct((bsz, 2, a), F32)],
        scratch_shapes=[pltpu.VMEM((2, a), F32)],
        compiler_params=_params("arbitrary", "arbitrary"),
        name="lru_scan",
    )(xcf, xcb, h0, wr_bd, br, wi_bd, bi, lam)


def _rwkv_prep(r, k, v, wlo, alo, w0, w2, a0, a2, k_k, k_a, r_k, rev):
    c, width = r.shape
    ones_bd = _head_ones(width).astype(BF16)
    w_raw = -_softplus(-(w0 + _dot_split(jnp.tanh(wlo), w2, 2))) - 0.5
    logw = -jnp.exp(w_raw)
    a_gate = jax.nn.sigmoid(a0 + _dot_split(alo, a2, 2))
    kk = k * k_k
    kk = kk / jnp.maximum(jnp.sqrt(_dot_split(kk * kk, ones_bd, 2)), 1e-12)
    k_eff = k * (1.0 + (a_gate - 1.0) * k_a)
    b_vec = kk * a_gate

    tr = lax.broadcasted_iota(jnp.int32, (c, c), 0)
    tc = lax.broadcasted_iota(jnp.int32, (c, c), 1)
    tri = (tc >= tr) if rev else (tc <= tr)
    cum = _dot_split(tri.astype(BF16), logw, 3)
    total = cum[0:1] if rev else cum[c - 1:c]
    g_inv = jnp.exp(-cum)
    g_end = jnp.exp(total - cum)
    earlier = (tc > tr) if rev else (tc < tr)
    return dict(
        a=(-kk * jnp.exp(cum - logw)).astype(BF16), b=(b_vec * g_inv).astype(BF16),
        k=(k_eff * g_inv).astype(BF16), r=(r * jnp.exp(cum)).astype(BF16), v=v.astype(BF16),
        b_end=(b_vec * g_end).astype(BF16), k_end=(k_eff * g_end).astype(BF16), g_total=jnp.exp(total),
        earlier=earlier, incl=earlier | (tr == tc),
        bonus_v=_dot_split(r * k_eff * r_k, ones_bd, 2) * v)


def _rwkv_solve(chains, c):
    tr = lax.broadcasted_iota(jnp.int32, (c, c), 0)
    tc = lax.broadcasted_iota(jnp.int32, (c, c), 1)
    eye_c = (tr == tc).astype(F32)
    hi = lax.broadcasted_iota(jnp.int32, (HEAD_DIM, HEAD_DIM), 0)
    hj = lax.broadcasted_iota(jnp.int32, (HEAD_DIM, HEAD_DIM), 1)
    n_ab = [jnp.where(ch['earlier'], _dot_nt(ch['a'], ch['b']), 0.0) for ch in chains]
    g_ak = [jnp.where(ch['earlier'], _dot_nt(ch['a'], ch['k']), 0.0).astype(BF16) for ch in chains]
    g_rb = [jnp.where(ch['incl'], _dot_nt(ch['r'], ch['b']), 0.0).astype(BF16) for ch in chains]
    g_rk = [jnp.where(ch['incl'], _dot_nt(ch['r'], ch['k']), 0.0).astype(BF16) for ch in chains]
    gv = [_dot(g, ch['v']).astype(BF16) for g, ch in zip(g_ak, chains)]

    t_inv = [eye_c + n for n in n_ab]
    pw = [n.astype(BF16) for n in n_ab]
    s = 2
    while s < c:
        pw = [_dot(p, p).astype(BF16) for p in pw]
        t_inv = [t + _dot(t.astype(BF16), p) for t, p in zip(t_inv, pw)]
        s *= 2

    aw = [_dot(t.astype(BF16), jnp.concatenate([ch['a'], g], axis=1)).astype(BF16)
          for t, ch, g in zip(t_inv, chains, gv)]
    ry = [_dot(g, x) for g, x in zip(g_rb, aw)]
    m = [ch['m'].astype(BF16) for ch in chains]
    ys = [_dot((ch['r'].astype(F32) + y[:, :HEAD_DIM]).astype(BF16), mm) + y[:, HEAD_DIM:] + _dot(g, ch['v'])
          for ch, y, mm, g in zip(chains, ry, m, g_rk)]
    pq = [_dot_tn(ch['b_end'], x) for ch, x in zip(chains, aw)]
    kv = [_dot_tn(ch['k_end'], ch['v']) for ch in chains]
    m_new = [_dot((p[:, :HEAD_DIM] + jnp.where(hi == hj, ch['g_total'], 0.0)).astype(BF16), mm) + p[:, HEAD_DIM:] + q
             for ch, p, q, mm in zip(chains, pq, kv, m)]
    return ys, m_new


def _rwkv_kernel(rkvf_ref, lof_ref, rkvb_ref, lob_ref, m0_ref, w0_ref, w2_ref, a0_ref, a2_ref,
                 kk_ref, ka_ref, rk_ref, yf_ref, yb_ref, bvf_ref, bvb_ref, mend_ref, m_f, m_b):
    i = pl.program_id(1)
    n = pl.num_programs(1)

    @pl.when(i == 0)
    def _():
        m_f[...] = m0_ref[:, 0]
        m_b[...] = m0_ref[:, 1]

    w = B_WIDTH
    nh = w // HEAD_DIM
    nb, c = rkvf_ref.shape[:2]
    chains = []
    for d, (rkv_ref, lo_ref, bv_ref, m_ref) in enumerate(
            ((rkvf_ref, lof_ref, bvf_ref, m_f), (rkvb_ref, lob_ref, bvb_ref, m_b))):
        for b in range(nb):
            rkv = rkv_ref[b]
            lo = lo_ref[b]
            p = _rwkv_prep(rkv[:, :w], rkv[:, w:2 * w], rkv[:, 2 * w:], lo[:, :LORA], lo[:, LORA:2 * LORA],
                           w0_ref[d:d + 1], w2_ref[d], a0_ref[d:d + 1], a2_ref[d],
                           kk_ref[...], ka_ref[...], rk_ref[...], rev=bool(d))
            bv_ref[b] = p['bonus_v']
            for h in range(nh):
                sl = slice(h * HEAD_DIM, (h + 1) * HEAD_DIM)
                ch = {key: p[key][:, sl] for key in ('a', 'b', 'k', 'r', 'v', 'b_end', 'k_end', 'g_total')}
                ch.update(earlier=p['earlier'], incl=p['incl'], m=m_ref[b, h])
                chains.append(ch)
    ys, m_new = _rwkv_solve(chains, c)
    for d, (y_ref, m_ref) in enumerate(((yf_ref, m_f), (yb_ref, m_b))):
        for b in range(nb):
            first = (d * nb + b) * nh
            y_ref[b] = jnp.concatenate(ys[first:first + nh], axis=1)
            for h in range(nh):
                m_ref[b, h] = m_new[first + h]

    @pl.when(i == n - 1)
    def _():
        mend_ref[:, 0] = m_f[...]
        mend_ref[:, 1] = m_b[...]


def _rwkv_scan(brkv, blo, m0, w0, w2, a0, a2, k_k, k_a, r_k):
    bsz, t, _ = brkv.shape
    c = RWKV_CHUNK
    nt = t // c
    w = B_WIDTH
    nb = RWKV_BATCH if bsz % RWKV_BATCH == 0 else 1
    full = lambda z: pl.BlockSpec(z.shape, lambda b, i: (0,) * z.ndim)
    fwd = lambda width: pl.BlockSpec((nb, c, width), lambda b, i: (b, i, 0))
    bwd = lambda width: pl.BlockSpec((nb, c, width), lambda b, i: (b, nt - 1 - i, 0))
    nh = w // HEAD_DIM
    st = pl.BlockSpec((nb, 2, nh, HEAD_DIM, HEAD_DIM), lambda b, i: (b, 0, 0, 0, 0))
    o = jax.ShapeDtypeStruct((bsz, t, w), F32)
    return pl.pallas_call(
        _rwkv_kernel,
        grid=(bsz // nb, nt),
        in_specs=[fwd(3 * w), fwd(256), bwd(3 * w), bwd(256), st,
                  full(w0), full(w2), full(a0), full(a2), full(k_k), full(k_a), full(r_k)],
        out_specs=[fwd(w), bwd(w), fwd(w), bwd(w), st],
        out_shape=[o, o, o, o, jax.ShapeDtypeStruct((bsz, 2, nh, HEAD_DIM, HEAD_DIM), F32)],
        scratch_shapes=[pltpu.VMEM((nb, nh, HEAD_DIM, HEAD_DIM), F32), pltpu.VMEM((nb, nh, HEAD_DIM, HEAD_DIM), F32)],
        compiler_params=_params("arbitrary", "arbitrary"),
        name="rwkv_scan",
    )(brkv, blo, brkv, blo, m0, w0, w2, a0, a2, k_k, k_a, r_k)


def _na_kernel(q_ref, k_ref, v_ref, kc_ref, vc_ref, bias_ref, o_ref):
    r = pl.program_id(1)
    rows = pl.num_programs(1)
    kh = bias_ref.shape[3] // GRID_W
    rs = jnp.clip(r - kh // 2, 0, rows - kh)
    start = pl.multiple_of(rs * GRID_W, GRID_W)
    nb = q_ref.shape[0]
    nh = q_ref.shape[2] // HEAD_DIM
    q = [q_ref[b] for b in range(nb)]
    kb = [k_ref[b, pl.ds(start, kh * GRID_W), :] for b in range(nb)]
    vb = [v_ref[b, pl.ds(start, kh * GRID_W), :] for b in range(nb)]
    kc = [kc_ref[b] for b in range(nb)]
    vc = [vc_ref[b] for b in range(nb)]
    ch = [(b, h, slice(h * HEAD_DIM, (h + 1) * HEAD_DIM)) for b in range(nb) for h in range(nh)]
    s_w = [_dot_nt(q[b][:, sl], kb[b][:, sl]) + bias_ref[0, h] for b, h, sl in ch]
    s_c = [_dot_nt(q[b][:, sl], kc[b][:, sl]) for b, h, sl in ch]
    m = [jnp.maximum(jnp.max(x, axis=-1, keepdims=True), jnp.max(y, axis=-1, keepdims=True))
         for x, y in zip(s_w, s_c)]
    p_w = [jnp.exp(x - mm) for x, mm in zip(s_w, m)]
    p_c = [jnp.exp(y - mm) for y, mm in zip(s_c, m)]
    den = [jnp.sum(x, axis=-1, keepdims=True) + jnp.sum(y, axis=-1, keepdims=True) for x, y in zip(p_w, p_c)]
    outs = [(_dot(x.astype(BF16), vb[b][:, sl]) + _dot(y.astype(BF16), vc[b][:, sl])) / d
            for x, y, d, (b, h, sl) in zip(p_w, p_c, den, ch)]
    for b in range(nb):
        o_ref[b] = jnp.concatenate(outs[b * nh:(b + 1) * nh], axis=1)


def _na_bias_table(rpb, rows):
    kh = min(WIN_H, rows)
    nh, nr, ncol = rpb.shape
    qcol = np.arange(GRID_W)[:, None]
    kcol = np.arange(GRID_W)[None, :]
    cstart = np.clip(qcol - WIN_W // 2, 0, GRID_W - WIN_W)
    valid = (kcol >= cstart) & (kcol < cstart + WIN_W)
    hot = ((kcol - qcol + (WIN_W - 1))[None] == np.arange(ncol)[:, None, None]) & valid[None]
    hot = jnp.asarray(hot.reshape(ncol, GRID_W * GRID_W), F32)
    toe = _dot(rpb.reshape(nh * nr, ncol), hot, HI).reshape(nh, nr, GRID_W, GRID_W)
    toe = jnp.where(jnp.asarray(valid)[None, None], toe, MASK_VALUE)
    tab = [jnp.concatenate([toe[:, i - var + (WIN_H - 1)] for i in range(kh)], axis=-1) for var in range(kh)]
    return jnp.stack(tab)


def _na_attention(q, k, v, kc, vc, bias_tab):
    bsz, seq, cw = q.shape
    ctx_len = kc.shape[1]
    rows = seq // GRID_W
    kh = bias_tab.shape[0]
    nh = bias_tab.shape[1]

    def bias_map(b, r):
        return (r - jnp.clip(r - kh // 2, 0, rows - kh), 0, 0, 0)

    nb = NA_BATCH if bsz % NA_BATCH == 0 else 1
    return pl.pallas_call(
        _na_kernel,
        grid=(bsz // nb, rows),
        in_specs=[pl.BlockSpec((nb, GRID_W, cw), lambda b, r: (b, r, 0)),
                  pl.BlockSpec((nb, seq, cw), lambda b, r: (b, 0, 0)),
                  pl.BlockSpec((nb, seq, cw), lambda b, r: (b, 0, 0)),
                  pl.BlockSpec((nb, ctx_len, cw), lambda b, r: (b, 0, 0)),
                  pl.BlockSpec((nb, ctx_len, cw), lambda b, r: (b, 0, 0)),
                  pl.BlockSpec((1, nh, GRID_W, kh * GRID_W), bias_map)],
        out_specs=pl.BlockSpec((nb, GRID_W, cw), lambda b, r: (b, r, 0)),
        out_shape=jax.ShapeDtypeStruct((bsz, seq, cw), F32),
        compiler_params=_params("arbitrary", "arbitrary"),
        name="na_attention",
    )(q, k, v, kc, vc, bias_tab)


def _ctx_attn_kernel(q_ref, k_ref, v_ref, o_ref):
    q = q_ref[0]
    k = k_ref[0]
    v = v_ref[0]
    outs = []
    for h in range(q.shape[1] // HEAD_DIM):
        sl = slice(h * HEAD_DIM, (h + 1) * HEAD_DIM)
        s = _dot_nt(q[:, sl], k[:, sl])
        p = jnp.exp(s - jnp.max(s, axis=-1, keepdims=True))
        outs.append(_dot(p.astype(BF16), v[:, sl]) / jnp.sum(p, axis=-1, keepdims=True))
    o_ref[0] = jnp.concatenate(outs, axis=1)


def _ctx_attention(q, k, v):
    bsz, n, cw = q.shape
    spec = pl.BlockSpec((1, n, cw), lambda b: (b, 0, 0))
    return pl.pallas_call(
        _ctx_attn_kernel, grid=(bsz,), in_specs=[spec, spec, spec], out_specs=spec,
        out_shape=jax.ShapeDtypeStruct((bsz, n, cw), F32),
        compiler_params=_params("arbitrary"), name="ctx_attention",
    )(q, k, v)


def _outproj_kernel(x_ref, hf_ref, hb_ref, ga_ref, yf_ref, yb_ref, bvf_ref, bvb_ref, lo_ref, yc_ref,
                    wo_ref, g2_ref, lng_ref, lnb_ref, gate_ref, g_ref, sh_ref, sc_ref, rw_ref, rb_ref,
                    xo_ref, h_ref, lg_ref):
    ya = jax.nn.gelu(ga_ref[0]) * (hf_ref[0] + hb_ref[0])

    y = yf_ref[0] + yb_ref[0]
    avg = (_head_ones(B_WIDTH).astype(F32) * (1.0 / HEAD_DIM)).astype(BF16)
    mean = _dot_split(y, avg, 3)
    yc_ = y - mean
    var = _dot_split(yc_ * yc_, avg, 2)
    yn = yc_ * lax.rsqrt(var + GN_EPS) * lng_ref[...] + lnb_ref[...] + bvf_ref[0] + bvb_ref[0]
    gate_b = _dot_split(jax.nn.sigmoid(lo_ref[0][:, 2 * LORA:3 * LORA]), g2_ref[...], 2)
    yb = yn * gate_b

    mix = (_dot(ya.astype(BF16), wo_ref[:A_WIDTH]) + _dot(yb.astype(BF16), wo_ref[A_WIDTH:A_WIDTH + B_WIDTH])
           + _dot(yc_ref[0].astype(BF16), wo_ref[A_WIDTH + B_WIDTH:]))
    x = x_ref[0] + gate_ref[0] * mix
    xo_ref[0] = x
    hn = x * lax.rsqrt(jnp.mean(x * x, axis=-1, keepdims=True) + RMS_EPS) * g_ref[...]
    hn = hn * (1.0 + sc_ref[0]) + sh_ref[0]
    h_ref[0] = hn.reshape(h_ref.shape[1:])
    lg_ref[0] = _dot_split(hn, rw_ref[...], 2) + rb_ref[...]


def _outproj(x, hf, hb, ga, yf, yb, bvf, bvb, blo, yc, wo, g2, lng, lnb, gate, g, shift, scale, rw, rb, tm):
    bsz, t, d = x.shape
    ne = rw.shape[1]
    blk = lambda w: pl.BlockSpec((1, tm, w), lambda b, i: (b, i, 0))
    full = lambda a: pl.BlockSpec(a.shape, lambda b, i: (0,) * a.ndim)
    vec = pl.BlockSpec((1, 1, d), lambda b, i: (b, 0, 0))
    return pl.pallas_call(
        _outproj_kernel,
        grid=(bsz, t // tm),
        in_specs=[blk(d), blk(A_WIDTH), blk(A_WIDTH), blk(A_WIDTH), blk(B_WIDTH), blk(B_WIDTH),
                  blk(B_WIDTH), blk(B_WIDTH), blk(256), blk(C_WIDTH),
                  full(wo), full(g2), full(lng), full(lnb), vec, full(g), vec, vec, full(rw), full(rb)],
        out_specs=[blk(d), pl.BlockSpec((1, tm) + _row_tile(d), lambda b, i: (b, i, 0, 0)), blk(ne)],
        out_shape=[jax.ShapeDtypeStruct((bsz, t, d), F32), jax.ShapeDtypeStruct((bsz, t) + _row_tile(d), F32),
                   jax.ShapeDtypeStruct((bsz, t, ne), F32)],
        compiler_params=_params("arbitrary", "arbitrary"),
        name="outproj",
    )(x, hf, hb, ga, yf, yb, bvf, bvb, blo, yc, wo, g2, lng, lnb, gate, g, shift, scale, rw, rb)


def _expert_kernel(be_ref, nb_ref, x_ref, wgu_ref, bgu_ref, wdn_ref, bdn_ref, o_ref, wgu_s, wdn_s):
    i = pl.program_id(0)
    e = be_ref[i]
    fresh = jnp.logical_or(i == 0, be_ref[jnp.maximum(i - 1, 0)] != e)

    @pl.when(fresh)
    def _():
        wgu_s[...] = wgu_ref[0].astype(BF16)
        wdn_s[...] = wdn_ref[0].astype(BF16)

    @pl.when(i < nb_ref[0])
    def _():
        dff, d = wdn_s.shape
        x = x_ref[...].reshape(x_ref.shape[0], d)
        gu = _dot(x.astype(BF16), wgu_s[...]) + bgu_ref[0]
        g_ = jnp.minimum(gu[:, :dff], SWIGLU_LIMIT)
        u_ = jnp.clip(gu[:, dff:], -SWIGLU_LIMIT, SWIGLU_LIMIT)
        act = (u_ + 1.0) * (g_ * jax.nn.sigmoid(SWIGLU_ALPHA * g_))
        o_ref[...] = (_dot(act.astype(BF16), wdn_s[...]) + bdn_ref[0]).reshape(o_ref.shape)

    @pl.when(i >= nb_ref[0])
    def _():
        o_ref[...] = jnp.zeros_like(o_ref)


def _experts(block_e, n_used, xb, w_gu, b_gu, w_dn, b_dn, layer):
    cap = xb.shape[0]
    row = xb.shape[1:]
    depth, ne, d, f2 = w_gu.shape
    dff = w_dn.shape[2]
    nb = cap // MOE_BM
    w_gu, b_gu, w_dn, b_dn = (z.reshape((depth * ne,) + s) for z, s in
                              ((w_gu, (d, f2)), (b_gu, (1, f2)), (w_dn, (dff, d)), (b_dn, (1, d))))
    e0 = layer * ne
    grid_spec = pltpu.PrefetchScalarGridSpec(
        num_scalar_prefetch=2,
        grid=(nb,),
        in_specs=[pl.BlockSpec((MOE_BM,) + row, lambda i, be, nu: (i, 0, 0)),
                  pl.BlockSpec((1, d, f2), lambda i, be, nu: (e0 + be[i], 0, 0)),
                  pl.BlockSpec((1, 1, f2), lambda i, be, nu: (e0 + be[i], 0, 0)),
                  pl.BlockSpec((1, dff, d), lambda i, be, nu: (e0 + be[i], 0, 0)),
                  pl.BlockSpec((1, 1, d), lambda i, be, nu: (e0 + be[i], 0, 0))],
        out_specs=pl.BlockSpec((MOE_BM,) + row, lambda i, be, nu: (i, 0, 0)),
        scratch_shapes=[pltpu.VMEM((d, f2), BF16), pltpu.VMEM((dff, d), BF16)],
    )
    return pl.pallas_call(
        _expert_kernel, grid_spec=grid_spec,
        out_shape=jax.ShapeDtypeStruct((cap,) + row, F32),
        compiler_params=_params("arbitrary"), name="experts",
    )(block_e, n_used, xb, w_gu, b_gu, w_dn, b_dn)


def _route_kernel(lg_ref, idx_ref, gate_ref, rank_ref, cnt_ref, carry):
    i = pl.program_id(0)

    @pl.when(i == 0)
    def _():
        carry[...] = jnp.zeros_like(carry)

    l = lg_ref[...]
    tm, ne = l.shape
    lane = lax.broadcasted_iota(jnp.int32, l.shape, 1)
    vals, idxs, hots = [], [], []
    for _ in range(TOP_K):
        m = jnp.max(l, axis=-1, keepdims=True)
        ix = jnp.min(jnp.where(l == m, lane, ne), axis=-1, keepdims=True)
        hot = lane == ix
        vals.append(m)
        idxs.append(ix)
        hots.append(hot)
        l = jnp.where(hot, -jnp.inf, l)
    ex = [jnp.exp(v - vals[0]) for v in vals]
    den = ex[0] + ex[1] + ex[2] + ex[3]
    hot_all = (hots[0] | hots[1] | hots[2] | hots[3]).astype(BF16)
    tr = lax.broadcasted_iota(jnp.int32, (tm, tm), 0)
    tc = lax.broadcasted_iota(jnp.int32, (tm, tm), 1)
    before = _dot((tc < tr).astype(BF16), hot_all) + carry[...]
    ranks = [jnp.sum(jnp.where(h, before, 0.0), axis=-1, keepdims=True) for h in hots]
    carry[...] = carry[...] + jnp.sum(hot_all.astype(F32), axis=0, keepdims=True)
    idx_ref[...] = jnp.concatenate(idxs, axis=1)
    gate_ref[...] = jnp.concatenate([e / den for e in ex], axis=1)
    rank_ref[...] = jnp.concatenate(ranks, axis=1).astype(jnp.int32)
    cnt_ref[...] = carry[...].astype(jnp.int32)


def _route(logits):
    n, ne = logits.shape
    tm = ROUTE_TM
    assert n % tm == 0, (n, tm)
    blk = pl.BlockSpec((tm, TOP_K), lambda i: (i, 0))
    return pl.pallas_call(
        _route_kernel,
        grid=(n // tm,),
        in_specs=[pl.BlockSpec((tm, ne), lambda i: (i, 0))],
        out_specs=[blk, blk, blk, pl.BlockSpec((1, ne), lambda i: (0, 0))],
        out_shape=[jax.ShapeDtypeStruct((n, TOP_K), jnp.int32), jax.ShapeDtypeStruct((n, TOP_K), F32),
                   jax.ShapeDtypeStruct((n, TOP_K), jnp.int32), jax.ShapeDtypeStruct((1, ne), jnp.int32)],
        scratch_shapes=[pltpu.VMEM((1, ne), F32)],
        compiler_params=_params("arbitrary"),
        name="route",
    )(logits)


def _dispatch_kernel(dest_hbm, h_ref, xb_init, xb_hbm, dest_s, sem_idx, sem):
    del xb_init
    i = pl.program_id(0)
    tm = h_ref.shape[0]
    cp = pltpu.make_async_copy(dest_hbm.at[pl.ds(i * tm * TOP_K, tm * TOP_K)], dest_s, sem_idx)
    cp.start()
    cp.wait()

    def issue(t, carry):
        src = h_ref.at[t]
        for j in range(TOP_K):
            pltpu.make_async_copy(src, xb_hbm.at[dest_s[t * TOP_K + j]], sem).start()
        return carry

    lax.fori_loop(0, tm, issue, 0, unroll=8)
    for _ in range(TOP_K):
        pltpu.make_async_copy(h_ref, xb_hbm.at[pl.ds(0, tm)], sem).wait()


def _dispatch(dest_flat, h, xb_init):
    n = h.shape[0]
    row = h.shape[1:]
    cap = xb_init.shape[0]
    tm = DISPATCH_TM
    assert n % tm == 0, (n, tm)
    any_spec = pl.BlockSpec(memory_space=pl.ANY)
    return pl.pallas_call(
        _dispatch_kernel,
        grid=(n // tm,),
        in_specs=[any_spec, pl.BlockSpec((tm,) + row, lambda i: (i, 0, 0)), any_spec],
        out_specs=any_spec,
        out_shape=jax.ShapeDtypeStruct((cap,) + row, F32),
        scratch_shapes=[pltpu.SMEM((tm * TOP_K,), jnp.int32), pltpu.SemaphoreType.DMA, pltpu.SemaphoreType.DMA],
        input_output_aliases={2: 0},
        compiler_params=pltpu.CompilerParams(dimension_semantics=("arbitrary",), has_side_effects=True),
        name="dispatch",
    )(dest_flat, h, xb_init)


def _combine_kernel(dest_hbm, yb_hbm, x_ref, gt_ref, mg_ref, g_ref, o_ref, dest_s, buf, sem_idx, sem,
                    *, tok0, final):
    i = pl.program_id(0)
    n = pl.num_programs(0)
    tm = x_ref.shape[0]

    def fetch(blk, slot):
        cp = pltpu.make_async_copy(dest_hbm.at[pl.ds((tok0 + blk * tm) * TOP_K, tm * TOP_K)], dest_s.at[slot],
                                   sem_idx)
        cp.start()
        cp.wait()

        def issue(t, carry):
            for j in range(TOP_K):
                pltpu.make_async_copy(yb_hbm.at[dest_s[slot, t * TOP_K + j]], buf.at[slot, j, t],
                                      sem.at[slot]).start()
            return carry

        lax.fori_loop(0, tm, issue, 0, unroll=8)

    @pl.when(i == 0)
    def _():
        fetch(0, 0)

    @pl.when(i + 1 < n)
    def _():
        fetch(i + 1, (i + 1) % 2)

    slot = i % 2
    for j in range(TOP_K):
        pltpu.make_async_copy(yb_hbm.at[pl.ds(0, tm)], buf.at[slot, j], sem.at[slot]).wait()
    gt = gt_ref[...]
    y = gt[:, 0:1] * buf[slot, 0].reshape(x_ref.shape)
    for j in range(1, TOP_K):
        y = y + gt[:, j:j + 1] * buf[slot, j].reshape(x_ref.shape)
    x = x_ref[...] + mg_ref[0] * y
    if final:
        x = x * lax.rsqrt(jnp.mean(x * x, axis=-1, keepdims=True) + RMS_EPS) * g_ref[...]
    o_ref[...] = x


def _combine(dest_flat, yb, x, gates, mod_gate, g, tok0, final):
    bsz, t, d = x.shape
    n = bsz * t
    tm = min(COMBINE_TM, t)
    per_b = t // tm
    any_spec = pl.BlockSpec(memory_space=pl.ANY)
    out = pl.pallas_call(
        functools.partial(_combine_kernel, tok0=tok0, final=final),
        grid=(n // tm,),
        in_specs=[any_spec, any_spec,
                  pl.BlockSpec((tm, d), lambda i: (i, 0)),
                  pl.BlockSpec((tm, TOP_K), lambda i: (i, 0)),
                  pl.BlockSpec((1, 1, d), lambda i: (i // per_b, 0, 0)),
                  pl.BlockSpec((1, d), lambda i: (0, 0))],
        out_specs=pl.BlockSpec((tm, d), lambda i: (i, 0)),
        out_shape=jax.ShapeDtypeStruct((n, d), F32),
        scratch_shapes=[pltpu.SMEM((2, tm * TOP_K), jnp.int32), pltpu.VMEM((2, TOP_K, tm) + _row_tile(d), F32),
                        pltpu.SemaphoreType.DMA, pltpu.SemaphoreType.DMA((2,))],
        compiler_params=_params("arbitrary"),
        name="combine",
    )(dest_flat, yb, x.reshape(n, d), gates, mod_gate, g)
    return out.reshape(bsz, t, d)


def _moe_blocks(n_tokens, ne):
    return (n_tokens * TOP_K + ne * (MOE_BM - 1) + MOE_BM - 1) // MOE_BM


def _moe_plan(logits, nb):
    n, ne = logits.shape
    assert nb >= _moe_blocks(n, ne), (nb, n)
    idx, gates, rank, counts = _route(logits)
    counts = counts[0]
    padded = (counts + MOE_BM - 1) // MOE_BM * MOE_BM
    pad_end = jnp.cumsum(padded)
    pad_start = pad_end - padded
    hot = idx[:, :, None] == jnp.arange(ne, dtype=jnp.int32)[None, None, :]
    dest = jnp.sum(jnp.where(hot, pad_start[None, None, :], 0), axis=-1) + rank
    block_start = jnp.arange(nb, dtype=jnp.int32) * MOE_BM
    block_e = jnp.minimum(jnp.sum(pad_end[None, :] <= block_start[:, None], axis=1), ne - 1).astype(jnp.int32)
    n_used = (pad_end[-1] // MOE_BM).astype(jnp.int32).reshape(1)
    return dest.reshape(-1).astype(jnp.int32), gates, block_e, n_used


def _block_diag(w):
    nh, n, _ = w.shape
    eye = jnp.eye(nh, dtype=w.dtype)
    return (w[:, :, None, :] * eye[:, None, :, None]).reshape(nh * n, nh * n)


def _permute_in_cols(w_in):
    a0 = 0
    b0 = 2 * A_WIDTH
    c0 = b0 + 3 * B_WIDTH + 3 * LORA
    d = w_in.shape[0]
    pad = jnp.zeros((d, 256 - 3 * LORA), w_in.dtype)
    return jnp.concatenate([w_in[:, a0:a0 + A_WIDTH], w_in[:, b0:b0 + 3 * B_WIDTH],
                            w_in[:, b0 + 3 * B_WIDTH:c0], pad,
                            w_in[:, A_WIDTH:2 * A_WIDTH], w_in[:, c0:]], axis=1)


def _mixers(p_ctx, p_lat, lp, need_ctx):
    (xcf_c, xcb_c, ga_c, brkv_c, blo_c, q_c, k_c, v_c) = p_ctx
    (xcf_l, xcb_l, ga_l, brkv_l, blo_l, q_l, k_l, v_l) = p_lat
    bsz = xcf_c.shape[0]

    lru_w = (lp['wr_bd'], lp['br'], lp['wi_bd'], lp['bi'], lp['lam'])
    hf_c, hb_c, hend = _lru_scan(xcf_c, xcb_c, jnp.zeros((bsz, 2, A_WIDTH), F32), *lru_w)
    hf_l, hb_l, _ = _lru_scan(xcf_l, xcb_l, hend, *lru_w)

    rw_w = (lp['w0'], lp['w2'], lp['a0'], lp['a2'], lp['k_k'], lp['k_a'], lp['r_k'])
    yf_c, yb_c, bvf_c, bvb_c, mend = _rwkv_scan(brkv_c, blo_c, jnp.zeros((bsz, 2, B_WIDTH // HEAD_DIM, HEAD_DIM, HEAD_DIM), F32), *rw_w)
    yf_l, yb_l, bvf_l, bvb_l, _ = _rwkv_scan(brkv_l, blo_l, mend, *rw_w)

    yc_l = _na_attention(q_l, k_l, v_l, k_c, v_c, lp['bias_tab'])
    lat = (hf_l, hb_l, ga_l, yf_l, yb_l, bvf_l, bvb_l, blo_l, yc_l)
    ctx = None
    if need_ctx:
        yc_c = _ctx_attention(q_c, k_c, v_c)
        ctx = (hf_c, hb_c, ga_c, yf_c, yb_c, bvf_c, bvb_c, blo_c, yc_c)
    return ctx, lat


def kernel(x, c, ctx, c_ctx, ada_w, ada_b, norm_mix_g, norm_ffn_g, w_in, w_out, lru_conv_w, lru_conv_b, lru_wr, lru_br, lru_wi, lru_bi, lru_lambda, rwkv_mu, rwkv_w0, rwkv_w2, rwkv_a0, rwkv_a2, rwkv_g2, rwkv_kk, rwkv_ka, rwkv_rk, rwkv_lnx_g, rwkv_lnx_b, na_rpb, router_w, router_b, moe_w_gu, moe_b_gu, moe_w_dn, moe_b_dn, final_g):
    bsz, seq, d = x.shape
    ctx_len = ctx.shape[1]
    depth = ada_w.shape[0]
    rows = seq // GRID_W
    tm_l = 256
    tm_c = min(256, ctx_len)

    cond = jnp.concatenate([c, c_ctx[None], jnp.zeros((8 - bsz - 1, d), F32)], axis=0)
    mod = _adaln(cond, ada_w, ada_b)

    moe_nb = _moe_blocks(bsz * (ctx_len + seq), router_w.shape[2])
    xb = jnp.zeros((moe_nb * MOE_BM,) + _row_tile(d), F32)
    xl, xc = x, ctx
    for l in range(depth):
        last = l == depth - 1
        ml = [mod[l, :bsz, j * d:(j + 1) * d][:, None, :] for j in range(6)]
        mc = [jnp.broadcast_to(mod[l, bsz:bsz + 1, j * d:(j + 1) * d][:, None, :], (bsz, 1, d)) for j in range(6)]
        n_b = 3 * B_WIDTH
        mu = rwkv_mu[l]
        lp = dict(
            wr_bd=jnp.stack([_block_diag(lru_wr[l, dd]) for dd in range(2)]),
            wi_bd=jnp.stack([_block_diag(lru_wi[l, dd]) for dd in range(2)]),
            br=lru_br[l], bi=lru_bi[l], lam=lru_lambda[l],
            w0=rwkv_w0[l], w2=rwkv_w2[l], a0=rwkv_a0[l], a2=rwkv_a2[l],
            k_k=rwkv_kk[l][None], k_a=rwkv_ka[l][None], r_k=rwkv_rk[l].reshape(1, B_WIDTH),
            bias_tab=_na_bias_table(na_rpb[l], rows),
        )
        w_perm = _permute_in_cols(w_in[l]).astype(BF16)
        mu_perm = jnp.concatenate([mu[:, :n_b], mu[:, n_b:], jnp.zeros((2, 256 - 3 * LORA), F32)], axis=1)
        g_mix = norm_mix_g[l][None]
        g_ffn = norm_ffn_g[l][None]
        wo = w_out[l].astype(BF16)

        p_lat = _inproj(xl, g_mix, ml[0], ml[1], w_perm, lru_conv_w[l], lru_conv_b[l], mu_perm, tm_l)
        p_ctx = _inproj(xc, g_mix, mc[0], mc[1], w_perm, lru_conv_w[l], lru_conv_b[l], mu_perm, tm_c)
        mix_c, mix_l = _mixers(p_ctx, p_lat, lp, not last)

        fin = (wo, rwkv_g2[l], rwkv_lnx_g[l][None], rwkv_lnx_b[l][None])
        rt = (router_w[l], router_b[l][None])
        xl, hl, lg_l = _outproj(xl, *mix_l, *fin, ml[2], g_ffn, ml[3], ml[4], *rt, tm_l)
        ne = router_w.shape[2]
        if last:
            tok = hl.reshape((-1,) + _row_tile(d))
            lg = lg_l.reshape(-1, ne)
            nc = 0
        else:
            xc, hc, lg_c = _outproj(xc, *mix_c, *fin, mc[2], g_ffn, mc[3], mc[4], *rt, tm_c)
            tok = jnp.concatenate([hc.reshape((-1,) + _row_tile(d)), hl.reshape((-1,) + _row_tile(d))], axis=0)
            lg = jnp.concatenate([lg_c.reshape(-1, ne), lg_l.reshape(-1, ne)], axis=0)
            nc = bsz * ctx_len
        dest, gates, block_e, n_used = _moe_plan(lg, moe_nb)
        xb = _dispatch(dest, tok, xb)
        yb = _experts(block_e, n_used, xb, moe_w_gu, moe_b_gu, moe_w_dn, moe_b_dn, l)
        if not last:
            xc = _combine(dest, yb, xc, gates[:nc], mc[5], final_g[None], 0, False)
        xl = _combine(dest, yb, xl, gates[nc:], ml[5], final_g[None], nc, last)
    return xl
```

```python
import functools

import numpy as np
import jax
import jax.numpy as jnp
from jax import lax
from jax.experimental import pallas as pl
from jax.experimental.pallas import tpu as pltpu

F32 = jnp.float32
BF16 = jnp.bfloat16
HI = lax.Precision.HIGHEST

RMS_EPS = 1e-6
GN_EPS = 64e-5
LRU_C = 8.0
SWIGLU_ALPHA = 1.702
SWIGLU_LIMIT = 7.0
TOP_K = 4
GRID_W = 64
WIN_H = 8
WIN_W = 16
HEAD_DIM = 64
A_WIDTH = 256
B_WIDTH = 256
C_WIDTH = 512
LORA = 64
LANES = 128
HALO = 8
RWKV_CHUNK = 64
RWKV_BATCH = 4
NA_BATCH = 2
LRU_CHUNK = 256
MOE_BM = 256
ROUTE_TM = 512
DISPATCH_TM = 512
COMBINE_TM = 256
MASK_VALUE = -1e30
VMEM_LIMIT = 56 * 1024 * 1024


def _dot(a, b, prec=None):
    return jnp.dot(a, b, preferred_element_type=F32, precision=prec)


def _dot_nt(a, b, prec=None):
    return lax.dot_general(a, b, (((1,), (1,)), ((), ())), preferred_element_type=F32, precision=prec)


def _dot_tn(a, b, prec=None):
    return lax.dot_general(a, b, (((0,), (0,)), ((), ())), preferred_element_type=F32, precision=prec)


def _split(x, n):
    terms = []
    for _ in range(n):
        t = x.astype(BF16)
        terms.append(t)
        x = x - t.astype(F32)
    return terms


def _dot_split(a, b, order):
    a_t = [a] if a.dtype == BF16 else _split(a, order)
    b_t = [b] if b.dtype == BF16 else _split(b, order)
    return _dot_pieces(a_t, b_t, order)


def _dot_pieces(a_t, b_t, order):
    out = None
    for i, x in enumerate(a_t):
        for j, y in enumerate(b_t):
            if i + j < order:
                p = _dot(x, y)
                out = p if out is None else out + p
    return out


def _softplus(z):
    return jnp.maximum(z, 0.0) + jnp.log1p(jnp.exp(-jnp.abs(z)))


def _params(*sem):
    return pltpu.CompilerParams(dimension_semantics=sem, vmem_limit_bytes=VMEM_LIMIT)


def _row_tile(d):
    assert d % LANES == 0, d
    return (d // LANES, LANES)


def _head_ones(width):
    r = lax.broadcasted_iota(jnp.int32, (width, width), 0) // HEAD_DIM
    c = lax.broadcasted_iota(jnp.int32, (width, width), 1) // HEAD_DIM
    return r == c


def _adaln_kernel(cond_ref, w_ref, b_ref, o_ref):
    c = cond_ref[...]
    o_ref[0] = _dot(c * jax.nn.sigmoid(c), w_ref[0], HI) + b_ref[0]


def _adaln(cond, ada_w, ada_b):
    depth, d, n = ada_w.shape
    tn = 1536
    return pl.pallas_call(
        _adaln_kernel,
        grid=(depth, n // tn),
        in_specs=[pl.BlockSpec((8, d), lambda l, j: (0, 0)),
                  pl.BlockSpec((1, d, tn), lambda l, j: (l, 0, j)),
                  pl.BlockSpec((1, 1, tn), lambda l, j: (l, 0, j))],
        out_specs=pl.BlockSpec((1, 8, tn), lambda l, j: (l, 0, j)),
        out_shape=jax.ShapeDtypeStruct((depth, 8, n), F32),
        compiler_params=_params("arbitrary", "arbitrary"),
        name="adaln",
    )(cond, ada_w, ada_b.reshape(depth, 1, n))


N_SHIFT = 1280
N_PROJ = 3072


def _inproj_kernel(x_ref, xp_ref, xn_ref, g_ref, sh_ref, sc_ref, w_ref, cw_ref, cb_ref, mu_ref,
                   xcf_ref, xcb_ref, ga_ref, brkv_ref, blo_ref, q_ref, k_ref, v_ref):
    i = pl.program_id(1)
    n = pl.num_programs(1)
    tm = x_ref.shape[1]
    g = g_ref[...]
    sh = sh_ref[0]
    sc = sc_ref[0]

    def norm_mod(x):
        y = x * lax.rsqrt(jnp.mean(x * x, axis=-1, keepdims=True) + RMS_EPS) * g
        return y * (1.0 + sc) + sh

    p = _dot(norm_mod(x_ref[0]).astype(BF16), w_ref[...])
    halo = jnp.concatenate([xp_ref[0], xn_ref[0]], axis=0)
    ph = _dot(norm_mod(halo).astype(BF16), w_ref[:, :N_SHIFT])
    p_prev = jnp.where(i > 0, ph[:HALO], 0.0)
    p_next = jnp.where(i < n - 1, ph[HALO:], 0.0)
    ext = jnp.concatenate([p_prev, p[:, :N_SHIFT], p_next], axis=0)

    xa = ext[:, :A_WIDTH]
    cw = cw_ref[...]
    cb = cb_ref[...]
    xcf = cb[0:1]
    xcb = cb[1:2]
    for j in range(cw.shape[1]):
        xcf = xcf + cw[0, j:j + 1] * xa[HALO - j:HALO - j + tm]
        xcb = xcb + cw[1, j:j + 1] * xa[HALO + j:HALO + j + tm]
    xcf_ref[0] = xcf
    xcb_ref[0] = xcb

    pb = ext[HALO:HALO + tm, A_WIDTH:]
    prev = ext[HALO - 1:HALO - 1 + tm, A_WIDTH:]
    nxt = ext[HALO + 1:HALO + 1 + tm, A_WIDTH:]
    mu = mu_ref[...]
    sb = pb + mu[0:1] * (prev - pb) + mu[1:2] * (nxt - pb)
    brkv_ref[0] = sb[:, :3 * B_WIDTH]
    blo_ref[0] = sb[:, 3 * B_WIDTH:]

    o = N_SHIFT
    ga_ref[0] = p[:, o:o + A_WIDTH]
    o += A_WIDTH
    q_ref[0] = (p[:, o:o + C_WIDTH] * (HEAD_DIM ** -0.5)).astype(BF16)
    k_ref[0] = p[:, o + C_WIDTH:o + 2 * C_WIDTH].astype(BF16)
    v_ref[0] = p[:, o + 2 * C_WIDTH:o + 3 * C_WIDTH].astype(BF16)


def _inproj(x, g, shift, scale, w_perm, conv_w, conv_b, mu_perm, tm):
    bsz, t, d = x.shape
    nt = t // tm
    hb = tm // HALO
    last = t // HALO - 1
    f = lambda shape, dt=F32: jax.ShapeDtypeStruct(shape, dt)
    blk = lambda w: pl.BlockSpec((1, tm, w), lambda b, i: (b, i, 0))
    full = lambda a: pl.BlockSpec(a.shape, lambda b, i: (0,) * a.ndim)
    return pl.pallas_call(
        _inproj_kernel,
        grid=(bsz, nt),
        in_specs=[blk(d),
                  pl.BlockSpec((1, HALO, d), lambda b, i: (b, jnp.maximum(i * hb - 1, 0), 0)),
                  pl.BlockSpec((1, HALO, d), lambda b, i: (b, jnp.minimum((i + 1) * hb, last), 0)),
                  full(g),
                  pl.BlockSpec((1, 1, d), lambda b, i: (b, 0, 0)),
                  pl.BlockSpec((1, 1, d), lambda b, i: (b, 0, 0)),
                  full(w_perm), full(conv_w), full(conv_b), full(mu_perm)],
        out_specs=[blk(A_WIDTH), blk(A_WIDTH), blk(A_WIDTH), blk(3 * B_WIDTH), blk(256),
                   blk(C_WIDTH), blk(C_WIDTH), blk(C_WIDTH)],
        out_shape=[f((bsz, t, A_WIDTH)), f((bsz, t, A_WIDTH)), f((bsz, t, A_WIDTH)),
                   f((bsz, t, 3 * B_WIDTH)), f((bsz, t, 256)),
                   f((bsz, t, C_WIDTH), BF16), f((bsz, t, C_WIDTH), BF16), f((bsz, t, C_WIDTH), BF16)],
        compiler_params=_params("arbitrary", "arbitrary"),
        name="inproj",
    )(x, x, x, g, shift, scale, w_perm, conv_w, conv_b, mu_perm)


def _chunk_scan(a, b, rev):
    n = a.shape[0]
    row = lax.broadcasted_iota(jnp.int32, a.shape, 0)
    s = 1
    while s < n:
        if rev:
            keep = row < n - s
            a_s = jnp.where(keep, pltpu.roll(a, n - s, 0), 1.0)
            b_s = jnp.where(keep, pltpu.roll(b, n - s, 0), 0.0)
        else:
            keep = row >= s
            a_s = jnp.where(keep, pltpu.roll(a, s, 0), 1.0)
            b_s = jnp.where(keep, pltpu.roll(b, s, 0), 0.0)
        b = a * b_s + b
        a = a * a_s
        s *= 2
    return a, b


def _lru_kernel(xf_ref, xb_ref, h0_ref, wr_ref, br_ref, wi_ref, bi_ref, lam_ref,
                hf_ref, hb_ref, hend_ref, carry):
    i = pl.program_id(1)
    n = pl.num_programs(1)
    ch = xf_ref.shape[1]

    @pl.when(i == 0)
    def _():
        carry[...] = h0_ref[0]

    for d, (x_ref, o_ref) in enumerate(((xf_ref, hf_ref), (xb_ref, hb_ref))):
        x = x_ref[0]
        x_t = _split(x, 2)
        gate_r = jax.nn.sigmoid(_dot_pieces(x_t, _split(wr_ref[d], 2), 2) + br_ref[d:d + 1])
        gate_i = jax.nn.sigmoid(_dot_pieces(x_t, _split(wi_ref[d], 2), 2) + bi_ref[d:d + 1])
        log_a = -LRU_C * gate_r * _softplus(-lam_ref[d:d + 1])
        a = jnp.exp(log_a)
        b = jnp.sqrt(1.0 - jnp.exp(2.0 * log_a)) * (gate_i * x)
        a_cum, h = _chunk_scan(a, b, rev=bool(d))
        h = h + a_cum * carry[d:d + 1]
        o_ref[0] = h
        carry[d:d + 1] = h[0:1] if d else h[ch - 1:ch]

    @pl.when(i == n - 1)
    def _():
        hend_ref[0] = carry[...]


def _lru_scan(xcf, xcb, h0, wr_bd, br, wi_bd, bi, lam):
    bsz, t, a = xcf.shape
    ch = min(LRU_CHUNK, t)
    nt = t // ch
    full = lambda z: pl.BlockSpec(z.shape, lambda b, i: (0,) * z.ndim)
    fwd = pl.BlockSpec((1, ch, a), lambda b, i: (b, i, 0))
    bwd = pl.BlockSpec((1, ch, a), lambda b, i: (b, nt - 1 - i, 0))
    st = pl.BlockSpec((1, 2, a), lambda b, i: (b, 0, 0))
    return pl.pallas_call(
        _lru_kernel,
        grid=(bsz, nt),
        in_specs=[fwd, bwd, st, full(wr_bd), full(br), full(wi_bd), full(bi), full(lam)],
        out_specs=[fwd, bwd, st],
        out_shape=[jax.ShapeDtypeStruct((bsz, t, a), F32), jax.ShapeDtypeStruct((bsz, t, a), F32),
                   jax.ShapeDtypeStruct((bsz, 2, a), F32)],
        scratch_shapes=[pltpu.VMEM((2, a), F32)],
        compiler_params=_params("arbitrary", "arbitrary"),
        name="lru_scan",
    )(xcf, xcb, h0, wr_bd, br, wi_bd, bi, lam)


def _rwkv_prep(r, k, v, wlo, alo, w0, w2, a0, a2, k_k, k_a, r_k, rev):
    c, width = r.shape
    ones_bd = _head_ones(width).astype(BF16)
    w_raw = -_softplus(-(w0 + _dot_split(jnp.tanh(wlo), w2, 2))) - 0.5
    logw = -jnp.exp(w_raw)
    a_gate = jax.nn.sigmoid(a0 + _dot_split(alo, a2, 2))
    kk = k * k_k
    kk = kk / jnp.maximum(jnp.sqrt(_dot_split(kk * kk, ones_bd, 2)), 1e-12)
    k_eff = k * (1.0 + (a_gate - 1.0) * k_a)
    b_vec = kk * a_gate

    tr = lax.broadcasted_iota(jnp.int32, (c, c), 0)
    tc = lax.broadcasted_iota(jnp.int32, (c, c), 1)
    tri = (tc >= tr) if rev else (tc <= tr)
    cum = _dot_split(tri.astype(BF16), logw, 3)
    total = cum[0:1] if rev else cum[c - 1:c]
    g_inv = jnp.exp(-cum)
    g_end = jnp.exp(total - cum)
    earlier = (tc > tr) if rev else (tc < tr)
    return dict(
        a=(-kk * jnp.exp(cum - logw)).astype(BF16), b=(b_vec * g_inv).astype(BF16),
        k=(k_eff * g_inv).astype(BF16), r=(r * jnp.exp(cum)).astype(BF16), v=v.astype(BF16),
        b_end=(b_vec * g_end).astype(BF16), k_end=(k_eff * g_end).astype(BF16), g_total=jnp.exp(total),
        earlier=earlier, incl=earlier | (tr == tc),
        bonus_v=_dot_split(r * k_eff * r_k, ones_bd, 2) * v)


def _rwkv_solve(chains, c):
    tr = lax.broadcasted_iota(jnp.int32, (c, c), 0)
    tc = lax.broadcasted_iota(jnp.int32, (c, c), 1)
    eye_c = (tr == tc).astype(F32)
    hi = lax.broadcasted_iota(jnp.int32, (HEAD_DIM, HEAD_DIM), 0)
    hj = lax.broadcasted_iota(jnp.int32, (HEAD_DIM, HEAD_DIM), 1)
    n_ab = [jnp.where(ch['earlier'], _dot_nt(ch['a'], ch['b']), 0.0) for ch in chains]
    g_ak = [jnp.where(ch['earlier'], _dot_nt(ch['a'], ch['k']), 0.0).astype(BF16) for ch in chains]
    g_rb = [jnp.where(ch['incl'], _dot_nt(ch['r'], ch['b']), 0.0).astype(BF16) for ch in chains]
    g_rk = [jnp.where(ch['incl'], _dot_nt(ch['r'], ch['k']), 0.0).astype(BF16) for ch in chains]
    gv = [_dot(g, ch['v']).astype(BF16) for g, ch in zip(g_ak, chains)]

    t_inv = [eye_c + n for n in n_ab]
    pw = [n.astype(BF16) for n in n_ab]
    s = 2
    while s < c:
        pw = [_dot(p, p).astype(BF16) for p in pw]
        t_inv = [t + _dot(t.astype(BF16), p) for t, p in zip(t_inv, pw)]
        s *= 2

    aw = [_dot(t.astype(BF16), jnp.concatenate([ch['a'], g], axis=1)).astype(BF16)
          for t, ch, g in zip(t_inv, chains, gv)]
    ry = [_dot(g, x) for g, x in zip(g_rb, aw)]
    m = [ch['m'].astype(BF16) for ch in chains]
    ys = [_dot((ch['r'].astype(F32) + y[:, :HEAD_DIM]).astype(BF16), mm) + y[:, HEAD_DIM:] + _dot(g, ch['v'])
          for ch, y, mm, g in zip(chains, ry, m, g_rk)]
    pq = [_dot_tn(ch['b_end'], x) for ch, x in zip(chains, aw)]
    kv = [_dot_tn(ch['k_end'], ch['v']) for ch in chains]
    m_new = [_dot((p[:, :HEAD_DIM] + jnp.where(hi == hj, ch['g_total'], 0.0)).astype(BF16), mm) + p[:, HEAD_DIM:] + q
             for ch, p, q, mm in zip(chains, pq, kv, m)]
    return ys, m_new


def _rwkv_kernel(rkvf_ref, lof_ref, rkvb_ref, lob_ref, m0_ref, w0_ref, w2_ref, a0_ref, a2_ref,
                 kk_ref, ka_ref, rk_ref, yf_ref, yb_ref, bvf_ref, bvb_ref, mend_ref, m_f, m_b):
    i = pl.program_id(1)
    n = pl.num_programs(1)

    @pl.when(i == 0)
    def _():
        m_f[...] = m0_ref[:, 0]
        m_b[...] = m0_ref[:, 1]

    w = B_WIDTH
    nh = w // HEAD_DIM
    nb, c = rkvf_ref.shape[:2]
    chains = []
    for d, (rkv_ref, lo_ref, bv_ref, m_ref) in enumerate(
            ((rkvf_ref, lof_ref, bvf_ref, m_f), (rkvb_ref, lob_ref, bvb_ref, m_b))):
        for b in range(nb):
            rkv = rkv_ref[b]
            lo = lo_ref[b]
            p = _rwkv_prep(rkv[:, :w], rkv[:, w:2 * w], rkv[:, 2 * w:], lo[:, :LORA], lo[:, LORA:2 * LORA],
                           w0_ref[d:d + 1], w2_ref[d], a0_ref[d:d + 1], a2_ref[d],
                           kk_ref[...], ka_ref[...], rk_ref[...], rev=bool(d))
            bv_ref[b] = p['bonus_v']
            for h in range(nh):
                sl = slice(h * HEAD_DIM, (h + 1) * HEAD_DIM)
                ch = {key: p[key][:, sl] for key in ('a', 'b', 'k', 'r', 'v', 'b_end', 'k_end', 'g_total')}
                ch.update(earlier=p['earlier'], incl=p['incl'], m=m_ref[b, h])
                chains.append(ch)
    ys, m_new = _rwkv_solve(chains, c)
    for d, (y_ref, m_ref) in enumerate(((yf_ref, m_f), (yb_ref, m_b))):
        for b in range(nb):
            first = (d * nb + b) * nh
            y_ref[b] = jnp.concatenate(ys[first:first + nh], axis=1)
            for h in range(nh):
                m_ref[b, h] = m_new[first + h]

    @pl.when(i == n - 1)
    def _():
        mend_ref[:, 0] = m_f[...]
        mend_ref[:, 1] = m_b[...]


def _rwkv_scan(brkv, blo, m0, w0, w2, a0, a2, k_k, k_a, r_k):
    bsz, t, _ = brkv.shape
    c = RWKV_CHUNK
    nt = t // c
    w = B_WIDTH
    nb = RWKV_BATCH if bsz % RWKV_BATCH == 0 else 1
    full = lambda z: pl.BlockSpec(z.shape, lambda b, i: (0,) * z.ndim)
    fwd = lambda width: pl.BlockSpec((nb, c, width), lambda b, i: (b, i, 0))
    bwd = lambda width: pl.BlockSpec((nb, c, width), lambda b, i: (b, nt - 1 - i, 0))
    nh = w // HEAD_DIM
    st = pl.BlockSpec((nb, 2, nh, HEAD_DIM, HEAD_DIM), lambda b, i: (b, 0, 0, 0, 0))
    o = jax.ShapeDtypeStruct((bsz, t, w), F32)
    return pl.pallas_call(
        _rwkv_kernel,
        grid=(bsz // nb, nt),
        in_specs=[fwd(3 * w), fwd(256), bwd(3 * w), bwd(256), st,
                  full(w0), full(w2), full(a0), full(a2), full(k_k), full(k_a), full(r_k)],
        out_specs=[fwd(w), bwd(w), fwd(w), bwd(w), st],
        out_shape=[o, o, o, o, jax.ShapeDtypeStruct((bsz, 2, nh, HEAD_DIM, HEAD_DIM), F32)],
        scratch_shapes=[pltpu.VMEM((nb, nh, HEAD_DIM, HEAD_DIM), F32), pltpu.VMEM((nb, nh, HEAD_DIM, HEAD_DIM), F32)],
        compiler_params=_params("arbitrary", "arbitrary"),
        name="rwkv_scan",
    )(brkv, blo, brkv, blo, m0, w0, w2, a0, a2, k_k, k_a, r_k)


def _na_kernel(q_ref, k_ref, v_ref, kc_ref, vc_ref, bias_ref, o_ref):
    r = pl.program_id(1)
    rows = pl.num_programs(1)
    kh = bias_ref.shape[3] // GRID_W
    rs = jnp.clip(r - kh // 2, 0, rows - kh)
    start = pl.multiple_of(rs * GRID_W, GRID_W)
    nb = q_ref.shape[0]
    nh = q_ref.shape[2] // HEAD_DIM
    q = [q_ref[b] for b in range(nb)]
    kb = [k_ref[b, pl.ds(start, kh * GRID_W), :] for b in range(nb)]
    vb = [v_ref[b, pl.ds(start, kh * GRID_W), :] for b in range(nb)]
    kc = [kc_ref[b] for b in range(nb)]
    vc = [vc_ref[b] for b in range(nb)]
    ch = [(b, h, slice(h * HEAD_DIM, (h + 1) * HEAD_DIM)) for b in range(nb) for h in range(nh)]
    s_w = [_dot_nt(q[b][:, sl], kb[b][:, sl]) + bias_ref[0, h] for b, h, sl in ch]
    s_c = [_dot_nt(q[b][:, sl], kc[b][:, sl]) for b, h, sl in ch]
    m = [jnp.maximum(jnp.max(x, axis=-1, keepdims=True), jnp.max(y, axis=-1, keepdims=True))
         for x, y in zip(s_w, s_c)]
    p_w = [jnp.exp(x - mm) for x, mm in zip(s_w, m)]
    p_c = [jnp.exp(y - mm) for y, mm in zip(s_c, m)]
    den = [jnp.sum(x, axis=-1, keepdims=True) + jnp.sum(y, axis=-1, keepdims=True) for x, y in zip(p_w, p_c)]
    outs = [(_dot(x.astype(BF16), vb[b][:, sl]) + _dot(y.astype(BF16), vc[b][:, sl])) / d
            for x, y, d, (b, h, sl) in zip(p_w, p_c, den, ch)]
    for b in range(nb):
        o_ref[b] = jnp.concatenate(outs[b * nh:(b + 1) * nh], axis=1)


def _na_bias_table(rpb, rows):
    kh = min(WIN_H, rows)
    nh, nr, ncol = rpb.shape
    qcol = np.arange(GRID_W)[:, None]
    kcol = np.arange(GRID_W)[None, :]
    cstart = np.clip(qcol - WIN_W // 2, 0, GRID_W - WIN_W)
    valid = (kcol >= cstart) & (kcol < cstart + WIN_W)
    hot = ((kcol - qcol + (WIN_W - 1))[None] == np.arange(ncol)[:, None, None]) & valid[None]
    hot = jnp.asarray(hot.reshape(ncol, GRID_W * GRID_W), F32)
    toe = _dot(rpb.reshape(nh * nr, ncol), hot, HI).reshape(nh, nr, GRID_W, GRID_W)
    toe = jnp.where(jnp.asarray(valid)[None, None], toe, MASK_VALUE)
    tab = [jnp.concatenate([toe[:, i - var + (WIN_H - 1)] for i in range(kh)], axis=-1) for var in range(kh)]
    return jnp.stack(tab)


def _na_attention(q, k, v, kc, vc, bias_tab):
    bsz, seq, cw = q.shape
    ctx_len = kc.shape[1]
    rows = seq // GRID_W
    kh = bias_tab.shape[0]
    nh = bias_tab.shape[1]

    def bias_map(b, r):
        return (r - jnp.clip(r - kh // 2, 0, rows - kh), 0, 0, 0)

    nb = NA_BATCH if bsz % NA_BATCH == 0 else 1
    return pl.pallas_call(
        _na_kernel,
        grid=(bsz // nb, rows),
        in_specs=[pl.BlockSpec((nb, GRID_W, cw), lambda b, r: (b, r, 0)),
                  pl.BlockSpec((nb, seq, cw), lambda b, r: (b, 0, 0)),
                  pl.BlockSpec((nb, seq, cw), lambda b, r: (b, 0, 0)),
                  pl.BlockSpec((nb, ctx_len, cw), lambda b, r: (b, 0, 0)),
                  pl.BlockSpec((nb, ctx_len, cw), lambda b, r: (b, 0, 0)),
                  pl.BlockSpec((1, nh, GRID_W, kh * GRID_W), bias_map)],
        out_specs=pl.BlockSpec((nb, GRID_W, cw), lambda b, r: (b, r, 0)),
        out_shape=jax.ShapeDtypeStruct((bsz, seq, cw), F32),
        compiler_params=_params("arbitrary", "arbitrary"),
        name="na_attention",
    )(q, k, v, kc, vc, bias_tab)


def _ctx_attn_kernel(q_ref, k_ref, v_ref, o_ref):
    q = q_ref[0]
    k = k_ref[0]
    v = v_ref[0]
    outs = []
    for h in range(q.shape[1] // HEAD_DIM):
        sl = slice(h * HEAD_DIM, (h + 1) * HEAD_DIM)
        s = _dot_nt(q[:, sl], k[:, sl])
        p = jnp.exp(s - jnp.max(s, axis=-1, keepdims=True))
        outs.append(_dot(p.astype(BF16), v[:, sl]) / jnp.sum(p, axis=-1, keepdims=True))
    o_ref[0] = jnp.concatenate(outs, axis=1)


def _ctx_attention(q, k, v):
    bsz, n, cw = q.shape
    spec = pl.BlockSpec((1, n, cw), lambda b: (b, 0, 0))
    return pl.pallas_call(
        _ctx_attn_kernel, grid=(bsz,), in_specs=[spec, spec, spec], out_specs=spec,
        out_shape=jax.ShapeDtypeStruct((bsz, n, cw), F32),
        compiler_params=_params("arbitrary"), name="ctx_attention",
    )(q, k, v)


def _outproj_kernel(x_ref, hf_ref, hb_ref, ga_ref, yf_ref, yb_ref, bvf_ref, bvb_ref, lo_ref, yc_ref,
                    wo_ref, g2_ref, lng_ref, lnb_ref, gate_ref, g_ref, sh_ref, sc_ref, rw_ref, rb_ref,
                    xo_ref, h_ref, lg_ref):
    ya = jax.nn.gelu(ga_ref[0]) * (hf_ref[0] + hb_ref[0])

    y = yf_ref[0] + yb_ref[0]
    avg = (_head_ones(B_WIDTH).astype(F32) * (1.0 / HEAD_DIM)).astype(BF16)
    mean = _dot_split(y, avg, 3)
    yc_ = y - mean
    var = _dot_split(yc_ * yc_, avg, 2)
    yn = yc_ * lax.rsqrt(var + GN_EPS) * lng_ref[...] + lnb_ref[...] + bvf_ref[0] + bvb_ref[0]
    gate_b = _dot_split(jax.nn.sigmoid(lo_ref[0][:, 2 * LORA:3 * LORA]), g2_ref[...], 2)
    yb = yn * gate_b

    mix = (_dot(ya.astype(BF16), wo_ref[:A_WIDTH]) + _dot(yb.astype(BF16), wo_ref[A_WIDTH:A_WIDTH + B_WIDTH])
           + _dot(yc_ref[0].astype(BF16), wo_ref[A_WIDTH + B_WIDTH:]))
    x = x_ref[0] + gate_ref[0] * mix
    xo_ref[0] = x
    hn = x * lax.rsqrt(jnp.mean(x * x, axis=-1, keepdims=True) + RMS_EPS) * g_ref[...]
    hn = hn * (1.0 + sc_ref[0]) + sh_ref[0]
    h_ref[0] = hn.reshape(h_ref.shape[1:])
    lg_ref[0] = _dot_split(hn, rw_ref[...], 2) + rb_ref[...]


def _outproj(x, hf, hb, ga, yf, yb, bvf, bvb, blo, yc, wo, g2, lng, lnb, gate, g, shift, scale, rw, rb, tm):
    bsz, t, d = x.shape
    ne = rw.shape[1]
    blk = lambda w: pl.BlockSpec((1, tm, w), lambda b, i: (b, i, 0))
    full = lambda a: pl.BlockSpec(a.shape, lambda b, i: (0,) * a.ndim)
    vec = pl.BlockSpec((1, 1, d), lambda b, i: (b, 0, 0))
    return pl.pallas_call(
        _outproj_kernel,
        grid=(bsz, t // tm),
        in_specs=[blk(d), blk(A_WIDTH), blk(A_WIDTH), blk(A_WIDTH), blk(B_WIDTH), blk(B_WIDTH),
                  blk(B_WIDTH), blk(B_WIDTH), blk(256), blk(C_WIDTH),
                  full(wo), full(g2), full(lng), full(lnb), vec, full(g), vec, vec, full(rw), full(rb)],
        out_specs=[blk(d), pl.BlockSpec((1, tm) + _row_tile(d), lambda b, i: (b, i, 0, 0)), blk(ne)],
        out_shape=[jax.ShapeDtypeStruct((bsz, t, d), F32), jax.ShapeDtypeStruct((bsz, t) + _row_tile(d), F32),
                   jax.ShapeDtypeStruct((bsz, t, ne), F32)],
        compiler_params=_params("arbitrary", "arbitrary"),
        name="outproj",
    )(x, hf, hb, ga, yf, yb, bvf, bvb, blo, yc, wo, g2, lng, lnb, gate, g, shift, scale, rw, rb)


def _expert_kernel(be_ref, nb_ref, x_ref, wgu_ref, bgu_ref, wdn_ref, bdn_ref, o_ref, wgu_s, wdn_s):
    i = pl.program_id(0)
    e = be_ref[i]
    fresh = jnp.logical_or(i == 0, be_ref[jnp.maximum(i - 1, 0)] != e)

    @pl.when(fresh)
    def _():
        wgu_s[...] = wgu_ref[0].astype(BF16)
        wdn_s[...] = wdn_ref[0].astype(BF16)

    @pl.when(i < nb_ref[0])
    def _():
        dff, d = wdn_s.shape
        x = x_ref[...].reshape(x_ref.shape[0], d)
        gu = _dot(x.astype(BF16), wgu_s[...]) + bgu_ref[0]
        g_ = jnp.minimum(gu[:, :dff], SWIGLU_LIMIT)
        u_ = jnp.clip(gu[:, dff:], -SWIGLU_LIMIT, SWIGLU_LIMIT)
        act = (u_ + 1.0) * (g_ * jax.nn.sigmoid(SWIGLU_ALPHA * g_))
        o_ref[...] = (_dot(act.astype(BF16), wdn_s[...]) + bdn_ref[0]).reshape(o_ref.shape)

    @pl.when(i >= nb_ref[0])
    def _():
        o_ref[...] = jnp.zeros_like(o_ref)


def _experts(block_e, n_used, xb, w_gu, b_gu, w_dn, b_dn, layer):
    cap = xb.shape[0]
    row = xb.shape[1:]
    depth, ne, d, f2 = w_gu.shape
    dff = w_dn.shape[2]
    nb = cap // MOE_BM
    w_gu, b_gu, w_dn, b_dn = (z.reshape((depth * ne,) + s) for z, s in
                              ((w_gu, (d, f2)), (b_gu, (1, f2)), (w_dn, (dff, d)), (b_dn, (1, d))))
    e0 = layer * ne
    grid_spec = pltpu.PrefetchScalarGridSpec(
        num_scalar_prefetch=2,
        grid=(nb,),
        in_specs=[pl.BlockSpec((MOE_BM,) + row, lambda i, be, nu: (i, 0, 0)),
                  pl.BlockSpec((1, d, f2), lambda i, be, nu: (e0 + be[i], 0, 0)),
                  pl.BlockSpec((1, 1, f2), lambda i, be, nu: (e0 + be[i], 0, 0)),
                  pl.BlockSpec((1, dff, d), lambda i, be, nu: (e0 + be[i], 0, 0)),
                  pl.BlockSpec((1, 1, d), lambda i, be, nu: (e0 + be[i], 0, 0))],
        out_specs=pl.BlockSpec((MOE_BM,) + row, lambda i, be, nu: (i, 0, 0)),
        scratch_shapes=[pltpu.VMEM((d, f2), BF16), pltpu.VMEM((dff, d), BF16)],
    )
    return pl.pallas_call(
        _expert_kernel, grid_spec=grid_spec,
        out_shape=jax.ShapeDtypeStruct((cap,) + row, F32),
        compiler_params=_params("arbitrary"), name="experts",
    )(block_e, n_used, xb, w_gu, b_gu, w_dn, b_dn)


def _route_kernel(lg_ref, idx_ref, gate_ref, rank_ref, cnt_ref, carry):
    i = pl.program_id(0)

    @pl.when(i == 0)
    def _():
        carry[...] = jnp.zeros_like(carry)

    l = lg_ref[...]
    tm, ne = l.shape
    lane = lax.broadcasted_iota(jnp.int32, l.shape, 1)
    vals, idxs, hots = [], [], []
    for _ in range(TOP_K):
        m = jnp.max(l, axis=-1, keepdims=True)
        ix = jnp.min(jnp.where(l == m, lane, ne), axis=-1, keepdims=True)
        hot = lane == ix
        vals.append(m)
        idxs.append(ix)
        hots.append(hot)
        l = jnp.where(hot, -jnp.inf, l)
    ex = [jnp.exp(v - vals[0]) for v in vals]
    den = ex[0] + ex[1] + ex[2] + ex[3]
    hot_all = (hots[0] | hots[1] | hots[2] | hots[3]).astype(BF16)
    tr = lax.broadcasted_iota(jnp.int32, (tm, tm), 0)
    tc = lax.broadcasted_iota(jnp.int32, (tm, tm), 1)
    before = _dot((tc < tr).astype(BF16), hot_all) + carry[...]
    ranks = [jnp.sum(jnp.where(h, before, 0.0), axis=-1, keepdims=True) for h in hots]
    carry[...] = carry[...] + jnp.sum(hot_all.astype(F32), axis=0, keepdims=True)
    idx_ref[...] = jnp.concatenate(idxs, axis=1)
    gate_ref[...] = jnp.concatenate([e / den for e in ex], axis=1)
    rank_ref[...] = jnp.concatenate(ranks, axis=1).astype(jnp.int32)
    cnt_ref[...] = carry[...].astype(jnp.int32)


def _route(logits):
    n, ne = logits.shape
    tm = ROUTE_TM
    assert n % tm == 0, (n, tm)
    blk = pl.BlockSpec((tm, TOP_K), lambda i: (i, 0))
    return pl.pallas_call(
        _route_kernel,
        grid=(n // tm,),
        in_specs=[pl.BlockSpec((tm, ne), lambda i: (i, 0))],
        out_specs=[blk, blk, blk, pl.BlockSpec((1, ne), lambda i: (0, 0))],
        out_shape=[jax.ShapeDtypeStruct((n, TOP_K), jnp.int32), jax.ShapeDtypeStruct((n, TOP_K), F32),
                   jax.ShapeDtypeStruct((n, TOP_K), jnp.int32), jax.ShapeDtypeStruct((1, ne), jnp.int32)],
        scratch_shapes=[pltpu.VMEM((1, ne), F32)],
        compiler_params=_params("arbitrary"),
        name="route",
    )(logits)


def _dispatch_kernel(dest_hbm, h_ref, xb_init, xb_hbm, dest_s, sem_idx, sem):
    del xb_init
    i = pl.program_id(0)
    tm = h_ref.shape[0]
    cp = pltpu.make_async_copy(dest_hbm.at[pl.ds(i * tm * TOP_K, tm * TOP_K)], dest_s, sem_idx)
    cp.start()
    cp.wait()

    def issue(t, carry):
        src = h_ref.at[t]
        for j in range(TOP_K):
            pltpu.make_async_copy(src, xb_hbm.at[dest_s[t * TOP_K + j]], sem).start()
        return carry

    lax.fori_loop(0, tm, issue, 0, unroll=8)
    for _ in range(TOP_K):
        pltpu.make_async_copy(h_ref, xb_hbm.at[pl.ds(0, tm)], sem).wait()


def _dispatch(dest_flat, h, xb_init):
    n = h.shape[0]
    row = h.shape[1:]
    cap = xb_init.shape[0]
    tm = DISPATCH_TM
    assert n % tm == 0, (n, tm)
    any_spec = pl.BlockSpec(memory_space=pl.ANY)
    return pl.pallas_call(
        _dispatch_kernel,
        grid=(n // tm,),
        in_specs=[any_spec, pl.BlockSpec((tm,) + row, lambda i: (i, 0, 0)), any_spec],
        out_specs=any_spec,
        out_shape=jax.ShapeDtypeStruct((cap,) + row, F32),
        scratch_shapes=[pltpu.SMEM((tm * TOP_K,), jnp.int32), pltpu.SemaphoreType.DMA, pltpu.SemaphoreType.DMA],
        input_output_aliases={2: 0},
        compiler_params=pltpu.CompilerParams(dimension_semantics=("arbitrary",), has_side_effects=True),
        name="dispatch",
    )(dest_flat, h, xb_init)


def _combine_kernel(dest_hbm, yb_hbm, x_ref, gt_ref, mg_ref, g_ref, o_ref, dest_s, buf, sem_idx, sem,
                    *, tok0, final):
    i = pl.program_id(0)
    n = pl.num_programs(0)
    tm = x_ref.shape[0]

    def fetch(blk, slot):
        cp = pltpu.make_async_copy(dest_hbm.at[pl.ds((tok0 + blk * tm) * TOP_K, tm * TOP_K)], dest_s.at[slot],
                                   sem_idx)
        cp.start()
        cp.wait()

        def issue(t, carry):
            for j in range(TOP_K):
                pltpu.make_async_copy(yb_hbm.at[dest_s[slot, t * TOP_K + j]], buf.at[slot, j, t],
                                      sem.at[slot]).start()
            return carry

        lax.fori_loop(0, tm, issue, 0, unroll=8)

    @pl.when(i == 0)
    def _():
        fetch(0, 0)

    @pl.when(i + 1 < n)
    def _():
        fetch(i + 1, (i + 1) % 2)

    slot = i % 2
    for j in range(TOP_K):
        pltpu.make_async_copy(yb_hbm.at[pl.ds(0, tm)], buf.at[slot, j], sem.at[slot]).wait()
    gt = gt_ref[...]
    y = gt[:, 0:1] * buf[slot, 0].reshape(x_ref.shape)
    for j in range(1, TOP_K):
        y = y + gt[:, j:j + 1] * buf[slot, j].reshape(x_ref.shape)
    x = x_ref[...] + mg_ref[0] * y
    if final:
        x = x * lax.rsqrt(jnp.mean(x * x, axis=-1, keepdims=True) + RMS_EPS) * g_ref[...]
    o_ref[...] = x


def _combine(dest_flat, yb, x, gates, mod_gate, g, tok0, final):
    bsz, t, d = x.shape
    n = bsz * t
    tm = min(COMBINE_TM, t)
    per_b = t // tm
    any_spec = pl.BlockSpec(memory_space=pl.ANY)
    out = pl.pallas_call(
        functools.partial(_combine_kernel, tok0=tok0, final=final),
        grid=(n // tm,),
        in_specs=[any_spec, any_spec,
                  pl.BlockSpec((tm, d), lambda i: (i, 0)),
                  pl.BlockSpec((tm, TOP_K), lambda i: (i, 0)),
                  pl.BlockSpec((1, 1, d), lambda i: (i // per_b, 0, 0)),
                  pl.BlockSpec((1, d), lambda i: (0, 0))],
        out_specs=pl.BlockSpec((tm, d), lambda i: (i, 0)),
        out_shape=jax.ShapeDtypeStruct((n, d), F32),
        scratch_shapes=[pltpu.SMEM((2, tm * TOP_K), jnp.int32), pltpu.VMEM((2, TOP_K, tm) + _row_tile(d), F32),
                        pltpu.SemaphoreType.DMA, pltpu.SemaphoreType.DMA((2,))],
        compiler_params=_params("arbitrary"),
        name="combine",
    )(dest_flat, yb, x.reshape(n, d), gates, mod_gate, g)
    return out.reshape(bsz, t, d)


def _moe_blocks(n_tokens, ne):
    return (n_tokens * TOP_K + ne * (MOE_BM - 1) + MOE_BM - 1) // MOE_BM


def _moe_plan(logits, nb):
    n, ne = logits.shape
    assert nb >= _moe_blocks(n, ne), (nb, n)
    idx, gates, rank, counts = _route(logits)
    counts = counts[0]
    padded = (counts + MOE_BM - 1) // MOE_BM * MOE_BM
    pad_end = jnp.cumsum(padded)
    pad_start = pad_end - padded
    hot = idx[:, :, None] == jnp.arange(ne, dtype=jnp.int32)[None, None, :]
    dest = jnp.sum(jnp.where(hot, pad_start[None, None, :], 0), axis=-1) + rank
    block_start = jnp.arange(nb, dtype=jnp.int32) * MOE_BM
    block_e = jnp.minimum(jnp.sum(pad_end[None, :] <= block_start[:, None], axis=1), ne - 1).astype(jnp.int32)
    n_used = (pad_end[-1] // MOE_BM).astype(jnp.int32).reshape(1)
    return dest.reshape(-1).astype(jnp.int32), gates, block_e, n_used


def _block_diag(w):
    nh, n, _ = w.shape
    eye = jnp.eye(nh, dtype=w.dtype)
    return (w[:, :, None, :] * eye[:, None, :, None]).reshape(nh * n, nh * n)


def _permute_in_cols(w_in):
    a0 = 0
    b0 = 2 * A_WIDTH
    c0 = b0 + 3 * B_WIDTH + 3 * LORA
    d = w_in.shape[0]
    pad = jnp.zeros((d, 256 - 3 * LORA), w_in.dtype)
    return jnp.concatenate([w_in[:, a0:a0 + A_WIDTH], w_in[:, b0:b0 + 3 * B_WIDTH],
                            w_in[:, b0 + 3 * B_WIDTH:c0], pad,
                            w_in[:, A_WIDTH:2 * A_WIDTH], w_in[:, c0:]], axis=1)


def _mixers(p_ctx, p_lat, lp, need_ctx):
    (xcf_c, xcb_c, ga_c, brkv_c, blo_c, q_c, k_c, v_c) = p_ctx
    (xcf_l, xcb_l, ga_l, brkv_l, blo_l, q_l, k_l, v_l) = p_lat
    bsz = xcf_c.shape[0]

    lru_w = (lp['wr_bd'], lp['br'], lp['wi_bd'], lp['bi'], lp['lam'])
    hf_c, hb_c, hend = _lru_scan(xcf_c, xcb_c, jnp.zeros((bsz, 2, A_WIDTH), F32), *lru_w)
    hf_l, hb_l, _ = _lru_scan(xcf_l, xcb_l, hend, *lru_w)

    rw_w = (lp['w0'], lp['w2'], lp['a0'], lp['a2'], lp['k_k'], lp['k_a'], lp['r_k'])
    yf_c, yb_c, bvf_c, bvb_c, mend = _rwkv_scan(brkv_c, blo_c, jnp.zeros((bsz, 2, B_WIDTH // HEAD_DIM, HEAD_DIM, HEAD_DIM), F32), *rw_w)
    yf_l, yb_l, bvf_l, bvb_l, _ = _rwkv_scan(brkv_l, blo_l, mend, *rw_w)

    yc_l = _na_attention(q_l, k_l, v_l, k_c, v_c, lp['bias_tab'])
    lat = (hf_l, hb_l, ga_l, yf_l, yb_l, bvf_l, bvb_l, blo_l, yc_l)
    ctx = None
    if need_ctx:
        yc_c = _ctx_attention(q_c, k_c, v_c)
        ctx = (hf_c, hb_c, ga_c, yf_c, yb_c, bvf_c, bvb_c, blo_c, yc_c)
    return ctx, lat


def kernel(x, c, ctx, c_ctx, ada_w, ada_b, norm_mix_g, norm_ffn_g, w_in, w_out, lru_conv_w, lru_conv_b, lru_wr, lru_br, lru_wi, lru_bi, lru_lambda, rwkv_mu, rwkv_w0, rwkv_w2, rwkv_a0, rwkv_a2, rwkv_g2, rwkv_kk, rwkv_ka, rwkv_rk, rwkv_lnx_g, rwkv_lnx_b, na_rpb, router_w, router_b, moe_w_gu, moe_b_gu, moe_w_dn, moe_b_dn, final_g):
    bsz, seq, d = x.shape
    ctx_len = ctx.shape[1]
    depth = ada_w.shape[0]
    rows = seq // GRID_W
    tm_l = 512
    tm_c = min(256, ctx_len)

    cond = jnp.concatenate([c, c_ctx[None], jnp.zeros((8 - bsz - 1, d), F32)], axis=0)
    mod = _adaln(cond, ada_w, ada_b)

    moe_nb = _moe_blocks(bsz * (ctx_len + seq), router_w.shape[2])
    xb = jnp.zeros((moe_nb * MOE_BM,) + _row_tile(d), F32)
    xl, xc = x, ctx
    for l in range(depth):
        last = l == depth - 1
        ml = [mod[l, :bsz, j * d:(j + 1) * d][:, None, :] for j in range(6)]
        mc = [jnp.broadcast_to(mod[l, bsz:bsz + 1, j * d:(j + 1) * d][:, None, :], (bsz, 1, d)) for j in range(6)]
        n_b = 3 * B_WIDTH
        mu = rwkv_mu[l]
        lp = dict(
            wr_bd=jnp.stack([_block_diag(lru_wr[l, dd]) for dd in range(2)]),
            wi_bd=jnp.stack([_block_diag(lru_wi[l, dd]) for dd in range(2)]),
            br=lru_br[l], bi=lru_bi[l], lam=lru_lambda[l],
            w0=rwkv_w0[l], w2=rwkv_w2[l], a0=rwkv_a0[l], a2=rwkv_a2[l],
            k_k=rwkv_kk[l][None], k_a=rwkv_ka[l][None], r_k=rwkv_rk[l].reshape(1, B_WIDTH),
            bias_tab=_na_bias_table(na_rpb[l], rows),
        )
        w_perm = _permute_in_cols(w_in[l]).astype(BF16)
        mu_perm = jnp.concatenate([mu[:, :n_b], mu[:, n_b:], jnp.zeros((2, 256 - 3 * LORA), F32)], axis=1)
        g_mix = norm_mix_g[l][None]
        g_ffn = norm_ffn_g[l][None]
        wo = w_out[l].astype(BF16)

        p_lat = _inproj(xl, g_mix, ml[0], ml[1], w_perm, lru_conv_w[l], lru_conv_b[l], mu_perm, tm_l)
        p_ctx = _inproj(xc, g_mix, mc[0], mc[1], w_perm, lru_conv_w[l], lru_conv_b[l], mu_perm, tm_c)
        mix_c, mix_l = _mixers(p_ctx, p_lat, lp, not last)

        fin = (wo, rwkv_g2[l], rwkv_lnx_g[l][None], rwkv_lnx_b[l][None])
        rt = (router_w[l], router_b[l][None])
        xl, hl, lg_l = _outproj(xl, *mix_l, *fin, ml[2], g_ffn, ml[3], ml[4], *rt, tm_l)
        ne = router_w.shape[2]
        if last:
            tok = hl.reshape((-1,) + _row_tile(d))
            lg = lg_l.reshape(-1, ne)
            nc = 0
        else:
            xc, hc, lg_c = _outproj(xc, *mix_c, *fin, mc[2], g_ffn, mc[3], mc[4], *rt, tm_c)
            tok = jnp.concatenate([hc.reshape((-1,) + _row_tile(d)), hl.reshape((-1,) + _row_tile(d))], axis=0)
            lg = jnp.concatenate([lg_c.reshape(-1, ne), lg_l.reshape(-1, ne)], axis=0)
            nc = bsz * ctx_len
        dest, gates, block_e, n_used = _moe_plan(lg, moe_nb)
        xb = _dispatch(dest, tok, xb)
        yb = _experts(block_e, n_used, xb, moe_w_gu, moe_b_gu, moe_w_dn, moe_b_dn, l)
        if not last:
            xc = _combine(dest, yb, xc, gates[:nc], mc[5], final_g[None], 0, False)
        xl = _combine(dest, yb, xl, gates[nc:], ml[5], final_g[None], nc, last)
    return xl
```

```python
import functools

import numpy as np
import jax
import jax.numpy as jnp
from jax import lax
from jax.experimental import pallas as pl
from jax.experimental.pallas import tpu as pltpu

F32 = jnp.float32
BF16 = jnp.bfloat16
HI = lax.Precision.HIGHEST

RMS_EPS = 1e-6
GN_EPS = 64e-5
LRU_C = 8.0
SWIGLU_ALPHA = 1.702
SWIGLU_LIMIT = 7.0
TOP_K = 4
GRID_W = 64
WIN_H = 8
WIN_W = 16
HEAD_DIM = 64
A_WIDTH = 256
B_WIDTH = 256
C_WIDTH = 512
LORA = 64
LANES = 128
HALO = 8
RWKV_CHUNK = 64
RWKV_BATCH = 4
NA_BATCH = 2
LRU_CHUNK = 256
MOE_BM = 256
ROUTE_TM = 512
DISPATCH_TM = 1024
COMBINE_TM = 512
MASK_VALUE = -1e30
VMEM_LIMIT = 56 * 1024 * 1024


def _dot(a, b, prec=None):
    return jnp.dot(a, b, preferred_element_type=F32, precision=prec)


def _dot_nt(a, b, prec=None):
    return lax.dot_general(a, b, (((1,), (1,)), ((), ())), preferred_element_type=F32, precision=prec)


def _dot_tn(a, b, prec=None):
    return lax.dot_general(a, b, (((0,), (0,)), ((), ())), preferred_element_type=F32, precision=prec)


def _split(x, n):
    terms = []
    for _ in range(n):
        t = x.astype(BF16)
        terms.append(t)
        x = x - t.astype(F32)
    return terms


def _dot_split(a, b, order):
    a_t = [a] if a.dtype == BF16 else _split(a, order)
    b_t = [b] if b.dtype == BF16 else _split(b, order)
    return _dot_pieces(a_t, b_t, order)


def _dot_pieces(a_t, b_t, order):
    out = None
    for i, x in enumerate(a_t):
        for j, y in enumerate(b_t):
            if i + j < order:
                p = _dot(x, y)
                out = p if out is None else out + p
    return out


def _softplus(z):
    return jnp.maximum(z, 0.0) + jnp.log1p(jnp.exp(-jnp.abs(z)))


def _params(*sem):
    return pltpu.CompilerParams(dimension_semantics=sem, vmem_limit_bytes=VMEM_LIMIT)


def _row_tile(d):
    assert d % LANES == 0, d
    return (d // LANES, LANES)


def _head_ones(width):
    r = lax.broadcasted_iota(jnp.int32, (width, width), 0) // HEAD_DIM
    c = lax.broadcasted_iota(jnp.int32, (width, width), 1) // HEAD_DIM
    return r == c


def _adaln_kernel(cond_ref, w_ref, b_ref, o_ref):
    c = cond_ref[...]
    o_ref[0] = _dot(c * jax.nn.sigmoid(c), w_ref[0], HI) + b_ref[0]


def _adaln(cond, ada_w, ada_b):
    depth, d, n = ada_w.shape
    tn = 1536
    return pl.pallas_call(
        _adaln_kernel,
        grid=(depth, n // tn),
        in_specs=[pl.BlockSpec((8, d), lambda l, j: (0, 0)),
                  pl.BlockSpec((1, d, tn), lambda l, j: (l, 0, j)),
                  pl.BlockSpec((1, 1, tn), lambda l, j: (l, 0, j))],
        out_specs=pl.BlockSpec((1, 8, tn), lambda l, j: (l, 0, j)),
        out_shape=jax.ShapeDtypeStruct((depth, 8, n), F32),
        compiler_params=_params("arbitrary", "arbitrary"),
        name="adaln",
    )(cond, ada_w, ada_b.reshape(depth, 1, n))


N_SHIFT = 1280
N_PROJ = 3072


def _inproj_kernel(x_ref, xp_ref, xn_ref, g_ref, sh_ref, sc_ref, w_ref, cw_ref, cb_ref, mu_ref,
                   xcf_ref, xcb_ref, ga_ref, brkv_ref, blo_ref, q_ref, k_ref, v_ref):
    i = pl.program_id(1)
    n = pl.num_programs(1)
    tm = x_ref.shape[1]
    g = g_ref[...]
    sh = sh_ref[0]
    sc = sc_ref[0]

    def norm_mod(x):
        y = x * lax.rsqrt(jnp.mean(x * x, axis=-1, keepdims=True) + RMS_EPS) * g
        return y * (1.0 + sc) + sh

    p = _dot(norm_mod(x_ref[0]).astype(BF16), w_ref[...])
    halo = jnp.concatenate([xp_ref[0], xn_ref[0]], axis=0)
    ph = _dot(norm_mod(halo).astype(BF16), w_ref[:, :N_SHIFT])
    p_prev = jnp.where(i > 0, ph[:HALO], 0.0)
    p_next = jnp.where(i < n - 1, ph[HALO:], 0.0)
    ext = jnp.concatenate([p_prev, p[:, :N_SHIFT], p_next], axis=0)

    xa = ext[:, :A_WIDTH]
    cw = cw_ref[...]
    cb = cb_ref[...]
    xcf = cb[0:1]
    xcb = cb[1:2]
    for j in range(cw.shape[1]):
        xcf = xcf + cw[0, j:j + 1] * xa[HALO - j:HALO - j + tm]
        xcb = xcb + cw[1, j:j + 1] * xa[HALO + j:HALO + j + tm]
    xcf_ref[0] = xcf
    xcb_ref[0] = xcb

    pb = ext[HALO:HALO + tm, A_WIDTH:]
    prev = ext[HALO - 1:HALO - 1 + tm, A_WIDTH:]
    nxt = ext[HALO + 1:HALO + 1 + tm, A_WIDTH:]
    mu = mu_ref[...]
    sb = pb + mu[0:1] * (prev - pb) + mu[1:2] * (nxt - pb)
    brkv_ref[0] = sb[:, :3 * B_WIDTH]
    blo_ref[0] = sb[:, 3 * B_WIDTH:]

    o = N_SHIFT
    ga_ref[0] = p[:, o:o + A_WIDTH]
    o += A_WIDTH
    q_ref[0] = (p[:, o:o + C_WIDTH] * (HEAD_DIM ** -0.5)).astype(BF16)
    k_ref[0] = p[:, o + C_WIDTH:o + 2 * C_WIDTH].astype(BF16)
    v_ref[0] = p[:, o + 2 * C_WIDTH:o + 3 * C_WIDTH].astype(BF16)


def _inproj(x, g, shift, scale, w_perm, conv_w, conv_b, mu_perm, tm):
    bsz, t, d = x.shape
    nt = t // tm
    hb = tm // HALO
    last = t // HALO - 1
    f = lambda shape, dt=F32: jax.ShapeDtypeStruct(shape, dt)
    blk = lambda w: pl.BlockSpec((1, tm, w), lambda b, i: (b, i, 0))
    full = lambda a: pl.BlockSpec(a.shape, lambda b, i: (0,) * a.ndim)
    return pl.pallas_call(
        _inproj_kernel,
        grid=(bsz, nt),
        in_specs=[blk(d),
                  pl.BlockSpec((1, HALO, d), lambda b, i: (b, jnp.maximum(i * hb - 1, 0), 0)),
                  pl.BlockSpec((1, HALO, d), lambda b, i: (b, jnp.minimum((i + 1) * hb, last), 0)),
                  full(g),
                  pl.BlockSpec((1, 1, d), lambda b, i: (b, 0, 0)),
                  pl.BlockSpec((1, 1, d), lambda b, i: (b, 0, 0)),
                  full(w_perm), full(conv_w), full(conv_b), full(mu_perm)],
        out_specs=[blk(A_WIDTH), blk(A_WIDTH), blk(A_WIDTH), blk(3 * B_WIDTH), blk(256),
                   blk(C_WIDTH), blk(C_WIDTH), blk(C_WIDTH)],
        out_shape=[f((bsz, t, A_WIDTH)), f((bsz, t, A_WIDTH)), f((bsz, t, A_WIDTH)),
                   f((bsz, t, 3 * B_WIDTH)), f((bsz, t, 256)),
                   f((bsz, t, C_WIDTH), BF16), f((bsz, t, C_WIDTH), BF16), f((bsz, t, C_WIDTH), BF16)],
        compiler_params=_params("arbitrary", "arbitrary"),
        name="inproj",
    )(x, x, x, g, shift, scale, w_perm, conv_w, conv_b, mu_perm)


def _chunk_scan(a, b, rev):
    n = a.shape[0]
    row = lax.broadcasted_iota(jnp.int32, a.shape, 0)
    s = 1
    while s < n:
        if rev:
            keep = row < n - s
            a_s = jnp.where(keep, pltpu.roll(a, n - s, 0), 1.0)
            b_s = jnp.where(keep, pltpu.roll(b, n - s, 0), 0.0)
        else:
            keep = row >= s
            a_s = jnp.where(keep, pltpu.roll(a, s, 0), 1.0)
            b_s = jnp.where(keep, pltpu.roll(b, s, 0), 0.0)
        b = a * b_s + b
        a = a * a_s
        s *= 2
    return a, b


def _lru_kernel(xf_ref, xb_ref, h0_ref, wr_ref, br_ref, wi_ref, bi_ref, lam_ref,
                hf_ref, hb_ref, hend_ref, carry):
    i = pl.program_id(1)
    n = pl.num_programs(1)
    ch = xf_ref.shape[1]

    @pl.when(i == 0)
    def _():
        carry[...] = h0_ref[0]

    for d, (x_ref, o_ref) in enumerate(((xf_ref, hf_ref), (xb_ref, hb_ref))):
        x = x_ref[0]
        x_t = _split(x, 2)
        gate_r = jax.nn.sigmoid(_dot_pieces(x_t, _split(wr_ref[d], 2), 2) + br_ref[d:d + 1])
        gate_i = jax.nn.sigmoid(_dot_pieces(x_t, _split(wi_ref[d], 2), 2) + bi_ref[d:d + 1])
        log_a = -LRU_C * gate_r * _softplus(-lam_ref[d:d + 1])
        a = jnp.exp(log_a)
        b = jnp.sqrt(1.0 - jnp.exp(2.0 * log_a)) * (gate_i * x)
        a_cum, h = _chunk_scan(a, b, rev=bool(d))
        h = h + a_cum * carry[d:d + 1]
        o_ref[0] = h
        carry[d:d + 1] = h[0:1] if d else h[ch - 1:ch]

    @pl.when(i == n - 1)
    def _():
        hend_ref[0] = carry[...]


def _lru_scan(xcf, xcb, h0, wr_bd, br, wi_bd, bi, lam):
    bsz, t, a = xcf.shape
    ch = min(LRU_CHUNK, t)
    nt = t // ch
    full = lambda z: pl.BlockSpec(z.shape, lambda b, i: (0,) * z.ndim)
    fwd = pl.BlockSpec((1, ch, a), lambda b, i: (b, i, 0))
    bwd = pl.BlockSpec((1, ch, a), lambda b, i: (b, nt - 1 - i, 0))
    st = pl.BlockSpec((1, 2, a), lambda b, i: (b, 0, 0))
    return pl.pallas_call(
        _lru_kernel,
        grid=(bsz, nt),
        in_specs=[fwd, bwd, st, full(wr_bd), full(br), full(wi_bd), full(bi), full(lam)],
        out_specs=[fwd, bwd, st],
        out_shape=[jax.ShapeDtypeStruct((bsz, t, a), F32), jax.ShapeDtypeStruct((bsz, t, a), F32),
                   jax.ShapeDtypeStruct((bsz, 2, a), F32)],
        scratch_shapes=[pltpu.VMEM((2, a), F32)],
        compiler_params=_params("arbitrary", "arbitrary"),
        name="lru_scan",
    )(xcf, xcb, h0, wr_bd, br, wi_bd, bi, lam)


def _rwkv_prep(r, k, v, wlo, alo, w0, w2, a0, a2, k_k, k_a, r_k, rev):
    c, width = r.shape
    ones_bd = _head_ones(width).astype(BF16)
    w_raw = -_softplus(-(w0 + _dot_split(jnp.tanh(wlo), w2, 2))) - 0.5
    logw = -jnp.exp(w_raw)
    a_gate = jax.nn.sigmoid(a0 + _dot_split(alo, a2, 2))
    kk = k * k_k
    kk = kk / jnp.maximum(jnp.sqrt(_dot_split(kk * kk, ones_bd, 2)), 1e-12)
    k_eff = k * (1.0 + (a_gate - 1.0) * k_a)
    b_vec = kk * a_gate

    tr = lax.broadcasted_iota(jnp.int32, (c, c), 0)
    tc = lax.broadcasted_iota(jnp.int32, (c, c), 1)
    tri = (tc >= tr) if rev else (tc <= tr)
    cum = _dot_split(tri.astype(BF16), logw, 3)
    total = cum[0:1] if rev else cum[c - 1:c]
    g_inv = jnp.exp(-cum)
    g_end = jnp.exp(total - cum)
    earlier = (tc > tr) if rev else (tc < tr)
    return dict(
        a=(-kk * jnp.exp(cum - logw)).astype(BF16), b=(b_vec * g_inv).astype(BF16),
        k=(k_eff * g_inv).astype(BF16), r=(r * jnp.exp(cum)).astype(BF16), v=v.astype(BF16),
        b_end=(b_vec * g_end).astype(BF16), k_end=(k_eff * g_end).astype(BF16), g_total=jnp.exp(total),
        earlier=earlier, incl=earlier | (tr == tc),
        bonus_v=_dot_split(r * k_eff * r_k, ones_bd, 2) * v)


def _rwkv_solve(chains, c):
    tr = lax.broadcasted_iota(jnp.int32, (c, c), 0)
    tc = lax.broadcasted_iota(jnp.int32, (c, c), 1)
    eye_c = (tr == tc).astype(F32)
    hi = lax.broadcasted_iota(jnp.int32, (HEAD_DIM, HEAD_DIM), 0)
    hj = lax.broadcasted_iota(jnp.int32, (HEAD_DIM, HEAD_DIM), 1)
    n_ab = [jnp.where(ch['earlier'], _dot_nt(ch['a'], ch['b']), 0.0) for ch in chains]
    g_ak = [jnp.where(ch['earlier'], _dot_nt(ch['a'], ch['k']), 0.0).astype(BF16) for ch in chains]
    g_rb = [jnp.where(ch['incl'], _dot_nt(ch['r'], ch['b']), 0.0).astype(BF16) for ch in chains]
    g_rk = [jnp.where(ch['incl'], _dot_nt(ch['r'], ch['k']), 0.0).astype(BF16) for ch in chains]
    gv = [_dot(g, ch['v']).astype(BF16) for g, ch in zip(g_ak, chains)]

    t_inv = [eye_c + n for n in n_ab]
    pw = [n.astype(BF16) for n in n_ab]
    s = 2
    while s < c:
        pw = [_dot(p, p).astype(BF16) for p in pw]
        t_inv = [t + _dot(t.astype(BF16), p) for t, p in zip(t_inv, pw)]
        s *= 2

    aw = [_dot(t.astype(BF16), jnp.concatenate([ch['a'], g], axis=1)).astype(BF16)
          for t, ch, g in zip(t_inv, chains, gv)]
    ry = [_dot(g, x) for g, x in zip(g_rb, aw)]
    m = [ch['m'].astype(BF16) for ch in chains]
    ys = [_dot((ch['r'].astype(F32) + y[:, :HEAD_DIM]).astype(BF16), mm) + y[:, HEAD_DIM:] + _dot(g, ch['v'])
          for ch, y, mm, g in zip(chains, ry, m, g_rk)]
    pq = [_dot_tn(ch['b_end'], x) for ch, x in zip(chains, aw)]
    kv = [_dot_tn(ch['k_end'], ch['v']) for ch in chains]
    m_new = [_dot((p[:, :HEAD_DIM] + jnp.where(hi == hj, ch['g_total'], 0.0)).astype(BF16), mm) + p[:, HEAD_DIM:] + q
             for ch, p, q, mm in zip(chains, pq, kv, m)]
    return ys, m_new


def _rwkv_kernel(rkvf_ref, lof_ref, rkvb_ref, lob_ref, m0_ref, w0_ref, w2_ref, a0_ref, a2_ref,
                 kk_ref, ka_ref, rk_ref, yf_ref, yb_ref, bvf_ref, bvb_ref, mend_ref, m_f, m_b):
    i = pl.program_id(1)
    n = pl.num_programs(1)

    @pl.when(i == 0)
    def _():
        m_f[...] = m0_ref[:, 0]
        m_b[...] = m0_ref[:, 1]

    w = B_WIDTH
    nh = w // HEAD_DIM
    nb, c = rkvf_ref.shape[:2]
    chains = []
    for d, (rkv_ref, lo_ref, bv_ref, m_ref) in enumerate(
            ((rkvf_ref, lof_ref, bvf_ref, m_f), (rkvb_ref, lob_ref, bvb_ref, m_b))):
        for b in range(nb):
            rkv = rkv_ref[b]
            lo = lo_ref[b]
            p = _rwkv_prep(rkv[:, :w], rkv[:, w:2 * w], rkv[:, 2 * w:], lo[:, :LORA], lo[:, LORA:2 * LORA],
                           w0_ref[d:d + 1], w2_ref[d], a0_ref[d:d + 1], a2_ref[d],
                           kk_ref[...], ka_ref[...], rk_ref[...], rev=bool(d))
            bv_ref[b] = p['bonus_v']
            for h in range(nh):
                sl = slice(h * HEAD_DIM, (h + 1) * HEAD_DIM)
                ch = {key: p[key][:, sl] for key in ('a', 'b', 'k', 'r', 'v', 'b_end', 'k_end', 'g_total')}
                ch.update(earlier=p['earlier'], incl=p['incl'], m=m_ref[b, h])
                chains.append(ch)
    ys, m_new = _rwkv_solve(chains, c)
    for d, (y_ref, m_ref) in enumerate(((yf_ref, m_f), (yb_ref, m_b))):
        for b in range(nb):
            first = (d * nb + b) * nh
            y_ref[b] = jnp.concatenate(ys[first:first + nh], axis=1)
            for h in range(nh):
                m_ref[b, h] = m_new[first + h]

    @pl.when(i == n - 1)
    def _():
        mend_ref[:, 0] = m_f[...]
        mend_ref[:, 1] = m_b[...]


def _rwkv_scan(brkv, blo, m0, w0, w2, a0, a2, k_k, k_a, r_k):
    bsz, t, _ = brkv.shape
    c = RWKV_CHUNK
    nt = t // c
    w = B_WIDTH
    nb = RWKV_BATCH if bsz % RWKV_BATCH == 0 else 1
    full = lambda z: pl.BlockSpec(z.shape, lambda b, i: (0,) * z.ndim)
    fwd = lambda width: pl.BlockSpec((nb, c, width), lambda b, i: (b, i, 0))
    bwd = lambda width: pl.BlockSpec((nb, c, width), lambda b, i: (b, nt - 1 - i, 0))
    nh = w // HEAD_DIM
    st = pl.BlockSpec((nb, 2, nh, HEAD_DIM, HEAD_DIM), lambda b, i: (b, 0, 0, 0, 0))
    o = jax.ShapeDtypeStruct((bsz, t, w), F32)
    return pl.pallas_call(
        _rwkv_kernel,
        grid=(bsz // nb, nt),
        in_specs=[fwd(3 * w), fwd(256), bwd(3 * w), bwd(256), st,
                  full(w0), full(w2), full(a0), full(a2), full(k_k), full(k_a), full(r_k)],
        out_specs=[fwd(w), bwd(w), fwd(w), bwd(w), st],
        out_shape=[o, o, o, o, jax.ShapeDtypeStruct((bsz, 2, nh, HEAD_DIM, HEAD_DIM), F32)],
        scratch_shapes=[pltpu.VMEM((nb, nh, HEAD_DIM, HEAD_DIM), F32), pltpu.VMEM((nb, nh, HEAD_DIM, HEAD_DIM), F32)],
        compiler_params=_params("arbitrary", "arbitrary"),
        name="rwkv_scan",
    )(brkv, blo, brkv, blo, m0, w0, w2, a0, a2, k_k, k_a, r_k)


def _na_kernel(q_ref, k_ref, v_ref, kc_ref, vc_ref, bias_ref, o_ref):
    r = pl.program_id(1)
    rows = pl.num_programs(1)
    kh = bias_ref.shape[3] // GRID_W
    rs = jnp.clip(r - kh // 2, 0, rows - kh)
    start = pl.multiple_of(rs * GRID_W, GRID_W)
    nb = q_ref.shape[0]
    nh = q_ref.shape[2] // HEAD_DIM
    q = [q_ref[b] for b in range(nb)]
    kb = [k_ref[b, pl.ds(start, kh * GRID_W), :] for b in range(nb)]
    vb = [v_ref[b, pl.ds(start, kh * GRID_W), :] for b in range(nb)]
    kc = [kc_ref[b] for b in range(nb)]
    vc = [vc_ref[b] for b in range(nb)]
    ch = [(b, h, slice(h * HEAD_DIM, (h + 1) * HEAD_DIM)) for b in range(nb) for h in range(nh)]
    s_w = [_dot_nt(q[b][:, sl], kb[b][:, sl]) + bias_ref[0, h] for b, h, sl in ch]
    s_c = [_dot_nt(q[b][:, sl], kc[b][:, sl]) for b, h, sl in ch]
    m = [jnp.maximum(jnp.max(x, axis=-1, keepdims=True), jnp.max(y, axis=-1, keepdims=True))
         for x, y in zip(s_w, s_c)]
    p_w = [jnp.exp(x - mm) for x, mm in zip(s_w, m)]
    p_c = [jnp.exp(y - mm) for y, mm in zip(s_c, m)]
    den = [jnp.sum(x, axis=-1, keepdims=True) + jnp.sum(y, axis=-1, keepdims=True) for x, y in zip(p_w, p_c)]
    outs = [(_dot(x.astype(BF16), vb[b][:, sl]) + _dot(y.astype(BF16), vc[b][:, sl])) / d
            for x, y, d, (b, h, sl) in zip(p_w, p_c, den, ch)]
    for b in range(nb):
        o_ref[b] = jnp.concatenate(outs[b * nh:(b + 1) * nh], axis=1)


def _na_bias_table(rpb, rows):
    kh = min(WIN_H, rows)
    nh, nr, ncol = rpb.shape
    qcol = np.arange(GRID_W)[:, None]
    kcol = np.arange(GRID_W)[None, :]
    cstart = np.clip(qcol - WIN_W // 2, 0, GRID_W - WIN_W)
    valid = (kcol >= cstart) & (kcol < cstart + WIN_W)
    hot = ((kcol - qcol + (WIN_W - 1))[None] == np.arange(ncol)[:, None, None]) & valid[None]
    hot = jnp.asarray(hot.reshape(ncol, GRID_W * GRID_W), F32)
    toe = _dot(rpb.reshape(nh * nr, ncol), hot, HI).reshape(nh, nr, GRID_W, GRID_W)
    toe = jnp.where(jnp.asarray(valid)[None, None], toe, MASK_VALUE)
    tab = [jnp.concatenate([toe[:, i - var + (WIN_H - 1)] for i in range(kh)], axis=-1) for var in range(kh)]
    return jnp.stack(tab)


def _na_attention(q, k, v, kc, vc, bias_tab):
    bsz, seq, cw = q.shape
    ctx_len = kc.shape[1]
    rows = seq // GRID_W
    kh = bias_tab.shape[0]
    nh = bias_tab.shape[1]

    def bias_map(b, r):
        return (r - jnp.clip(r - kh // 2, 0, rows - kh), 0, 0, 0)

    nb = NA_BATCH if bsz % NA_BATCH == 0 else 1
    return pl.pallas_call(
        _na_kernel,
        grid=(bsz // nb, rows),
        in_specs=[pl.BlockSpec((nb, GRID_W, cw), lambda b, r: (b, r, 0)),
                  pl.BlockSpec((nb, seq, cw), lambda b, r: (b, 0, 0)),
                  pl.BlockSpec((nb, seq, cw), lambda b, r: (b, 0, 0)),
                  pl.BlockSpec((nb, ctx_len, cw), lambda b, r: (b, 0, 0)),
                  pl.BlockSpec((nb, ctx_len, cw), lambda b, r: (b, 0, 0)),
                  pl.BlockSpec((1, nh, GRID_W, kh * GRID_W), bias_map)],
        out_specs=pl.BlockSpec((nb, GRID_W, cw), lambda b, r: (b, r, 0)),
        out_shape=jax.ShapeDtypeStruct((bsz, seq, cw), F32),
        compiler_params=_params("arbitrary", "arbitrary"),
        name="na_attention",
    )(q, k, v, kc, vc, bias_tab)


def _ctx_attn_kernel(q_ref, k_ref, v_ref, o_ref):
    q = q_ref[0]
    k = k_ref[0]
    v = v_ref[0]
    outs = []
    for h in range(q.shape[1] // HEAD_DIM):
        sl = slice(h * HEAD_DIM, (h + 1) * HEAD_DIM)
        s = _dot_nt(q[:, sl], k[:, sl])
        p = jnp.exp(s - jnp.max(s, axis=-1, keepdims=True))
        outs.append(_dot(p.astype(BF16), v[:, sl]) / jnp.sum(p, axis=-1, keepdims=True))
    o_ref[0] = jnp.concatenate(outs, axis=1)


def _ctx_attention(q, k, v):
    bsz, n, cw = q.shape
    spec = pl.BlockSpec((1, n, cw), lambda b: (b, 0, 0))
    return pl.pallas_call(
        _ctx_attn_kernel, grid=(bsz,), in_specs=[spec, spec, spec], out_specs=spec,
        out_shape=jax.ShapeDtypeStruct((bsz, n, cw), F32),
        compiler_params=_params("arbitrary"), name="ctx_attention",
    )(q, k, v)


def _outproj_kernel(x_ref, hf_ref, hb_ref, ga_ref, yf_ref, yb_ref, bvf_ref, bvb_ref, lo_ref, yc_ref,
                    wo_ref, g2_ref, lng_ref, lnb_ref, gate_ref, g_ref, sh_ref, sc_ref, rw_ref, rb_ref,
                    xo_ref, h_ref, lg_ref):
    ya = jax.nn.gelu(ga_ref[0]) * (hf_ref[0] + hb_ref[0])

    y = yf_ref[0] + yb_ref[0]
    avg = (_head_ones(B_WIDTH).astype(F32) * (1.0 / HEAD_DIM)).astype(BF16)
    mean = _dot_split(y, avg, 3)
    yc_ = y - mean
    var = _dot_split(yc_ * yc_, avg, 2)
    yn = yc_ * lax.rsqrt(var + GN_EPS) * lng_ref[...] + lnb_ref[...] + bvf_ref[0] + bvb_ref[0]
    gate_b = _dot_split(jax.nn.sigmoid(lo_ref[0][:, 2 * LORA:3 * LORA]), g2_ref[...], 2)
    yb = yn * gate_b

    mix = (_dot(ya.astype(BF16), wo_ref[:A_WIDTH]) + _dot(yb.astype(BF16), wo_ref[A_WIDTH:A_WIDTH + B_WIDTH])
           + _dot(yc_ref[0].astype(BF16), wo_ref[A_WIDTH + B_WIDTH:]))
    x = x_ref[0] + gate_ref[0] * mix
    xo_ref[0] = x
    hn = x * lax.rsqrt(jnp.mean(x * x, axis=-1, keepdims=True) + RMS_EPS) * g_ref[...]
    hn = hn * (1.0 + sc_ref[0]) + sh_ref[0]
    h_ref[0] = hn.reshape(h_ref.shape[1:])
    lg_ref[0] = _dot_split(hn, rw_ref[...], 2) + rb_ref[...]


def _outproj(x, hf, hb, ga, yf, yb, bvf, bvb, blo, yc, wo, g2, lng, lnb, gate, g, shift, scale, rw, rb, tm):
    bsz, t, d = x.shape
    ne = rw.shape[1]
    blk = lambda w: pl.BlockSpec((1, tm, w), lambda b, i: (b, i, 0))
    full = lambda a: pl.BlockSpec(a.shape, lambda b, i: (0,) * a.ndim)
    vec = pl.BlockSpec((1, 1, d), lambda b, i: (b, 0, 0))
    return pl.pallas_call(
        _outproj_kernel,
        grid=(bsz, t // tm),
        in_specs=[blk(d), blk(A_WIDTH), blk(A_WIDTH), blk(A_WIDTH), blk(B_WIDTH), blk(B_WIDTH),
                  blk(B_WIDTH), blk(B_WIDTH), blk(256), blk(C_WIDTH),
                  full(wo), full(g2), full(lng), full(lnb), vec, full(g), vec, vec, full(rw), full(rb)],
        out_specs=[blk(d), pl.BlockSpec((1, tm) + _row_tile(d), lambda b, i: (b, i, 0, 0)), blk(ne)],
        out_shape=[jax.ShapeDtypeStruct((bsz, t, d), F32), jax.ShapeDtypeStruct((bsz, t) + _row_tile(d), F32),
                   jax.ShapeDtypeStruct((bsz, t, ne), F32)],
        compiler_params=_params("arbitrary", "arbitrary"),
        name="outproj",
    )(x, hf, hb, ga, yf, yb, bvf, bvb, blo, yc, wo, g2, lng, lnb, gate, g, shift, scale, rw, rb)


def _expert_kernel(be_ref, nb_ref, x_ref, wgu_ref, bgu_ref, wdn_ref, bdn_ref, o_ref, wgu_s, wdn_s):
    i = pl.program_id(0)
    e = be_ref[i]
    fresh = jnp.logical_or(i == 0, be_ref[jnp.maximum(i - 1, 0)] != e)

    @pl.when(fresh)
    def _():
        wgu_s[...] = wgu_ref[0].astype(BF16)
        wdn_s[...] = wdn_ref[0].astype(BF16)

    @pl.when(i < nb_ref[0])
    def _():
        dff, d = wdn_s.shape
        x = x_ref[...].reshape(x_ref.shape[0], d)
        gu = _dot(x.astype(BF16), wgu_s[...]) + bgu_ref[0]
        g_ = jnp.minimum(gu[:, :dff], SWIGLU_LIMIT)
        u_ = jnp.clip(gu[:, dff:], -SWIGLU_LIMIT, SWIGLU_LIMIT)
        act = (u_ + 1.0) * (g_ * jax.nn.sigmoid(SWIGLU_ALPHA * g_))
        o_ref[...] = (_dot(act.astype(BF16), wdn_s[...]) + bdn_ref[0]).reshape(o_ref.shape)

    @pl.when(i >= nb_ref[0])
    def _():
        o_ref[...] = jnp.zeros_like(o_ref)


def _experts(block_e, n_used, xb, w_gu, b_gu, w_dn, b_dn, layer):
    cap = xb.shape[0]
    row = xb.shape[1:]
    depth, ne, d, f2 = w_gu.shape
    dff = w_dn.shape[2]
    nb = cap // MOE_BM
    w_gu, b_gu, w_dn, b_dn = (z.reshape((depth * ne,) + s) for z, s in
                              ((w_gu, (d, f2)), (b_gu, (1, f2)), (w_dn, (dff, d)), (b_dn, (1, d))))
    e0 = layer * ne
    grid_spec = pltpu.PrefetchScalarGridSpec(
        num_scalar_prefetch=2,
        grid=(nb,),
        in_specs=[pl.BlockSpec((MOE_BM,) + row, lambda i, be, nu: (i, 0, 0)),
                  pl.BlockSpec((1, d, f2), lambda i, be, nu: (e0 + be[i], 0, 0)),
                  pl.BlockSpec((1, 1, f2), lambda i, be, nu: (e0 + be[i], 0, 0)),
                  pl.BlockSpec((1, dff, d), lambda i, be, nu: (e0 + be[i], 0, 0)),
                  pl.BlockSpec((1, 1, d), lambda i, be, nu: (e0 + be[i], 0, 0))],
        out_specs=pl.BlockSpec((MOE_BM,) + row, lambda i, be, nu: (i, 0, 0)),
        scratch_shapes=[pltpu.VMEM((d, f2), BF16), pltpu.VMEM((dff, d), BF16)],
    )
    return pl.pallas_call(
        _expert_kernel, grid_spec=grid_spec,
        out_shape=jax.ShapeDtypeStruct((cap,) + row, F32),
        compiler_params=_params("arbitrary"), name="experts",
    )(block_e, n_used, xb, w_gu, b_gu, w_dn, b_dn)


def _route_kernel(lg_ref, idx_ref, gate_ref, rank_ref, cnt_ref, carry):
    i = pl.program_id(0)

    @pl.when(i == 0)
    def _():
        carry[...] = jnp.zeros_like(carry)

    l = lg_ref[...]
    tm, ne = l.shape
    lane = lax.broadcasted_iota(jnp.int32, l.shape, 1)
    vals, idxs, hots = [], [], []
    for _ in range(TOP_K):
        m = jnp.max(l, axis=-1, keepdims=True)
        ix = jnp.min(jnp.where(l == m, lane, ne), axis=-1, keepdims=True)
        hot = lane == ix
        vals.append(m)
        idxs.append(ix)
        hots.append(hot)
        l = jnp.where(hot, -jnp.inf, l)
    ex = [jnp.exp(v - vals[0]) for v in vals]
    den = ex[0] + ex[1] + ex[2] + ex[3]
    hot_all = (hots[0] | hots[1] | hots[2] | hots[3]).astype(BF16)
    tr = lax.broadcasted_iota(jnp.int32, (tm, tm), 0)
    tc = lax.broadcasted_iota(jnp.int32, (tm, tm), 1)
    before = _dot((tc < tr).astype(BF16), hot_all) + carry[...]
    ranks = [jnp.sum(jnp.where(h, before, 0.0), axis=-1, keepdims=True) for h in hots]
    carry[...] = carry[...] + jnp.sum(hot_all.astype(F32), axis=0, keepdims=True)
    idx_ref[...] = jnp.concatenate(idxs, axis=1)
    gate_ref[...] = jnp.concatenate([e / den for e in ex], axis=1)
    rank_ref[...] = jnp.concatenate(ranks, axis=1).astype(jnp.int32)
    cnt_ref[...] = carry[...].astype(jnp.int32)


def _route(logits):
    n, ne = logits.shape
    tm = ROUTE_TM
    assert n % tm == 0, (n, tm)
    blk = pl.BlockSpec((tm, TOP_K), lambda i: (i, 0))
    return pl.pallas_call(
        _route_kernel,
        grid=(n // tm,),
        in_specs=[pl.BlockSpec((tm, ne), lambda i: (i, 0))],
        out_specs=[blk, blk, blk, pl.BlockSpec((1, ne), lambda i: (0, 0))],
        out_shape=[jax.ShapeDtypeStruct((n, TOP_K), jnp.int32), jax.ShapeDtypeStruct((n, TOP_K), F32),
                   jax.ShapeDtypeStruct((n, TOP_K), jnp.int32), jax.ShapeDtypeStruct((1, ne), jnp.int32)],
        scratch_shapes=[pltpu.VMEM((1, ne), F32)],
        compiler_params=_params("arbitrary"),
        name="route",
    )(logits)


def _dispatch_kernel(dest_hbm, h_ref, xb_init, xb_hbm, dest_s, sem_idx, sem):
    del xb_init
    i = pl.program_id(0)
    tm = h_ref.shape[0]
    cp = pltpu.make_async_copy(dest_hbm.at[pl.ds(i * tm * TOP_K, tm * TOP_K)], dest_s, sem_idx)
    cp.start()
    cp.wait()

    def issue(t, carry):
        src = h_ref.at[t]
        for j in range(TOP_K):
            pltpu.make_async_copy(src, xb_hbm.at[dest_s[t * TOP_K + j]], sem).start()
        return carry

    lax.fori_loop(0, tm, issue, 0, unroll=8)
    for _ in range(TOP_K):
        pltpu.make_async_copy(h_ref, xb_hbm.at[pl.ds(0, tm)], sem).wait()


def _dispatch(dest_flat, h, xb_init):
    n = h.shape[0]
    row = h.shape[1:]
    cap = xb_init.shape[0]
    tm = DISPATCH_TM
    assert n % tm == 0, (n, tm)
    any_spec = pl.BlockSpec(memory_space=pl.ANY)
    return pl.pallas_call(
        _dispatch_kernel,
        grid=(n // tm,),
        in_specs=[any_spec, pl.BlockSpec((tm,) + row, lambda i: (i, 0, 0)), any_spec],
        out_specs=any_spec,
        out_shape=jax.ShapeDtypeStruct((cap,) + row, F32),
        scratch_shapes=[pltpu.SMEM((tm * TOP_K,), jnp.int32), pltpu.SemaphoreType.DMA, pltpu.SemaphoreType.DMA],
        input_output_aliases={2: 0},
        compiler_params=pltpu.CompilerParams(dimension_semantics=("arbitrary",), has_side_effects=True),
        name="dispatch",
    )(dest_flat, h, xb_init)


def _combine_kernel(dest_hbm, yb_hbm, x_ref, gt_ref, mg_ref, g_ref, o_ref, dest_s, buf, sem_idx, sem,
                    *, tok0, final):
    i = pl.program_id(0)
    n = pl.num_programs(0)
    tm = x_ref.shape[0]

    def fetch(blk, slot):
        cp = pltpu.make_async_copy(dest_hbm.at[pl.ds((tok0 + blk * tm) * TOP_K, tm * TOP_K)], dest_s.at[slot],
                                   sem_idx)
        cp.start()
        cp.wait()

        def issue(t, carry):
            for j in range(TOP_K):
                pltpu.make_async_copy(yb_hbm.at[dest_s[slot, t * TOP_K + j]], buf.at[slot, j, t],
                                      sem.at[slot]).start()
            return carry

        lax.fori_loop(0, tm, issue, 0, unroll=8)

    @pl.when(i == 0)
    def _():
        fetch(0, 0)

    @pl.when(i + 1 < n)
    def _():
        fetch(i + 1, (i + 1) % 2)

    slot = i % 2
    for j in range(TOP_K):
        pltpu.make_async_copy(yb_hbm.at[pl.ds(0, tm)], buf.at[slot, j], sem.at[slot]).wait()
    gt = gt_ref[...]
    y = gt[:, 0:1] * buf[slot, 0].reshape(x_ref.shape)
    for j in range(1, TOP_K):
        y = y + gt[:, j:j + 1] * buf[slot, j].reshape(x_ref.shape)
    x = x_ref[...] + mg_ref[0] * y
    if final:
        x = x * lax.rsqrt(jnp.mean(x * x, axis=-1, keepdims=True) + RMS_EPS) * g_ref[...]
    o_ref[...] = x


def _combine(dest_flat, yb, x, gates, mod_gate, g, tok0, final):
    bsz, t, d = x.shape
    n = bsz * t
    tm = min(COMBINE_TM, t)
    per_b = t // tm
    any_spec = pl.BlockSpec(memory_space=pl.ANY)
    out = pl.pallas_call(
        functools.partial(_combine_kernel, tok0=tok0, final=final),
        grid=(n // tm,),
        in_specs=[any_spec, any_spec,
                  pl.BlockSpec((tm, d), lambda i: (i, 0)),
                  pl.BlockSpec((tm, TOP_K), lambda i: (i, 0)),
                  pl.BlockSpec((1, 1, d), lambda i: (i // per_b, 0, 0)),
                  pl.BlockSpec((1, d), lambda i: (0, 0))],
        out_specs=pl.BlockSpec((tm, d), lambda i: (i, 0)),
        out_shape=jax.ShapeDtypeStruct((n, d), F32),
        scratch_shapes=[pltpu.SMEM((2, tm * TOP_K), jnp.int32), pltpu.VMEM((2, TOP_K, tm) + _row_tile(d), F32),
                        pltpu.SemaphoreType.DMA, pltpu.SemaphoreType.DMA((2,))],
        compiler_params=_params("arbitrary"),
        name="combine",
    )(dest_flat, yb, x.reshape(n, d), gates, mod_gate, g)
    return out.reshape(bsz, t, d)


def _moe_blocks(n_tokens, ne):
    return (n_tokens * TOP_K + ne * (MOE_BM - 1) + MOE_BM - 1) // MOE_BM


def _moe_plan(logits, nb):
    n, ne = logits.shape
    assert nb >= _moe_blocks(n, ne), (nb, n)
    idx, gates, rank, counts = _route(logits)
    counts = counts[0]
    padded = (counts + MOE_BM - 1) // MOE_BM * MOE_BM
    pad_end = jnp.cumsum(padded)
    pad_start = pad_end - padded
    hot = idx[:, :, None] == jnp.arange(ne, dtype=jnp.int32)[None, None, :]
    dest = jnp.sum(jnp.where(hot, pad_start[None, None, :], 0), axis=-1) + rank
    block_start = jnp.arange(nb, dtype=jnp.int32) * MOE_BM
    block_e = jnp.minimum(jnp.sum(pad_end[None, :] <= block_start[:, None], axis=1), ne - 1).astype(jnp.int32)
    n_used = (pad_end[-1] // MOE_BM).astype(jnp.int32).reshape(1)
    return dest.reshape(-1).astype(jnp.int32), gates, block_e, n_used


def _block_diag(w):
    nh, n, _ = w.shape
    eye = jnp.eye(nh, dtype=w.dtype)
    return (w[:, :, None, :] * eye[:, None, :, None]).reshape(nh * n, nh * n)


def _permute_in_cols(w_in):
    a0 = 0
    b0 = 2 * A_WIDTH
    c0 = b0 + 3 * B_WIDTH + 3 * LORA
    d = w_in.shape[0]
    pad = jnp.zeros((d, 256 - 3 * LORA), w_in.dtype)
    return jnp.concatenate([w_in[:, a0:a0 + A_WIDTH], w_in[:, b0:b0 + 3 * B_WIDTH],
                            w_in[:, b0 + 3 * B_WIDTH:c0], pad,
                            w_in[:, A_WIDTH:2 * A_WIDTH], w_in[:, c0:]], axis=1)


def _mixers(p_ctx, p_lat, lp, need_ctx):
    (xcf_c, xcb_c, ga_c, brkv_c, blo_c, q_c, k_c, v_c) = p_ctx
    (xcf_l, xcb_l, ga_l, brkv_l, blo_l, q_l, k_l, v_l) = p_lat
    bsz = xcf_c.shape[0]

    lru_w = (lp['wr_bd'], lp['br'], lp['wi_bd'], lp['bi'], lp['lam'])
    hf_c, hb_c, hend = _lru_scan(xcf_c, xcb_c, jnp.zeros((bsz, 2, A_WIDTH), F32), *lru_w)
    hf_l, hb_l, _ = _lru_scan(xcf_l, xcb_l, hend, *lru_w)

    rw_w = (lp['w0'], lp['w2'], lp['a0'], lp['a2'], lp['k_k'], lp['k_a'], lp['r_k'])
    yf_c, yb_c, bvf_c, bvb_c, mend = _rwkv_scan(brkv_c, blo_c, jnp.zeros((bsz, 2, B_WIDTH // HEAD_DIM, HEAD_DIM, HEAD_DIM), F32), *rw_w)
    yf_l, yb_l, bvf_l, bvb_l, _ = _rwkv_scan(brkv_l, blo_l, mend, *rw_w)

    yc_l = _na_attention(q_l, k_l, v_l, k_c, v_c, lp['bias_tab'])
    lat = (hf_l, hb_l, ga_l, yf_l, yb_l, bvf_l, bvb_l, blo_l, yc_l)
    ctx = None
    if need_ctx:
        yc_c = _ctx_attention(q_c, k_c, v_c)
        ctx = (hf_c, hb_c, ga_c, yf_c, yb_c, bvf_c, bvb_c, blo_c, yc_c)
    return ctx, lat


def kernel(x, c, ctx, c_ctx, ada_w, ada_b, norm_mix_g, norm_ffn_g, w_in, w_out, lru_conv_w, lru_conv_b, lru_wr, lru_br, lru_wi, lru_bi, lru_lambda, rwkv_mu, rwkv_w0, rwkv_w2, rwkv_a0, rwkv_a2, rwkv_g2, rwkv_kk, rwkv_ka, rwkv_rk, rwkv_lnx_g, rwkv_lnx_b, na_rpb, router_w, router_b, moe_w_gu, moe_b_gu, moe_w_dn, moe_b_dn, final_g):
    bsz, seq, d = x.shape
    ctx_len = ctx.shape[1]
    depth = ada_w.shape[0]
    rows = seq // GRID_W
    tm_l = 512
    tm_c = min(256, ctx_len)

    cond = jnp.concatenate([c, c_ctx[None], jnp.zeros((8 - bsz - 1, d), F32)], axis=0)
    mod = _adaln(cond, ada_w, ada_b)

    moe_nb = _moe_blocks(bsz * (ctx_len + seq), router_w.shape[2])
    xb = jnp.zeros((moe_nb * MOE_BM,) + _row_tile(d), F32)
    xl, xc = x, ctx
    for l in range(depth):
        last = l == depth - 1
        ml = [mod[l, :bsz, j * d:(j + 1) * d][:, None, :] for j in range(6)]
        mc = [jnp.broadcast_to(mod[l, bsz:bsz + 1, j * d:(j + 1) * d][:, None, :], (bsz, 1, d)) for j in range(6)]
        n_b = 3 * B_WIDTH
        mu = rwkv_mu[l]
        lp = dict(
            wr_bd=jnp.stack([_block_diag(lru_wr[l, dd]) for dd in range(2)]),
            wi_bd=jnp.stack([_block_diag(lru_wi[l, dd]) for dd in range(2)]),
            br=lru_br[l], bi=lru_bi[l], lam=lru_lambda[l],
            w0=rwkv_w0[l], w2=rwkv_w2[l], a0=rwkv_a0[l], a2=rwkv_a2[l],
            k_k=rwkv_kk[l][None], k_a=rwkv_ka[l][None], r_k=rwkv_rk[l].reshape(1, B_WIDTH),
            bias_tab=_na_bias_table(na_rpb[l], rows),
        )
        w_perm = _permute_in_cols(w_in[l]).astype(BF16)
        mu_perm = jnp.concatenate([mu[:, :n_b], mu[:, n_b:], jnp.zeros((2, 256 - 3 * LORA), F32)], axis=1)
        g_mix = norm_mix_g[l][None]
        g_ffn = norm_ffn_g[l][None]
        wo = w_out[l].astype(BF16)

        p_lat = _inproj(xl, g_mix, ml[0], ml[1], w_perm, lru_conv_w[l], lru_conv_b[l], mu_perm, tm_l)
        p_ctx = _inproj(xc, g_mix, mc[0], mc[1], w_perm, lru_conv_w[l], lru_conv_b[l], mu_perm, tm_c)
        mix_c, mix_l = _mixers(p_ctx, p_lat, lp, not last)

        fin = (wo, rwkv_g2[l], rwkv_lnx_g[l][None], rwkv_lnx_b[l][None])
        rt = (router_w[l], router_b[l][None])
        xl, hl, lg_l = _outproj(xl, *mix_l, *fin, ml[2], g_ffn, ml[3], ml[4], *rt, tm_l)
        ne = router_w.shape[2]
        if last:
            tok = hl.reshape((-1,) + _row_tile(d))
            lg = lg_l.reshape(-1, ne)
            nc = 0
        else:
            xc, hc, lg_c = _outproj(xc, *mix_c, *fin, mc[2], g_ffn, mc[3], mc[4], *rt, tm_c)
            tok = jnp.concatenate([hc.reshape((-1,) + _row_tile(d)), hl.reshape((-1,) + _row_tile(d))], axis=0)
            lg = jnp.concatenate([lg_c.reshape(-1, ne), lg_l.reshape(-1, ne)], axis=0)
            nc = bsz * ctx_len
        dest, gates, block_e, n_used = _moe_plan(lg, moe_nb)
        xb = _dispatch(dest, tok, xb)
        yb = _experts(block_e, n_used, xb, moe_w_gu, moe_b_gu, moe_w_dn, moe_b_dn, l)
        if not last:
            xc = _combine(dest, yb, xc, gates[:nc], mc[5], final_g[None], 0, False)
        xl = _combine(dest, yb, xl, gates[nc:], ml[5], final_g[None], nc, last)
    return xl
```
